```python
import math
import jax
import jax.numpy as jnp
from jax import lax
import numpy as np

D_MODEL = 1024
BATCH = 4
SEQ = 8192
DEPTH = 4

N_EVEN = (DEPTH + 1) // 2
N_ODD = DEPTH // 2
PLE_DIM = 256
NORM_EPS = 1e-6
F32 = jnp.float32

A_HEADS = 4
A_HEAD_DIM = 128
A_WIDTH = A_HEADS * A_HEAD_DIM
A_CONV = 5
A_CHUNK = 64
A_COLS = 4 * A_WIDTH + 4 * A_HEADS

B_HEADS = 8
B_HEAD_DIM = 64
B_WIDTH = B_HEADS * B_HEAD_DIM
B_DECAY_LORA = 32
B_AAA_LORA = 32
B_GATE_LORA = 96
B_GN_EPS = 64e-5
B_DECAY_SCALE = 0.606531
B_COLS = 3 * B_WIDTH + 2 * B_DECAY_LORA + 2 * B_AAA_LORA + B_GATE_LORA

EVEN_IN = A_COLS + B_COLS
EVEN_MIX = A_WIDTH + B_WIDTH

C_CHUNK = 128
C_GROUPS = 8
C_WIDTH = 1024
C_GROUP_DIM = C_WIDTH // C_GROUPS
C_LN_EPS = 1e-5

D_FF = -(-8 * D_MODEL // (3 * 256)) * 256

kernel_name = 'hybrid_deltanet_rwkv7_gmlp_encoder'


def rmsnorm(x, w, eps=NORM_EPS):
    xf = x.astype(F32)
    y = xf * lax.rsqrt(jnp.mean(xf * xf, axis=-1, keepdims=True) + eps)
    return (y * w).astype(x.dtype)


def layernorm(x, w, b, eps):
    xf = x.astype(F32)
    mu = jnp.mean(xf, axis=-1, keepdims=True)
    xc = xf - mu
    var = jnp.mean(xc * xc, axis=-1, keepdims=True)
    return (xc * lax.rsqrt(var + eps) * w + b).astype(x.dtype)


def l2norm(x):
    xf = x.astype(F32)
    return xf * lax.rsqrt(jnp.sum(xf * xf, axis=-1, keepdims=True) + 1e-6)


def dwconv_centred(x, w):
    K = w.shape[0]
    return lax.conv_general_dilated(x, w[:, None, :], window_strides=(1,), padding=[(K // 2, K // 2)],
                                    dimension_numbers=('NWC', 'WIO', 'NWC'), feature_group_count=x.shape[-1])


def centred_shift(x):
    xp = jnp.pad(x, ((0, 0), (1, 1), (0, 0)))
    return 0.5 * (xp[:, :-2] + xp[:, 2:])


def stack_dirs(fwd, bwd):
    return jnp.concatenate([fwd, jnp.flip(bwd, axis=1)], axis=0)


def merge_dirs(y, n):
    return y[:n] + jnp.flip(y[n:], axis=1)


def delta_chunk_step(S, inp):
    u, w, qk, q_dec, k_dec, g_last = inp
    v_new = u - jnp.einsum('bhcd,bhde->bhce', w, S)
    o = jnp.einsum('bhcd,bhde->bhce', q_dec, S) + jnp.einsum('bhij,bhje->bhie', qk, v_new)
    S = S * jnp.exp(g_last)[..., None, None] + jnp.einsum('bhcd,bhce->bhde', k_dec, v_new)
    return S, o


def gated_delta_chunked(q, k, v, g, beta):
    Bd, T, H, DK = q.shape
    DV = v.shape[-1]
    N = T // A_CHUNK

    def chunk(t):
        t = t.astype(F32).reshape((Bd, N, A_CHUNK, H) + t.shape[3:])
        return jnp.moveaxis(t, 3, 1)

    q = chunk(q) * (DK ** -0.5)
    k, v, g, beta = chunk(k), chunk(v), chunk(g), chunk(beta)
    gc = jnp.cumsum(g, axis=-1)
    incl = jnp.tril(jnp.ones((A_CHUNK, A_CHUNK), bool))
    strict = jnp.tril(jnp.ones((A_CHUNK, A_CHUNK), bool), -1)
    diff = gc[..., :, None] - gc[..., None, :]
    decay = jnp.where(incl, jnp.exp(jnp.where(incl, diff, 0.0)), 0.0)
    m = jnp.where(strict, beta[..., :, None] * jnp.einsum('bhnid,bhnjd->bhnij', k, k) * decay, 0.0)
    eye = jnp.eye(A_CHUNK, dtype=F32)
    t_inv = lax.linalg.triangular_solve(eye + m, jnp.broadcast_to(eye, m.shape), left_side=True, lower=True)
    u = jnp.einsum('bhnij,bhnjd->bhnid', t_inv, v * beta[..., None])
    w = jnp.einsum('bhnij,bhnjd->bhnid', t_inv, k * (beta * jnp.exp(gc))[..., None])
    qk = jnp.einsum('bhnid,bhnjd->bhnij', q, k) * decay
    q_dec = q * jnp.exp(gc)[..., None]
    g_last = gc[..., -1]
    k_dec = k * jnp.exp(g_last[..., None] - gc)[..., None]
    xs = tuple(jnp.moveaxis(t, 2, 0) for t in (u, w, qk, q_dec, k_dec, g_last))
    S0 = jnp.zeros((Bd, H, DK, DV), F32)
    _, o = lax.scan(delta_chunk_step, S0, xs)
    return jnp.transpose(o, (1, 0, 3, 2, 4)).reshape(Bd, T, H, DV)


def gated_deltanet_mixer(pa, conv_w, a_log, dt_bias, norm_w):
    Bb, T, _ = pa.shape
    heads = lambda t: t.reshape(t.shape[:-1] + (A_HEADS, A_HEAD_DIM))
    qkv = jax.nn.silu(dwconv_centred(pa[..., :3 * A_WIDTH], conv_w))
    q = l2norm(heads(qkv[..., :A_WIDTH]))
    k = l2norm(heads(qkv[..., A_WIDTH:2 * A_WIDTH]))
    v = heads(qkv[..., 2 * A_WIDTH:])
    z = heads(pa[..., 3 * A_WIDTH:4 * A_WIDTH])
    ab = pa[..., 4 * A_WIDTH:].astype(F32).reshape(Bb, T, 4, A_HEADS)
    g = -jnp.exp(a_log.astype(F32)) * jax.nn.softplus(ab[:, :, :2] + dt_bias)
    beta = jax.nn.sigmoid(ab[:, :, 2:])
    o = gated_delta_chunked(stack_dirs(q, q), stack_dirs(k, k), stack_dirs(v, v),
                            stack_dirs(g[:, :, 0], g[:, :, 1]), stack_dirs(beta[:, :, 0], beta[:, :, 1]))
    o = merge_dirs(o, Bb)
    o = rmsnorm(o, norm_w) * jax.nn.silu(z)
    return o.reshape(Bb, T, A_WIDTH).astype(pa.dtype)


def rwkv7_step(S, inp):
    r, w, k, v, kk, a = inp
    sa = jnp.einsum('bhvk,bhk->bhv', S, kk)
    S = S * w[:, :, None, :] - sa[..., None] * (kk * a)[:, :, None, :] + v[..., None] * k[:, :, None, :]
    return S, jnp.einsum('bhvk,bhk->bhv', S, r)


def rwkv7_mixer(pb, mu, w0, w2, a0, a2, g2, k_k, k_a, r_k, ln_w, ln_b):
    Bb, T, _ = pb.shape
    heads = lambda t: t.reshape(t.shape[:-1] + (B_HEADS, B_HEAD_DIM))
    pb = pb + (centred_shift(pb) - pb) * mu
    r = pb[..., :B_WIDTH]
    k = pb[..., B_WIDTH:2 * B_WIDTH]
    v = pb[..., 2 * B_WIDTH:3 * B_WIDTH]
    o = 3 * B_WIDTH
    wd = pb[..., o:o + 2 * B_DECAY_LORA].reshape(Bb, T, 2, B_DECAY_LORA)
    o += 2 * B_DECAY_LORA
    ad = pb[..., o:o + 2 * B_AAA_LORA].reshape(Bb, T, 2, B_AAA_LORA)
    o += 2 * B_AAA_LORA
    gd = pb[..., o:]
    w = jnp.exp(-B_DECAY_SCALE * jax.nn.sigmoid((w0 + jnp.einsum('btel,elc->btec', jnp.tanh(wd), w2)).astype(F32)))
    a = jax.nn.sigmoid(a0 + jnp.einsum('btel,elc->btec', ad, a2))
    g = jax.nn.sigmoid(gd) @ g2
    kk = l2norm(heads(k * k_k))
    k_dir = k[:, :, None, :] * (1.0 + (a - 1.0) * k_a)
    r_h, v_h = heads(r), heads(v)
    xs = (stack_dirs(r_h, r_h),
          stack_dirs(heads(w[:, :, 0]), heads(w[:, :, 1])),
          stack_dirs(heads(k_dir[:, :, 0]), heads(k_dir[:, :, 1])),
          stack_dirs(v_h, v_h),
          stack_dirs(kk, kk),
          stack_dirs(heads(a[:, :, 0]), heads(a[:, :, 1])))
    xs = tuple(jnp.moveaxis(t.astype(F32), 1, 0) for t in xs)
    S0 = jnp.zeros((2 * Bb, B_HEADS, B_HEAD_DIM, B_HEAD_DIM), F32)
    _, y = lax.scan(rwkv7_step, S0, xs)
    y = merge_dirs(jnp.moveaxis(y, 0, 1), Bb)
    y = layernorm(y, heads(ln_w), heads(ln_b), B_GN_EPS)
    k_b = heads(0.5 * (k_dir[:, :, 0] + k_dir[:, :, 1])).astype(F32)
    bonus = jnp.sum(r_h * k_b * r_k, axis=-1, keepdims=True) * v_h
    return ((y + bonus).reshape(Bb, T, B_WIDTH) * g).astype(pb.dtype)


def even_mixer(h, w_in, w_out, a_conv, a_log, a_dt_bias, a_norm, b_mu, b_w0, b_w2, b_a0, b_a2, b_g2,
               b_k_k, b_k_a, b_r_k, b_ln_w, b_ln_b):
    proj = h @ w_in
    ya = gated_deltanet_mixer(proj[..., :A_COLS], a_conv, a_log, a_dt_bias, a_norm)
    yb = rwkv7_mixer(proj[..., A_COLS:], b_mu, b_w0, b_w2, b_a0, b_a2, b_g2, b_k_k, b_k_a, b_r_k, b_ln_w, b_ln_b)
    return jnp.concatenate([ya, yb], axis=-1) @ w_out


def odd_mixer(h, w_in, ln_w, ln_b, ws, bs, w_out):
    Bb, T, _ = h.shape
    uv = jax.nn.gelu(h @ w_in)
    u, v = uv[..., :C_WIDTH], uv[..., C_WIDTH:]
    v = layernorm(v, ln_w, ln_b, C_LN_EPS).reshape(Bb, T // C_CHUNK, C_CHUNK, C_GROUPS, C_GROUP_DIM)
    sv = jnp.einsum('gij,bnjgd->bnigd', ws, v) + bs.T[None, None, :, :, None]
    return (u * sv.reshape(Bb, T, C_WIDTH)) @ w_out


def swiglu(h, w_gate, w_up, w_down):
    return (jax.nn.silu(h @ w_gate) * (h @ w_up)) @ w_down


def setup_inputs(seed: int = 0) -> dict:
    key = jax.random.key(seed)
    ks = iter(jax.random.split(key, 48))

    def nrm(shape, scale):
        return jax.random.normal(next(ks), shape, F32) * scale

    def gain(shape):
        return 1.0 + nrm(shape, 0.05)

    def unif(shape, lo, hi):
        return jax.random.uniform(next(ks), shape, F32, lo, hi)

    dt = jnp.exp(unif((N_EVEN, 2, A_HEADS), math.log(1e-3), math.log(1e-1)))
    return {
        'x': nrm((BATCH, SEQ, D_MODEL), 1.0),
        'p': nrm((DEPTH, BATCH, SEQ, PLE_DIM), 1.0),
        'norm_mix': gain((DEPTH, D_MODEL)),
        'norm_ffn': gain((DEPTH, D_MODEL)),
        'norm_ple': gain((DEPTH, D_MODEL)),
        'norm_final': gain((D_MODEL,)),
        'w_in_even': nrm((N_EVEN, D_MODEL, EVEN_IN), D_MODEL ** -0.5),
        'w_out_even': nrm((N_EVEN, EVEN_MIX, D_MODEL), EVEN_MIX ** -0.5),
        'a_conv': nrm((N_EVEN, A_CONV, 3 * A_WIDTH), A_CONV ** -0.5),
        'a_log': jnp.log(unif((N_EVEN, 2, A_HEADS), 1.0, 16.0)),
        'a_dt_bias': dt + jnp.log(-jnp.expm1(-dt)),
        'a_norm': gain((N_EVEN, A_HEAD_DIM)),
        'b_mu': unif((N_EVEN, B_COLS), 0.0, 1.0),
        'b_w0': nrm((N_EVEN, 2, B_WIDTH), 0.5),
        'b_w2': nrm((N_EVEN, 2, B_DECAY_LORA, B_WIDTH), B_DECAY_LORA ** -0.5),
        'b_a0': nrm((N_EVEN, 2, B_WIDTH), 0.1),
        'b_a2': nrm((N_EVEN, 2, B_AAA_LORA, B_WIDTH), B_AAA_LORA ** -0.5),
        'b_g2': nrm((N_EVEN, B_GATE_LORA, B_WIDTH), B_GATE_LORA ** -0.5),
        'b_k_k': 0.85 + nrm((N_EVEN, B_WIDTH), 0.05),
        'b_k_a': gain((N_EVEN, B_WIDTH)),
        'b_r_k': nrm((N_EVEN, B_HEADS, B_HEAD_DIM), 0.1),
        'b_ln_w': gain((N_EVEN, B_WIDTH)),
        'b_ln_b': nrm((N_EVEN, B_WIDTH), 0.02),
        'w_in_odd': nrm((N_ODD, D_MODEL, 2 * C_WIDTH), D_MODEL ** -0.5),
        'c_ln_w': gain((N_ODD, C_WIDTH)),
        'c_ln_b': nrm((N_ODD, C_WIDTH), 0.02),
        'c_ws': nrm((N_ODD, C_GROUPS, C_CHUNK, C_CHUNK), C_CHUNK ** -0.5),
        'c_bs': 1.0 + nrm((N_ODD, C_GROUPS, C_CHUNK), 0.1),
        'w_out_odd': nrm((N_ODD, C_WIDTH, D_MODEL), C_WIDTH ** -0.5),
        'w_gate': nrm((DEPTH, D_MODEL, D_FF), D_MODEL ** -0.5),
        'w_up': nrm((DEPTH, D_MODEL, D_FF), D_MODEL ** -0.5),
        'w_down': nrm((DEPTH, D_FF, D_MODEL), D_FF ** -0.5),
        'w_ple': nrm((DEPTH, PLE_DIM, D_MODEL), PLE_DIM ** -0.5),
        'w_ple_gate': nrm((DEPTH, D_MODEL, D_MODEL), D_MODEL ** -0.5),
    }


def reference(x, p, norm_mix, norm_ffn, norm_ple, norm_final, w_in_even, w_out_even, a_conv, a_log,
              a_dt_bias, a_norm, b_mu, b_w0, b_w2, b_a0, b_a2, b_g2, b_k_k, b_k_a, b_r_k, b_ln_w, b_ln_b,
              w_in_odd, c_ln_w, c_ln_b, c_ws, c_bs, w_out_odd, w_gate, w_up, w_down, w_ple, w_ple_gate):
    h = x
    for i in range(DEPTH):
        j = i // 2
        hn = rmsnorm(h, norm_mix[i])
        if i % 2 == 0:
            h = h + even_mixer(hn, w_in_even[j], w_out_even[j], a_conv[j], a_log[j], a_dt_bias[j], a_norm[j],
                               b_mu[j], b_w0[j], b_w2[j], b_a0[j], b_a2[j], b_g2[j], b_k_k[j], b_k_a[j],
                               b_r_k[j], b_ln_w[j], b_ln_b[j])
        else:
            h = h + odd_mixer(hn, w_in_odd[j], c_ln_w[j], c_ln_b[j], c_ws[j], c_bs[j], w_out_odd[j])
        h = h + swiglu(rmsnorm(h, norm_ffn[i]), w_gate[i], w_up[i], w_down[i])
        h = h + (p[i] @ w_ple[i]) * jax.nn.sigmoid(rmsnorm(h, norm_ple[i]) @ w_ple_gate[i])
    return rmsnorm(h, norm_final)
```

```python
import functools
import math

import jax
import jax.numpy as jnp
from jax import lax
from jax.experimental import pallas as pl
from jax.experimental.pallas import tpu as pltpu

F32 = jnp.float32
BF16 = jnp.bfloat16

NORM_EPS = 1e-6
L2_EPS = 1e-6
B_GN_EPS = 64e-5
B_DECAY_SCALE = 0.606531
C_LN_EPS = 1e-5

LANES = 128
SUBLANES = 8
CHUNK = 64
UNIT = 2 * CHUNK
INV_BLOCK = 16
HALO = SUBLANES
VMEM_BUDGET = 56 * 1024 * 1024


def _params(sem, *block_bytes, scratch=0, temps=0):
    need = 2 * sum(block_bytes) + scratch + temps
    return pltpu.CompilerParams(dimension_semantics=sem, vmem_limit_bytes=int(min(max(need, 16 << 20), VMEM_BUDGET)))


def _dot(a, b):
    return jnp.dot(a.astype(BF16), b.astype(BF16), preferred_element_type=F32)


def _dot_nt(a, b):
    return lax.dot_general(a.astype(BF16), b.astype(BF16), (((1,), (1,)), ((), ())), preferred_element_type=F32)


def _split3(x):
    hi = x.astype(BF16)
    r1 = x - hi.astype(F32)
    mid = r1.astype(BF16)
    lo = (r1 - mid.astype(F32)).astype(BF16)
    return hi, mid, lo


def _dot_sel(sel, x):
    hi, mid, lo = _split3(x)
    d = lambda p: jnp.dot(sel, p, preferred_element_type=F32)
    return d(hi) + d(mid) + d(lo)


def _dot_sel_r(x, sel):
    hi, mid, lo = _split3(x)
    d = lambda p: jnp.dot(p, sel, preferred_element_type=F32)
    return d(hi) + d(mid) + d(lo)


def _sigmoid(x):
    return 1.0 / (1.0 + jnp.exp(-x))


def _silu(x):
    return x * _sigmoid(x)


def _softplus(x):
    return jnp.maximum(x, 0.0) + jnp.log(1.0 + jnp.exp(-jnp.abs(x)))


def _gelu_tanh(x):
    return 0.5 * x * (1.0 + jnp.tanh(math.sqrt(2.0 / math.pi) * (x + 0.044715 * (x * x * x))))


def _rms(x, w, eps=NORM_EPS):
    return x * lax.rsqrt(jnp.mean(x * x, axis=-1, keepdims=True) + eps) * w


def _iota2(shape, dim):
    return lax.broadcasted_iota(jnp.int32, shape, dim)


def _unit_masks(upper_top, upper_bot):
    r = _iota2((UNIT, UNIT), 0)
    c = _iota2((UNIT, UNIT), 1)
    same = (r // CHUNK) == (c // CHUNK)
    sign = jnp.where(r < CHUNK, -1 if upper_top else 1, -1 if upper_bot else 1)
    d = jnp.where(same, (r - c) * sign, -1)
    incl = d >= 0
    strict = d > 0
    blk = (r // INV_BLOCK) == (c // INV_BLOCK)
    eye = jnp.where(r == c, 1.0, 0.0).astype(F32)
    return incl, strict, blk, same, eye


def _inv_i_minus(a, blk, eye):
    ad = jnp.where(blk, a, 0.0)
    ao = a - ad
    p = ad
    t = eye + p
    for _ in range(int(math.log2(INV_BLOCK)) - 1):
        p = _dot(p, p)
        t = t + _dot(t, p)
    x = _dot(t, ao)
    assert CHUNK // INV_BLOCK == 4
    x2 = _dot(x, x)
    s = eye + x + x2 + _dot(x, x2)
    return _dot(s, t)


def _fill_halo(xe_ref, x_ref, prev_ref, next_ref, i, nblk):
    rows = x_ref.shape[1]
    pf = jnp.where(i > 0, 1.0, 0.0).astype(F32)
    nf = jnp.where(i < nblk - 1, 1.0, 0.0).astype(F32)
    xe_ref[0:HALO, :] = prev_ref[0] * pf
    xe_ref[HALO:HALO + rows, :] = x_ref[0]
    xe_ref[HALO + rows:2 * HALO + rows, :] = next_ref[0] * nf


def _halo_specs(rows, width, t_len):
    per = rows // HALO
    last = t_len // HALO - 1
    return [
        pl.BlockSpec((1, rows, width), lambda b, i: (b, i, 0)),
        pl.BlockSpec((1, HALO, width), lambda b, i: (b, jnp.maximum(i * per - 1, 0), 0)),
        pl.BlockSpec((1, HALO, width), lambda b, i: (b, jnp.minimum((i + 1) * per, last), 0)),
    ]


def _const_spec(shape):
    nd = len(shape)
    return pl.BlockSpec(shape, lambda *_: (0,) * nd)


def _proj_kernel(h_ref, nw_ref, w_ref, *o_refs, splits):
    hn = _rms(h_ref[...], nw_ref[...]).astype(BF16)
    off = 0
    for o_ref, n in zip(o_refs, splits):
        o_ref[...] = jnp.dot(hn, w_ref[:, off:off + n], preferred_element_type=F32)
        off += n


def _proj(h2, nw, w, splits, tm=512):
    m, d = h2.shape
    n = w.shape[1]
    return pl.pallas_call(
        functools.partial(_proj_kernel, splits=splits),
        grid=(m // tm,),
        in_specs=[pl.BlockSpec((tm, d), lambda i: (i, 0)), _const_spec((1, d)), _const_spec((d, n))],
        out_specs=[pl.BlockSpec((tm, s), lambda i: (i, 0)) for s in splits],
        out_shape=[jax.ShapeDtypeStruct((m, s), F32) for s in splits],
        compiler_params=_params(("parallel",), tm * d * 4, d * n * 2, tm * n * 4, temps=tm * d * 8),
        name="proj",
    )(h2, nw, w)


def _ffn_kernel(h_ref, nw_ref, wg_ref, wu_ref, wd_ref, o_ref, hn_ref, acc_ref):
    f = pl.program_id(1)

    @pl.when(f == 0)
    def _():
        hn_ref[...] = _rms(h_ref[...], nw_ref[...]).astype(BF16)
        acc_ref[...] = h_ref[...]

    hn = hn_ref[...]
    g = jnp.dot(hn, wg_ref[...], preferred_element_type=F32)
    u = jnp.dot(hn, wu_ref[...], preferred_element_type=F32)
    acc_ref[...] += jnp.dot((_silu(g) * u).astype(BF16), wd_ref[...], preferred_element_type=F32)

    @pl.when(f == pl.num_programs(1) - 1)
    def _():
        o_ref[...] = acc_ref[...]


def _ffn(h2, nw, wg, wu, wd, tm=1024, tf=256):
    m, d = h2.shape
    ff = wg.shape[1]
    return pl.pallas_call(
        _ffn_kernel,
        grid=(m // tm, ff // tf),
        in_specs=[
            pl.BlockSpec((tm, d), lambda i, f: (i, 0)),
            _const_spec((1, d)),
            pl.BlockSpec((d, tf), lambda i, f: (0, f)),
            pl.BlockSpec((d, tf), lambda i, f: (0, f)),
            pl.BlockSpec((tf, d), lambda i, f: (f, 0)),
        ],
        out_specs=pl.BlockSpec((tm, d), lambda i, f: (i, 0)),
        out_shape=jax.ShapeDtypeStruct((m, d), F32),
        scratch_shapes=[pltpu.VMEM((tm, d), BF16), pltpu.VMEM((tm, d), F32)],
        compiler_params=_params(("parallel", "arbitrary"), tm * d * 4, 3 * d * tf * 2, tm * d * 4,
                                scratch=tm * d * 6, temps=3 * tm * tf * 4),
        name="ffn",
    )(h2, nw, wg, wu, wd)


def _ple_kernel(h_ref, p_ref, nw_ref, wp_ref, wg_ref, *rest, final):
    if final:
        nf_ref, o_ref = rest
    else:
        (o_ref,) = rest
    h = h_ref[...]
    gate = _sigmoid(jnp.dot(_rms(h, nw_ref[...]).astype(BF16), wg_ref[...], preferred_element_type=F32))
    e = jnp.dot(p_ref[...].astype(BF16), wp_ref[...], preferred_element_type=F32)
    out = h + e * gate
    if final:
        out = _rms(out, nf_ref[...])
    o_ref[...] = out


def _ple(h2, p2, nw, wp, wg, nf=None, tm=512):
    m, d = h2.shape
    pd = p2.shape[1]
    final = nf is not None
    ins = [h2, p2, nw, wp, wg] + ([nf] if final else [])
    specs = [pl.BlockSpec((tm, d), lambda i: (i, 0)), pl.BlockSpec((tm, pd), lambda i: (i, 0)),
             _const_spec((1, d)), _const_spec((pd, d)), _const_spec((d, d))]
    if final:
        specs.append(_const_spec((1, d)))
    return pl.pallas_call(
        functools.partial(_ple_kernel, final=final),
        grid=(m // tm,),
        in_specs=specs,
        out_specs=pl.BlockSpec((tm, d), lambda i: (i, 0)),
        out_shape=jax.ShapeDtypeStruct((m, d), F32),
        compiler_params=_params(("parallel",), tm * d * 4, tm * pd * 4, pd * d * 2, d * d * 2, tm * d * 4,
                                temps=4 * tm * d * 4),
        name="ple",
    )(*ins)


def _gmlp_kernel(h_ref, nw_ref, wi_ref, lnw_ref, lnb_ref, ws_ref, bs_ref, wo_ref, o_ref, *, chunk, groups):
    h = h_ref[...]
    rows, d = h.shape
    cw = wo_ref.shape[0]
    gd = cw // groups
    hn = _rms(h, nw_ref[...]).astype(BF16)
    uv = _gelu_tanh(jnp.dot(hn, wi_ref[...], preferred_element_type=F32))
    u = uv[:, :cw]
    v = uv[:, cw:]
    mu = jnp.mean(v, axis=-1, keepdims=True)
    vc = v - mu
    var = jnp.mean(vc * vc, axis=-1, keepdims=True)
    vn = (vc * lax.rsqrt(var + C_LN_EPS) * lnw_ref[...] + lnb_ref[...]).astype(BF16)
    bs = bs_ref[...]
    pieces = []
    for c in range(rows // chunk):
        vrow = vn[c * chunk:(c + 1) * chunk]
        sv = jnp.concatenate(
            [jnp.dot(ws_ref[g], vrow[:, g * gd:(g + 1) * gd], preferred_element_type=F32) for g in range(groups)],
            axis=1)
        pieces.append((u[c * chunk:(c + 1) * chunk] * (sv + bs)).astype(BF16))
    gated = jnp.concatenate(pieces, axis=0)
    o_ref[...] = h + jnp.dot(gated, wo_ref[...], preferred_element_type=F32)


def _gmlp(h2, nw, wi, lnw, lnb, ws, bs_full, wo, tm=512):
    m, d = h2.shape
    groups, chunk, _ = ws.shape
    cw = wo.shape[0]
    return pl.pallas_call(
        functools.partial(_gmlp_kernel, chunk=chunk, groups=groups),
        grid=(m // tm,),
        in_specs=[pl.BlockSpec((tm, d), lambda i: (i, 0)), _const_spec((1, d)), _const_spec(wi.shape),
                  _const_spec((1, cw)), _const_spec((1, cw)), _const_spec(ws.shape), _const_spec(bs_full.shape),
                  _const_spec(wo.shape)],
        out_specs=pl.BlockSpec((tm, d), lambda i: (i, 0)),
        out_shape=jax.ShapeDtypeStruct((m, d), F32),
        compiler_params=_params(("parallel",), tm * d * 4, wi.size * 2, ws.size * 2, bs_full.size * 4, wo.size * 2,
                                tm * d * 4, temps=5 * tm * 2 * cw * 4),
        name="gmlp",
    )(h2, nw, wi, lnw, lnb, ws, bs_full, wo)


def _delta_prep_kernel(x_ref, xp_ref, xn_ref, ab_ref, cw_ref, alog_ref, dtb_ref,
                       wq_ref, u_ref, aq_ref, bg_ref,
                       xe_ref, q_scr, k_scr, v_scr, g_scr, b_scr, *, heads, taps):
    i = pl.program_id(1)
    rows = x_ref.shape[1]
    width = x_ref.shape[2]
    aw = width // 3
    hd = aw // heads
    _fill_halo(xe_ref, x_ref, xp_ref, xn_ref, i, pl.num_programs(1))

    acc = jnp.zeros((rows, width), F32)
    for j in range(taps):
        s = HALO - taps // 2 + j
        acc = acc + cw_ref[j:j + 1, :] * xe_ref[s:s + rows, :]
    y = _silu(acc)
    for hh in range(heads):
        qh = y[:, hh * hd:(hh + 1) * hd]
        kh = y[:, aw + hh * hd:aw + (hh + 1) * hd]
        q_scr[hh] = qh * lax.rsqrt(jnp.sum(qh * qh, axis=-1, keepdims=True) + L2_EPS) * (hd ** -0.5)
        k_scr[hh] = kh * lax.rsqrt(jnp.sum(kh * kh, axis=-1, keepdims=True) + L2_EPS)
        v_scr[hh] = y[:, 2 * aw + hh * hd:2 * aw + (hh + 1) * hd]
    ab = ab_ref[0]
    g_scr[...] = -jnp.exp(alog_ref[...]) * _softplus(ab + dtb_ref[...])
    b_scr[...] = _sigmoid(ab)

    incl, strict, blk, _, eye = _unit_masks(False, True)
    r1 = _iota2((UNIT, 1), 0)
    top_col = r1 < CHUNK
    top_row = _iota2((1, UNIT), 1) < CHUNK
    tri_r = _iota2((UNIT, CHUNK), 0)
    tri_c = _iota2((UNIT, CHUNK), 1)
    cum_sel = jnp.where(jnp.where(tri_r < CHUNK, tri_r - tri_c, tri_c - tri_r + CHUNK) >= 0, 1.0, 0.0).astype(BF16)

    def chunk_body(c, carry):
        r0 = pl.multiple_of(c * CHUNK, CHUNK)
        gst = _dot_sel(cum_sel, g_scr[pl.ds(r0, CHUNK), :])
        gst_t = gst.T
        bch = b_scr[pl.ds(r0, CHUNK), :]
        bst = jnp.concatenate([bch, bch], axis=0)
        for hh in range(heads):
            fw, bw = hh, heads + hh
            gcs = jnp.where(top_col, gst[:, fw:fw + 1], gst[:, bw:bw + 1])
            grow = jnp.where(top_row, gst_t[fw:fw + 1, :], gst_t[bw:bw + 1, :])
            bcol = jnp.where(top_col, bst[:, 2 * heads + fw:2 * heads + fw + 1],
                             bst[:, 2 * heads + bw:2 * heads + bw + 1])
            glast = jnp.where(top_col, gst[CHUNK - 1:CHUNK, fw:fw + 1], gst[CHUNK:CHUNK + 1, bw:bw + 1])
            q = q_scr[hh, pl.ds(r0, CHUNK), :]
            k = k_scr[hh, pl.ds(r0, CHUNK), :]
            v = v_scr[hh, pl.ds(r0, CHUNK), :]
            qst = jnp.concatenate([q, q], axis=0)
            kst = jnp.concatenate([k, k], axis=0)
            vst = jnp.concatenate([v, v], axis=0)
            decay = jnp.where(incl, jnp.exp(jnp.where(incl, gcs - grow, 0.0)), 0.0)
            kk = _dot_nt(kst, kst)
            m = jnp.where(strict, bcol * kk * decay, 0.0)
            t = _inv_i_minus(-m, blk, eye)
            eg = jnp.exp(gcs)
            uw = _dot(t, jnp.concatenate([vst * bcol, kst * (bcol * eg)], axis=1))
            qk = _dot_nt(qst, kst) * decay
            qd = qst * eg
            kdec = kst * jnp.exp(glast - gcs)
            gam = jnp.exp(glast)
            for dr in range(2):
                sl = slice(dr * CHUNK, (dr + 1) * CHUNK)
                wq_ref[dr, 0, hh, c] = jnp.concatenate([-uw[sl, hd:], qd[sl]], axis=0).astype(BF16)
                u_ref[dr, 0, hh, c] = uw[sl, :hd]
                aq_ref[dr, 0, hh, c] = qk[sl].astype(BF16)
                bg_ref[dr, 0, hh, c] = jnp.concatenate(
                    [kdec[sl], jnp.broadcast_to(gam[sl], (CHUNK, hd))], axis=0).T
        return carry

    lax.fori_loop(0, rows // CHUNK, chunk_body, 0)


def _delta_prep(qkv, ab, conv_w, alog_row, dtb_row, heads, rows=256):
    b, t, width = qkv.shape
    hd = width // 3 // heads
    assert hd == LANES
    n = t // CHUNK
    nb = rows // CHUNK
    taps = conv_w.shape[0]
    cw = jnp.zeros((SUBLANES, width), F32).at[:taps].set(conv_w)
    unit = lambda r, dt: jax.ShapeDtypeStruct((2, b, heads, n, r, LANES), dt)
    ospec = lambda r: pl.BlockSpec((2, 1, heads, nb, r, LANES), lambda bi, i: (0, bi, 0, i, 0, 0))
    return pl.pallas_call(
        functools.partial(_delta_prep_kernel, heads=heads, taps=taps),
        grid=(b, t // rows),
        in_specs=_halo_specs(rows, width, t) + [
            pl.BlockSpec((1, rows, LANES), lambda bi, i: (bi, i, 0)),
            _const_spec((SUBLANES, width)), _const_spec((1, LANES)), _const_spec((1, LANES))],
        out_specs=[ospec(UNIT), ospec(CHUNK), ospec(CHUNK), ospec(UNIT)],
        out_shape=[unit(UNIT, BF16), unit(CHUNK, F32), unit(CHUNK, BF16), unit(UNIT, F32)],
        scratch_shapes=[pltpu.VMEM((rows + 2 * HALO, width), F32)]
        + [pltpu.VMEM((heads, rows, LANES), F32)] * 3 + [pltpu.VMEM((rows, LANES), F32)] * 2,
        compiler_params=_params(("parallel", "arbitrary"), rows * width * 4, rows * LANES * 4,
                                2 * heads * nb * (UNIT * 2 + CHUNK * 4 + CHUNK * 2 + UNIT * 4) * LANES,
                                scratch=(2 * rows + 2 * HALO) * width * 4, temps=4 * rows * width * 4),
        name="delta_prep",
    )(qkv, qkv, qkv, ab, cw, alog_row, dtb_row)


def _rwkv_prep_kernel(x_ref, xp_ref, xn_ref, l_ref, lp_ref, ln_ref, mux_ref, mul_ref, wcat_ref, bias_ref, g2_ref,
                      kk_w_ref, ka_ref, rk_ref,
                      wq_ref, u0_ref, y0_ref, aq_ref, bg_ref, kv0_ref, bonus_ref, gate_ref,
                      xe_ref, le_ref, r_scr, v_scr, kk_scr, lw_scr, kka_scr, kd_scr, *, lora_w, lora_a):
    i = pl.program_id(1)
    nblk = pl.num_programs(1)
    rows = x_ref.shape[1]
    bw = x_ref.shape[2] // 3
    pairs = bw // LANES
    _fill_halo(xe_ref, x_ref, xp_ref, xn_ref, i, nblk)
    _fill_halo(le_ref, l_ref, lp_ref, ln_ref, i, nblk)

    def shifted(e_ref, mu):
        x = e_ref[HALO:HALO + rows, :]
        nb = 0.5 * (e_ref[HALO - 1:HALO - 1 + rows, :] + e_ref[HALO + 1:HALO + 1 + rows, :])
        return x + (nb - x) * mu

    x = shifted(xe_ref, mux_ref[...])
    lo = shifted(le_ref, mul_ref[...])
    r = x[:, :bw]
    k = x[:, bw:2 * bw]
    v = x[:, 2 * bw:]
    l01 = lo[:, :LANES]
    lane = _iota2((1, LANES), 1)
    xin = jnp.where(lane < 2 * lora_w, jnp.tanh(l01), l01)
    wa = _dot(xin, wcat_ref[...]) + bias_ref[...]
    gate_ref[0] = _dot(_sigmoid(lo[:, LANES:]), g2_ref[...])

    sr = _iota2((LANES, LANES), 0) // CHUNK
    sc = _iota2((LANES, LANES), 1) // CHUNK
    head_sel = jnp.where(sr == sc, 1.0, 0.0).astype(BF16)

    def head_sum(a):
        return jnp.concatenate(
            [_dot_sel_r(a[:, p * LANES:(p + 1) * LANES], head_sel) for p in range(pairs)], axis=1)

    kkp = k * kk_w_ref[...]
    kk = kkp * lax.rsqrt(head_sum(kkp * kkp) + L2_EPS)
    r_scr[...] = r
    v_scr[...] = v
    kk_scr[...] = kk
    ka = ka_ref[...]
    kb = jnp.zeros_like(k)
    for e in range(2):
        a = _sigmoid(wa[:, (2 + e) * bw:(3 + e) * bw])
        lw_scr[e] = -B_DECAY_SCALE * _sigmoid(wa[:, e * bw:(e + 1) * bw])
        kka_scr[e] = kk * a
        kd = k * (1.0 + (a - 1.0) * ka)
        kd_scr[e] = kd
        kb = kb + 0.5 * kd
    bonus_ref[0] = head_sum(r * kb * rk_ref[...]) * v

    lanes2 = _iota2((1, LANES), 1)
    m0 = lanes2 < CHUNK
    tr = _iota2((CHUNK, CHUNK), 0)
    tc = _iota2((CHUNK, CHUNK), 1)
    cum_sel = (jnp.where(tc <= tr, 1.0, 0.0).astype(BF16), jnp.where(tc >= tr, 1.0, 0.0).astype(BF16))
    masks = (_unit_masks(False, False), _unit_masks(True, True))

    def stack(a):
        return jnp.concatenate([jnp.where(m0, a, 0.0), jnp.where(m0, 0.0, a)], axis=0)

    def halves(a):
        return a[:CHUNK] + a[CHUNK:]

    def chunk_body(c, carry):
        r0 = pl.multiple_of(c * CHUNK, CHUNK)
        rows_c = pl.ds(r0, CHUNK)
        for e in range(2):
            incl, strict, blk, same, eye = masks[e]
            lw_c = lw_scr[e, rows_c, :]
            cs_c = _dot_sel(cum_sel[e], lw_c)
            last = CHUNK - 1 if e == 0 else 0
            for p in range(pairs):
                ls = slice(p * LANES, (p + 1) * LANES)
                lw = lw_c[:, ls]
                cs = cs_c[:, ls]
                cl = cs_c[last:last + 1, ls]
                rr = r_scr[rows_c, ls]
                vv = v_scr[rows_c, ls]
                kk_ = kk_scr[rows_c, ls]
                kka = kka_scr[e, rows_c, ls]
                kd = kd_scr[e, rows_c, ls]
                e_neg = jnp.exp(-cs)
                e_last = jnp.exp(cl - cs)
                xs = stack(-kk_ * jnp.exp(cs - lw))
                rt = rr * jnp.exp(cs)
                g = _dot_nt(jnp.concatenate([xs, stack(rt)], axis=0),
                            jnp.concatenate([stack(kka * e_neg), stack(kd * e_neg)], axis=0))
                aab = jnp.where(strict, g[:UNIT, :UNIT], 0.0)
                aak = jnp.where(strict, g[:UNIT, UNIT:], 0.0)
                arb = jnp.where(incl, g[UNIT:, :UNIT], 0.0)
                ark = jnp.where(incl, g[UNIT:, UNIT:], 0.0)
                t = _inv_i_minus(aab, blk, eye)
                av = _dot(jnp.concatenate([aak, ark], axis=0), stack(vv))
                tw = _dot(t, jnp.concatenate([xs, av[:UNIT]], axis=1))
                wq_ref[e, 0, p, c] = jnp.concatenate([halves(tw[:, :LANES]), rt], axis=0).astype(BF16)
                u0_ref[e, 0, p, c] = halves(tw[:, LANES:])
                y0_ref[e, 0, p, c] = halves(av[UNIT:])
                aq_ref[e, 0, p, c] = halves(arb).astype(BF16)
                kv0_ref[e, 0, p, c] = jnp.where(same, _dot((kd * e_last).T, vv), 0.0)
                bg_ref[e, 0, p, c] = jnp.concatenate(
                    [kka * e_last, jnp.broadcast_to(jnp.exp(cl), (CHUNK, LANES))], axis=0).T
        return carry

    lax.fori_loop(0, rows // CHUNK, chunk_body, 0)


def _rwkv_prep(rkv, lora, mu_x, mu_l, wcat, bias, g2p, k_k, k_a, r_k, lora_w, lora_a, rows=256):
    b, t, width = rkv.shape
    bw = width // 3
    pairs = bw // LANES
    lw_ = lora.shape[2]
    n = t // CHUNK
    nb = rows // CHUNK
    unit = lambda r, dt: jax.ShapeDtypeStruct((2, b, pairs, n, r, LANES), dt)
    ospec = lambda r: pl.BlockSpec((2, 1, pairs, nb, r, LANES), lambda bi, i: (0, bi, 0, i, 0, 0))
    row_spec = pl.BlockSpec((1, rows, bw), lambda bi, i: (bi, i, 0))
    return pl.pallas_call(
        functools.partial(_rwkv_prep_kernel, lora_w=lora_w, lora_a=lora_a),
        grid=(b, t // rows),
        in_specs=_halo_specs(rows, width, t) + _halo_specs(rows, lw_, t) + [
            _const_spec((1, width)), _const_spec((1, lw_)), _const_spec(wcat.shape), _const_spec(bias.shape),
            _const_spec(g2p.shape), _const_spec((1, bw)), _const_spec((1, bw)), _const_spec((1, bw))],
        out_specs=[ospec(UNIT), ospec(CHUNK), ospec(CHUNK), ospec(CHUNK), ospec(UNIT), ospec(UNIT), row_spec, row_spec],
        out_shape=[unit(UNIT, BF16), unit(CHUNK, F32), unit(CHUNK, F32), unit(CHUNK, BF16), unit(UNIT, F32),
                   unit(UNIT, F32), jax.ShapeDtypeStruct((b, t, bw), F32), jax.ShapeDtypeStruct((b, t, bw), F32)],
        scratch_shapes=[pltpu.VMEM((rows + 2 * HALO, width), F32), pltpu.VMEM((rows + 2 * HALO, lw_), F32)]
        + [pltpu.VMEM((rows, bw), F32)] * 3 + [pltpu.VMEM((2, rows, bw), F32)] * 3,
        compiler_params=_params(("parallel", "arbitrary"), rows * width * 4, rows * lw_ * 4, wcat.size * 2,
                                2 * pairs * nb * (UNIT * 2 + CHUNK * 10 + UNIT * 8) * LANES, 2 * rows * bw * 4,
                                scratch=(rows + 2 * HALO) * (width + lw_) * 4 + 9 * rows * bw * 4,
                                temps=8 * rows * bw * 4 * 4),
        name="rwkv_prep",
    )(rkv, rkv, rkv, lora, lora, lora, mu_x, mu_l, wcat, bias, g2p, k_k, k_a, r_k)


def _seq_kernel(*refs, paired, ub):
    if paired:
        wq_ref, u_ref, aq_ref, bg_ref, y0_ref, kv0_ref, y_ref, s_ref = refs
    else:
        wq_ref, u_ref, aq_ref, bg_ref, y_ref, s_ref = refs

    @pl.when(pl.program_id(1) == 0)
    def _():
        s_ref[...] = jnp.zeros_like(s_ref)

    m0 = _iota2((1, LANES), 1) < CHUNK
    same = (_iota2((LANES, LANES), 0) // CHUNK) == (_iota2((LANES, LANES), 1) // CHUNK)
    for j in range(ub):
        s = s_ref[j]
        pq = jnp.dot(wq_ref[j, 0], s.astype(BF16), preferred_element_type=F32)
        un = u_ref[j, 0] + pq[:CHUNK]
        if paired:
            uns = jnp.concatenate([jnp.where(m0, un, 0.0), jnp.where(m0, 0.0, un)], axis=0)
        else:
            uns = jnp.concatenate([un, un], axis=0)
        y = pq[CHUNK:] + jnp.dot(aq_ref[j, 0], uns.astype(BF16), preferred_element_type=F32)
        bg = bg_ref[j, 0]
        upd = _dot(bg[:, :CHUNK], un)
        gam = jnp.broadcast_to(bg[:, CHUNK:CHUNK + 1], (LANES, LANES))
        if paired:
            y = y + y0_ref[j, 0]
            upd = jnp.where(same, upd, 0.0) + kv0_ref[j, 0]
        s_ref[j] = s * gam + upd
        y_ref[j, 0] = y


def _seq(ops, extra, per_dir, ub=8):
    ub = math.gcd(ub, per_dir)
    wq, u, aq, bg = ops
    g, n = wq.shape[0], wq.shape[1]
    paired = extra is not None
    assert per_dir % ub == 0

    def imap(gi, ni):
        back = (gi * ub) // per_dir
        return (gi, ni + back * (n - 1 - 2 * ni), 0, 0)

    spec = lambda r: pl.BlockSpec((ub, 1, r, LANES), imap)
    ins = [wq, u, aq, bg] + (list(extra) if paired else [])
    specs = [spec(UNIT), spec(CHUNK), spec(CHUNK), spec(UNIT)] + ([spec(CHUNK), spec(UNIT)] if paired else [])
    return pl.pallas_call(
        functools.partial(_seq_kernel, paired=paired, ub=ub),
        grid=(g // ub, n),
        in_specs=specs,
        out_specs=spec(CHUNK),
        out_shape=jax.ShapeDtypeStruct((g, n, CHUNK, LANES), F32),
        scratch_shapes=[pltpu.VMEM((ub, LANES, LANES), F32)],
        compiler_params=_params(("parallel", "arbitrary"), ub * (UNIT * 2 + CHUNK * 4 + CHUNK * 2 + UNIT * 4) * LANES,
                                ub * (CHUNK + UNIT) * LANES * 4 if paired else 0, ub * CHUNK * LANES * 4,
                                scratch=ub * LANES * LANES * 4),
        name="seq_rwkv" if paired else "seq_delta",
    )(*ins)


def _even_out_kernel(of_ref, ob_ref, yf_ref, yb_ref, z_ref, bonus_ref, gate_ref, h_ref, an_ref, lnw_ref, lnb_ref,
                     wo_ref, o_ref):
    rows = h_ref.shape[1]
    a_heads = of_ref.shape[2]
    pairs = yf_ref.shape[2]
    z = z_ref[0]
    pieces = []
    for hh in range(a_heads):
        o = (of_ref[0, 0, hh] + ob_ref[0, 0, hh]).reshape(rows, LANES)
        pieces.append(_rms(o, an_ref[...]) * _silu(z[:, hh * LANES:(hh + 1) * LANES]))
    sr = _iota2((LANES, LANES), 0) // CHUNK
    sc = _iota2((LANES, LANES), 1) // CHUNK
    head_sel = jnp.where(sr == sc, 1.0, 0.0).astype(BF16)
    inv_n = 1.0 / CHUNK
    for p in range(pairs):
        ls = slice(p * LANES, (p + 1) * LANES)
        y = (yf_ref[0, 0, p] + yb_ref[0, 0, p]).reshape(rows, LANES)
        mu = _dot_sel_r(y, head_sel) * inv_n
        yc = y - mu
        var = _dot_sel_r(yc * yc, head_sel) * inv_n
        yn = yc * lax.rsqrt(var + B_GN_EPS) * lnw_ref[:, ls] + lnb_ref[:, ls]
        pieces.append((yn + bonus_ref[0][:, ls]) * gate_ref[0][:, ls])
    mix = jnp.concatenate(pieces, axis=1).astype(BF16)
    o_ref[0] = h_ref[0] + jnp.dot(mix, wo_ref[...], preferred_element_type=F32)


def _even_out(od, yr, z, bonus, gate, h3, a_norm, ln_w, ln_b, wo, rows=256):
    b, t, d = h3.shape
    nb = rows // CHUNK
    a_heads, pairs = od.shape[2], yr.shape[2]
    aw, bw = z.shape[2], bonus.shape[2]
    dspec = lambda dr, nh: pl.BlockSpec((1, 1, nh, nb, CHUNK, LANES), lambda bi, i: (dr, bi, 0, i, 0, 0))
    rspec = lambda w: pl.BlockSpec((1, rows, w), lambda bi, i: (bi, i, 0))
    return pl.pallas_call(
        _even_out_kernel,
        grid=(b, t // rows),
        in_specs=[dspec(0, a_heads), dspec(1, a_heads), dspec(0, pairs), dspec(1, pairs), rspec(aw), rspec(bw),
                  rspec(bw), rspec(d), _const_spec((1, LANES)), _const_spec((1, bw)), _const_spec((1, bw)),
                  _const_spec(wo.shape)],
        out_specs=rspec(d),
        out_shape=jax.ShapeDtypeStruct((b, t, d), F32),
        compiler_params=_params(("parallel", "parallel"), 2 * rows * (aw + bw) * 4, rows * (aw + 2 * bw + 2 * d) * 4,
                                wo.size * 2, temps=6 * rows * d * 4),
        name="even_out",
    )(od, od, yr, yr, z, bonus, gate, h3, a_norm, ln_w, ln_b, wo)


def _even_layer(h3, nw, w_in, w_out, a_conv, a_log, a_dt_bias, a_norm, b_mu, b_w0, b_w2, b_a0, b_a2, b_g2,
                b_k_k, b_k_a, b_r_k, b_ln_w, b_ln_b):
    b, t, d = h3.shape
    a_heads = a_log.shape[1]
    hd = a_norm.shape[0]
    aw = a_heads * hd
    b_heads, b_hd = b_r_k.shape
    bw = b_heads * b_hd
    lora_w, lora_a, lora_g = b_w2.shape[1], b_a2.shape[1], b_g2.shape[0]
    assert b_hd == CHUNK and hd == LANES and 2 * (lora_w + lora_a) == LANES and lora_g <= LANES
    a_cols = 4 * aw + 4 * a_heads
    n_gate = 4 * a_heads
    pad_cols = lambda w, n: jnp.pad(w, ((0, 0), (0, n - w.shape[1])))
    wa, wb = w_in[:, :a_cols], w_in[:, a_cols:]
    w_all = jnp.concatenate([wa[:, :4 * aw], pad_cols(wa[:, 4 * aw:], LANES),
                             wb[:, :3 * bw], pad_cols(wb[:, 3 * bw:], 2 * LANES)], axis=1).astype(BF16)
    splits = (3 * aw, aw, LANES, 3 * bw, 2 * LANES)
    qkv, z, ab, rkv, lora = _proj(h3.reshape(b * t, d), nw.reshape(1, d), w_all, splits)
    to3 = lambda a: a.reshape(b, t, a.shape[-1])

    alog_row = jnp.zeros((1, LANES), F32).at[0, :2 * a_heads].set(a_log.reshape(-1))
    dtb_row = jnp.zeros((1, LANES), F32).at[0, :2 * a_heads].set(a_dt_bias.reshape(-1))
    d_ops = _delta_prep(to3(qkv), to3(ab), a_conv, alog_row, dtb_row, a_heads)
    n = t // CHUNK
    flat = lambda a: a.reshape((-1, n) + a.shape[4:])
    od = _seq([flat(a) for a in d_ops], None, b * a_heads)
    od = od.reshape(2, b, a_heads, n, CHUNK, LANES)

    wcat = jnp.zeros((LANES, 4 * bw), F32)
    for e in range(2):
        wcat = wcat.at[e * lora_w:(e + 1) * lora_w, e * bw:(e + 1) * bw].set(b_w2[e])
        wcat = wcat.at[2 * lora_w + e * lora_a:2 * lora_w + (e + 1) * lora_a, (2 + e) * bw:(3 + e) * bw].set(b_a2[e])
    bias = jnp.concatenate([b_w0[0], b_w0[1], b_a0[0], b_a0[1]]).reshape(1, 4 * bw)
    g2p = jnp.zeros((LANES, bw), F32).at[:lora_g].set(b_g2).astype(BF16)
    mu_x = b_mu[:3 * bw].reshape(1, 3 * bw)
    mu_l = jnp.pad(b_mu[3 * bw:], (0, 2 * LANES - (b_mu.shape[0] - 3 * bw))).reshape(1, 2 * LANES)
    row = lambda a: a.reshape(1, bw)
    r_ops = _rwkv_prep(to3(rkv), to3(lora), mu_x, mu_l, wcat.astype(BF16), bias, g2p, row(b_k_k), row(b_k_a),
                       row(b_r_k), lora_w, lora_a)
    wq, u0, y0, aq, bg, kv0, bonus, gate = r_ops
    pairs = bw // LANES
    yr = _seq([flat(a) for a in (wq, u0, aq, bg)], (flat(y0), flat(kv0)), b * pairs)
    yr = yr.reshape(2, b, pairs, n, CHUNK, LANES)

    return _even_out(od, yr, to3(z), bonus, gate, h3, a_norm.reshape(1, hd), row(b_ln_w), row(b_ln_b),
                     w_out.astype(BF16))


def _odd_layer(h3, nw, w_in, ln_w, ln_b, ws, bs, w_out):
    b, t, d = h3.shape
    groups, chunk, _ = ws.shape
    cw = w_out.shape[0]
    bs_full = jnp.repeat(bs.T, cw // groups, axis=1)
    out = _gmlp(h3.reshape(b * t, d), nw.reshape(1, d), w_in.astype(BF16), ln_w.reshape(1, cw), ln_b.reshape(1, cw),
                ws.astype(BF16), bs_full, w_out.astype(BF16))
    return out.reshape(b, t, d)


def kernel(x, p, norm_mix, norm_ffn, norm_ple, norm_final, w_in_even, w_out_even, a_conv, a_log, a_dt_bias, a_norm,
           b_mu, b_w0, b_w2, b_a0, b_a2, b_g2, b_k_k, b_k_a, b_r_k, b_ln_w, b_ln_b, w_in_odd, c_ln_w, c_ln_b, c_ws,
           c_bs, w_out_odd, w_gate, w_up, w_down, w_ple, w_ple_gate):
    b, t, d = x.shape
    depth = p.shape[0]
    h = x
    for i in range(depth):
        j = i // 2
        if i % 2 == 0:
            h = _even_layer(h, norm_mix[i], w_in_even[j], w_out_even[j], a_conv[j], a_log[j], a_dt_bias[j], a_norm[j],
                            b_mu[j], b_w0[j], b_w2[j], b_a0[j], b_a2[j], b_g2[j], b_k_k[j], b_k_a[j], b_r_k[j],
                            b_ln_w[j], b_ln_b[j])
        else:
            h = _odd_layer(h, norm_mix[i], w_in_odd[j], c_ln_w[j], c_ln_b[j], c_ws[j], c_bs[j], w_out_odd[j])
        h2 = h.reshape(b * t, d)
        h2 = _ffn(h2, norm_ffn[i].reshape(1, d), w_gate[i].astype(BF16), w_up[i].astype(BF16),
                  w_down[i].astype(BF16))
        h2 = _ple(h2, p[i].reshape(b * t, -1), norm_ple[i].reshape(1, d), w_ple[i].astype(BF16),
                  w_ple_gate[i].astype(BF16), norm_final.reshape(1, d) if i == depth - 1 else None)
        h = h2.reshape(b, t, d)
    return h
```

```python
import functools
import math

import jax
import jax.numpy as jnp
from jax import lax
from jax.experimental import pallas as pl
from jax.experimental.pallas import tpu as pltpu

F32 = jnp.float32
BF16 = jnp.bfloat16

NORM_EPS = 1e-6
L2_EPS = 1e-6
B_GN_EPS = 64e-5
B_DECAY_SCALE = 0.606531
C_LN_EPS = 1e-5

LANES = 128
SUBLANES = 8
CHUNK = 64
UNIT = 2 * CHUNK
INV_BLOCK = 16
UNITS_IN_FLIGHT = 8
HALO = SUBLANES
VMEM_BUDGET = 56 * 1024 * 1024


def _params(sem, *block_bytes, scratch=0, temps=0):
    need = 2 * sum(block_bytes) + scratch + temps
    return pltpu.CompilerParams(dimension_semantics=sem, vmem_limit_bytes=int(min(max(need, 16 << 20), VMEM_BUDGET)))


def _dot(a, b):
    return jnp.dot(a.astype(BF16), b.astype(BF16), preferred_element_type=F32)


def _dot_nt(a, b):
    return lax.dot_general(a.astype(BF16), b.astype(BF16), (((1,), (1,)), ((), ())), preferred_element_type=F32)


def _split3(x):
    hi = x.astype(BF16)
    r1 = x - hi.astype(F32)
    mid = r1.astype(BF16)
    lo = (r1 - mid.astype(F32)).astype(BF16)
    return hi, mid, lo


def _dot_sel(sel, x):
    hi, mid, lo = _split3(x)
    d = lambda p: jnp.dot(sel, p, preferred_element_type=F32)
    return d(hi) + d(mid) + d(lo)


def _dot_sel_r(x, sel):
    hi, mid, lo = _split3(x)
    d = lambda p: jnp.dot(p, sel, preferred_element_type=F32)
    return d(hi) + d(mid) + d(lo)


def _sigmoid(x):
    return 1.0 / (1.0 + jnp.exp(-x))


def _silu(x):
    return x * _sigmoid(x)


def _softplus(x):
    return jnp.maximum(x, 0.0) + jnp.log(1.0 + jnp.exp(-jnp.abs(x)))


def _gelu_tanh(x):
    return 0.5 * x * (1.0 + jnp.tanh(math.sqrt(2.0 / math.pi) * (x + 0.044715 * (x * x * x))))


def _rms(x, w, eps=NORM_EPS):
    return x * lax.rsqrt(jnp.mean(x * x, axis=-1, keepdims=True) + eps) * w


def _iota2(shape, dim):
    return lax.broadcasted_iota(jnp.int32, shape, dim)


def _unit_masks(upper_top, upper_bot):
    r = _iota2((UNIT, UNIT), 0)
    c = _iota2((UNIT, UNIT), 1)
    same = (r // CHUNK) == (c // CHUNK)
    sign = jnp.where(r < CHUNK, -1 if upper_top else 1, -1 if upper_bot else 1)
    d = jnp.where(same, (r - c) * sign, -1)
    incl = d >= 0
    strict = d > 0
    blk = (r // INV_BLOCK) == (c // INV_BLOCK)
    eye = jnp.where(r == c, 1.0, 0.0).astype(F32)
    return incl, strict, blk, same, eye


def _stage_unit(u, a, blk, eye, p_scr, t_scr, z_scr):
    ad = jnp.where(blk, a, 0.0)
    p_scr[u] = ad.astype(BF16)
    t_scr[u] = eye + ad
    z_scr[u, :, :UNIT] = jnp.where(blk, 0.0, a).astype(BF16)


def _solve_units(nu, p_scr, t_scr, z_scr, x_scr, y_scr):
    assert CHUNK // INV_BLOCK == 4
    dot = lambda a, b: jnp.dot(a, b, preferred_element_type=F32)
    for _ in range(int(math.log2(INV_BLOCK)) - 1):
        for u in range(nu):
            p = p_scr[u]
            p_scr[u] = dot(p, p).astype(BF16)
        for u in range(nu):
            t = t_scr[u]
            t_scr[u] = t + dot(t.astype(BF16), p_scr[u])
    for u in range(nu):
        t = t_scr[u].astype(BF16)
        x_scr[0, u] = dot(t, z_scr[u, :, :UNIT]).astype(BF16)
        y_scr[u] = dot(t, z_scr[u, :, UNIT:])
    for u in range(nu):
        x = x_scr[0, u]
        x_scr[1, u] = dot(x, x).astype(BF16)
        y = y_scr[u]
        y_scr[u] = y + dot(x, y.astype(BF16))
    for u in range(nu):
        y = y_scr[u]
        y_scr[u] = y + dot(x_scr[1, u], y.astype(BF16))


def _solve_scratch(nu):
    return [pltpu.VMEM((nu, UNIT, UNIT), BF16), pltpu.VMEM((nu, UNIT, UNIT), F32),
            pltpu.VMEM((nu, UNIT, 3 * UNIT), BF16), pltpu.VMEM((2, nu, UNIT, UNIT), BF16),
            pltpu.VMEM((nu, UNIT, 2 * UNIT), F32)]


SOLVE_SCRATCH_BYTES_PER_UNIT = UNIT * UNIT * (2 + 4 + 6 + 4 + 8)


def _fill_halo(xe_ref, x_ref, prev_ref, next_ref, i, nblk):
    rows = x_ref.shape[1]
    pf = jnp.where(i > 0, 1.0, 0.0).astype(F32)
    nf = jnp.where(i < nblk - 1, 1.0, 0.0).astype(F32)
    xe_ref[0:HALO, :] = prev_ref[0] * pf
    xe_ref[HALO:HALO + rows, :] = x_ref[0]
    xe_ref[HALO + rows:2 * HALO + rows, :] = next_ref[0] * nf


def _halo_specs(rows, width, t_len):
    per = rows // HALO
    last = t_len // HALO - 1
    return [
        pl.BlockSpec((1, rows, width), lambda b, i: (b, i, 0)),
        pl.BlockSpec((1, HALO, width), lambda b, i: (b, jnp.maximum(i * per - 1, 0), 0)),
        pl.BlockSpec((1, HALO, width), lambda b, i: (b, jnp.minimum((i + 1) * per, last), 0)),
    ]


def _const_spec(shape):
    nd = len(shape)
    return pl.BlockSpec(shape, lambda *_: (0,) * nd)


def _proj_kernel(h_ref, nw_ref, w_ref, *o_refs, splits):
    hn = _rms(h_ref[...], nw_ref[...]).astype(BF16)
    off = 0
    for o_ref, n in zip(o_refs, splits):
        o_ref[...] = jnp.dot(hn, w_ref[:, off:off + n], preferred_element_type=F32)
        off += n


def _proj(h2, nw, w, splits, tm=512):
    m, d = h2.shape
    n = w.shape[1]
    return pl.pallas_call(
        functools.partial(_proj_kernel, splits=splits),
        grid=(m // tm,),
        in_specs=[pl.BlockSpec((tm, d), lambda i: (i, 0)), _const_spec((1, d)), _const_spec((d, n))],
        out_specs=[pl.BlockSpec((tm, s), lambda i: (i, 0)) for s in splits],
        out_shape=[jax.ShapeDtypeStruct((m, s), F32) for s in splits],
        compiler_params=_params(("parallel",), tm * d * 4, d * n * 2, tm * n * 4, temps=tm * d * 8),
        name="proj",
    )(h2, nw, w)


def _ffn_kernel(h_ref, nw_ref, wg_ref, wu_ref, wd_ref, o_ref, hn_ref, acc_ref):
    f = pl.program_id(1)

    @pl.when(f == 0)
    def _():
        hn_ref[...] = _rms(h_ref[...], nw_ref[...]).astype(BF16)
        acc_ref[...] = h_ref[...]

    hn = hn_ref[...]
    g = jnp.dot(hn, wg_ref[...], preferred_element_type=F32)
    u = jnp.dot(hn, wu_ref[...], preferred_element_type=F32)
    acc_ref[...] += jnp.dot((_silu(g) * u).astype(BF16), wd_ref[...], preferred_element_type=F32)

    @pl.when(f == pl.num_programs(1) - 1)
    def _():
        o_ref[...] = acc_ref[...]


def _ffn(h2, nw, wg, wu, wd, tm=512):
    m, d = h2.shape
    ff = wg.shape[1]
    tf = ff // 2 if (ff // 2) % LANES == 0 else ff
    return pl.pallas_call(
        _ffn_kernel,
        grid=(m // tm, ff // tf),
        in_specs=[
            pl.BlockSpec((tm, d), lambda i, f: (i, 0)),
            _const_spec((1, d)),
            pl.BlockSpec((d, tf), lambda i, f: (0, f)),
            pl.BlockSpec((d, tf), lambda i, f: (0, f)),
            pl.BlockSpec((tf, d), lambda i, f: (f, 0)),
        ],
        out_specs=pl.BlockSpec((tm, d), lambda i, f: (i, 0)),
        out_shape=jax.ShapeDtypeStruct((m, d), F32),
        scratch_shapes=[pltpu.VMEM((tm, d), BF16), pltpu.VMEM((tm, d), F32)],
        compiler_params=_params(("parallel", "arbitrary"), tm * d * 4, 3 * d * tf * 2, tm * d * 4,
                                scratch=tm * d * 6, temps=3 * tm * tf * 4),
        name="ffn",
    )(h2, nw, wg, wu, wd)


def _ple_kernel(h_ref, p_ref, nw_ref, wp_ref, wg_ref, *rest, final):
    if final:
        nf_ref, o_ref = rest
    else:
        (o_ref,) = rest
    h = h_ref[...]
    gate = _sigmoid(jnp.dot(_rms(h, nw_ref[...]).astype(BF16), wg_ref[...], preferred_element_type=F32))
    e = jnp.dot(p_ref[...].astype(BF16), wp_ref[...], preferred_element_type=F32)
    out = h + e * gate
    if final:
        out = _rms(out, nf_ref[...])
    o_ref[...] = out


def _ple(h2, p2, nw, wp, wg, nf=None, tm=512):
    m, d = h2.shape
    pd = p2.shape[1]
    final = nf is not None
    ins = [h2, p2, nw, wp, wg] + ([nf] if final else [])
    specs = [pl.BlockSpec((tm, d), lambda i: (i, 0)), pl.BlockSpec((tm, pd), lambda i: (i, 0)),
             _const_spec((1, d)), _const_spec((pd, d)), _const_spec((d, d))]
    if final:
        specs.append(_const_spec((1, d)))
    return pl.pallas_call(
        functools.partial(_ple_kernel, final=final),
        grid=(m // tm,),
        in_specs=specs,
        out_specs=pl.BlockSpec((tm, d), lambda i: (i, 0)),
        out_shape=jax.ShapeDtypeStruct((m, d), F32),
        compiler_params=_params(("parallel",), tm * d * 4, tm * pd * 4, pd * d * 2, d * d * 2, tm * d * 4,
                                temps=4 * tm * d * 4),
        name="ple",
    )(*ins)


def _gmlp_kernel(h_ref, nw_ref, wi_ref, lnw_ref, lnb_ref, ws_ref, bs_ref, wo_ref, o_ref, *, chunk, groups):
    h = h_ref[...]
    rows, d = h.shape
    cw = wo_ref.shape[0]
    gd = cw // groups
    hn = _rms(h, nw_ref[...]).astype(BF16)
    uv = _gelu_tanh(jnp.dot(hn, wi_ref[...], preferred_element_type=F32))
    u = uv[:, :cw]
    v = uv[:, cw:]
    mu = jnp.mean(v, axis=-1, keepdims=True)
    vc = v - mu
    var = jnp.mean(vc * vc, axis=-1, keepdims=True)
    vn = (vc * lax.rsqrt(var + C_LN_EPS) * lnw_ref[...] + lnb_ref[...]).astype(BF16)
    bs = bs_ref[...]
    pieces = []
    for c in range(rows // chunk):
        vrow = vn[c * chunk:(c + 1) * chunk]
        sv = jnp.concatenate(
            [jnp.dot(ws_ref[g], vrow[:, g * gd:(g + 1) * gd], preferred_element_type=F32) for g in range(groups)],
            axis=1)
        pieces.append((u[c * chunk:(c + 1) * chunk] * (sv + bs)).astype(BF16))
    gated = jnp.concatenate(pieces, axis=0)
    o_ref[...] = h + jnp.dot(gated, wo_ref[...], preferred_element_type=F32)


def _gmlp(h2, nw, wi, lnw, lnb, ws, bs_full, wo, tm=512):
    m, d = h2.shape
    groups, chunk, _ = ws.shape
    cw = wo.shape[0]
    return pl.pallas_call(
        functools.partial(_gmlp_kernel, chunk=chunk, groups=groups),
        grid=(m // tm,),
        in_specs=[pl.BlockSpec((tm, d), lambda i: (i, 0)), _const_spec((1, d)), _const_spec(wi.shape),
                  _const_spec((1, cw)), _const_spec((1, cw)), _const_spec(ws.shape), _const_spec(bs_full.shape),
                  _const_spec(wo.shape)],
        out_specs=pl.BlockSpec((tm, d), lambda i: (i, 0)),
        out_shape=jax.ShapeDtypeStruct((m, d), F32),
        compiler_params=_params(("parallel",), tm * d * 4, wi.size * 2, ws.size * 2, bs_full.size * 4, wo.size * 2,
                                tm * d * 4, temps=5 * tm * 2 * cw * 4),
        name="gmlp",
    )(h2, nw, wi, lnw, lnb, ws, bs_full, wo)


def _delta_prep_kernel(x_ref, xp_ref, xn_ref, ab_ref, cw_ref, alog_ref, dtb_ref,
                       wq_ref, u_ref, aq_ref, bg_ref,
                       xe_ref, q_scr, k_scr, v_scr, g_scr, b_scr, p_scr, t_scr, z_scr, x_scr, y_scr,
                       *, heads, taps, group):
    i = pl.program_id(1)
    rows = x_ref.shape[1]
    width = x_ref.shape[2]
    aw = width // 3
    hd = aw // heads
    _fill_halo(xe_ref, x_ref, xp_ref, xn_ref, i, pl.num_programs(1))

    acc = jnp.zeros((rows, width), F32)
    for j in range(taps):
        s = HALO - taps // 2 + j
        acc = acc + cw_ref[j:j + 1, :] * xe_ref[s:s + rows, :]
    y = _silu(acc)
    for hh in range(heads):
        qh = y[:, hh * hd:(hh + 1) * hd]
        kh = y[:, aw + hh * hd:aw + (hh + 1) * hd]
        q_scr[hh] = qh * lax.rsqrt(jnp.sum(qh * qh, axis=-1, keepdims=True) + L2_EPS) * (hd ** -0.5)
        k_scr[hh] = kh * lax.rsqrt(jnp.sum(kh * kh, axis=-1, keepdims=True) + L2_EPS)
        v_scr[hh] = y[:, 2 * aw + hh * hd:2 * aw + (hh + 1) * hd]
    ab = ab_ref[0]
    g_scr[...] = -jnp.exp(alog_ref[...]) * _softplus(ab + dtb_ref[...])
    b_scr[...] = _sigmoid(ab)

    incl, strict, blk, _, eye = _unit_masks(False, True)
    r1 = _iota2((UNIT, 1), 0)
    top_col = r1 < CHUNK
    top_row = _iota2((1, UNIT), 1) < CHUNK
    tri_r = _iota2((UNIT, CHUNK), 0)
    tri_c = _iota2((UNIT, CHUNK), 1)
    cum_sel = jnp.where(jnp.where(tri_r < CHUNK, tri_r - tri_c, tri_c - tri_r + CHUNK) >= 0, 1.0, 0.0).astype(BF16)

    def stage_chunk(c, u0):
        r0 = pl.multiple_of(c * CHUNK, CHUNK)
        gst = _dot_sel(cum_sel, g_scr[pl.ds(r0, CHUNK), :])
        gst_t = gst.T
        bch = b_scr[pl.ds(r0, CHUNK), :]
        bst = jnp.concatenate([bch, bch], axis=0)
        for hh in range(heads):
            fw, bw = hh, heads + hh
            gcs = jnp.where(top_col, gst[:, fw:fw + 1], gst[:, bw:bw + 1])
            grow = jnp.where(top_row, gst_t[fw:fw + 1, :], gst_t[bw:bw + 1, :])
            bcol = jnp.where(top_col, bst[:, 2 * heads + fw:2 * heads + fw + 1],
                             bst[:, 2 * heads + bw:2 * heads + bw + 1])
            glast = jnp.where(top_col, gst[CHUNK - 1:CHUNK, fw:fw + 1], gst[CHUNK:CHUNK + 1, bw:bw + 1])
            q = q_scr[hh, pl.ds(r0, CHUNK), :]
            k = k_scr[hh, pl.ds(r0, CHUNK), :]
            v = v_scr[hh, pl.ds(r0, CHUNK), :]
            qst = jnp.concatenate([q, q], axis=0)
            kst = jnp.concatenate([k, k], axis=0)
            vst = jnp.concatenate([v, v], axis=0)
            kst_b = kst.astype(BF16)
            decay = jnp.where(incl, jnp.exp(jnp.where(incl, gcs - grow, 0.0)), 0.0)
            kk = _dot_nt(kst_b, kst_b)
            m = jnp.where(strict, bcol * kk * decay, 0.0)
            _stage_unit(u0 + hh, -m, blk, eye, p_scr, t_scr, z_scr)
            eg = jnp.exp(gcs)
            z_scr[u0 + hh, :, UNIT:2 * UNIT] = (vst * bcol).astype(BF16)
            z_scr[u0 + hh, :, 2 * UNIT:] = (kst * (bcol * eg)).astype(BF16)
            qk = _dot_nt(qst, kst_b) * decay
            qd = qst * eg
            kdec = kst * jnp.exp(glast - gcs)
            gam = jnp.exp(glast)
            for dr in range(2):
                sl = slice(dr * CHUNK, (dr + 1) * CHUNK)
                wq_ref.at[dr, 0, hh, c][CHUNK:, :] = qd[sl].astype(BF16)
                aq_ref[dr, 0, hh, c] = qk[sl].astype(BF16)
                bg_ref[dr, 0, hh, c] = jnp.concatenate(
                    [kdec[sl], jnp.broadcast_to(gam[sl], (CHUNK, hd))], axis=0).T

    def finish_chunk(c, u0):
        for hh in range(heads):
            y = y_scr[u0 + hh]
            for dr in range(2):
                sl = slice(dr * CHUNK, (dr + 1) * CHUNK)
                wq_ref.at[dr, 0, hh, c][:CHUNK, :] = (-y[sl, hd:]).astype(BF16)
                u_ref[dr, 0, hh, c] = y[sl, :hd]

    def group_body(gi, carry):
        for cc in range(group):
            stage_chunk(gi * group + cc, cc * heads)
        _solve_units(group * heads, p_scr, t_scr, z_scr, x_scr, y_scr)
        for cc in range(group):
            finish_chunk(gi * group + cc, cc * heads)
        return carry

    lax.fori_loop(0, rows // (CHUNK * group), group_body, 0)


def _delta_prep(qkv, ab, conv_w, alog_row, dtb_row, heads, rows=256):
    b, t, width = qkv.shape
    hd = width // 3 // heads
    assert hd == LANES
    n = t // CHUNK
    nb = rows // CHUNK
    taps = conv_w.shape[0]
    cw = jnp.zeros((SUBLANES, width), F32).at[:taps].set(conv_w)
    group = UNITS_IN_FLIGHT // heads
    assert nb % group == 0
    unit = lambda r, dt: jax.ShapeDtypeStruct((2, b, heads, n, r, LANES), dt)
    ospec = lambda r: pl.BlockSpec((2, 1, heads, nb, r, LANES), lambda bi, i: (0, bi, 0, i, 0, 0))
    return pl.pallas_call(
        functools.partial(_delta_prep_kernel, heads=heads, taps=taps, group=group),
        grid=(b, t // rows),
        in_specs=_halo_specs(rows, width, t) + [
            pl.BlockSpec((1, rows, LANES), lambda bi, i: (bi, i, 0)),
            _const_spec((SUBLANES, width)), _const_spec((1, LANES)), _const_spec((1, LANES))],
        out_specs=[ospec(UNIT), ospec(CHUNK), ospec(CHUNK), ospec(UNIT)],
        out_shape=[unit(UNIT, BF16), unit(CHUNK, F32), unit(CHUNK, BF16), unit(UNIT, F32)],
        scratch_shapes=[pltpu.VMEM((rows + 2 * HALO, width), F32)]
        + [pltpu.VMEM((heads, rows, LANES), F32)] * 3 + [pltpu.VMEM((rows, LANES), F32)] * 2
        + _solve_scratch(group * heads),
        compiler_params=_params(("parallel", "arbitrary"), rows * width * 4, rows * LANES * 4,
                                2 * heads * nb * (UNIT * 2 + CHUNK * 4 + CHUNK * 2 + UNIT * 4) * LANES,
                                scratch=(2 * rows + 2 * HALO) * width * 4
                                + group * heads * SOLVE_SCRATCH_BYTES_PER_UNIT, temps=4 * rows * width * 4),
        name="delta_prep",
    )(qkv, qkv, qkv, ab, cw, alog_row, dtb_row)


def _rwkv_prep_kernel(x_ref, xp_ref, xn_ref, l_ref, lp_ref, ln_ref, mux_ref, mul_ref, wcat_ref, bias_ref, g2_ref,
                      kk_w_ref, ka_ref, rk_ref,
                      wq_ref, u0_ref, y0_ref, aq_ref, bg_ref, kv0_ref, bonus_ref, gate_ref,
                      xe_ref, le_ref, r_scr, v_scr, kk_scr, lw_scr, kka_scr, kd_scr, kr_scr, vs_scr,
                      p_scr, t_scr, z_scr, x_scr, y_scr, *, lora_w, lora_a):
    i = pl.program_id(1)
    nblk = pl.num_programs(1)
    rows = x_ref.shape[1]
    bw = x_ref.shape[2] // 3
    pairs = bw // LANES
    _fill_halo(xe_ref, x_ref, xp_ref, xn_ref, i, nblk)
    _fill_halo(le_ref, l_ref, lp_ref, ln_ref, i, nblk)

    def shifted(e_ref, mu):
        x = e_ref[HALO:HALO + rows, :]
        nb = 0.5 * (e_ref[HALO - 1:HALO - 1 + rows, :] + e_ref[HALO + 1:HALO + 1 + rows, :])
        return x + (nb - x) * mu

    x = shifted(xe_ref, mux_ref[...])
    lo = shifted(le_ref, mul_ref[...])
    r = x[:, :bw]
    k = x[:, bw:2 * bw]
    v = x[:, 2 * bw:]
    l01 = lo[:, :LANES]
    lane = _iota2((1, LANES), 1)
    xin = jnp.where(lane < 2 * lora_w, jnp.tanh(l01), l01)
    wa = _dot(xin, wcat_ref[...]) + bias_ref[...]
    gate_ref[0] = _dot(_sigmoid(lo[:, LANES:]), g2_ref[...])

    sr = _iota2((LANES, LANES), 0) // CHUNK
    sc = _iota2((LANES, LANES), 1) // CHUNK
    head_sel = jnp.where(sr == sc, 1.0, 0.0).astype(BF16)

    def head_sum(a):
        return jnp.concatenate(
            [_dot_sel_r(a[:, p * LANES:(p + 1) * LANES], head_sel) for p in range(pairs)], axis=1)

    kkp = k * kk_w_ref[...]
    kk = kkp * lax.rsqrt(head_sum(kkp * kkp) + L2_EPS)
    r_scr[...] = r
    v_scr[...] = v
    kk_scr[...] = kk
    ka = ka_ref[...]
    kb = jnp.zeros_like(k)
    for e in range(2):
        a = _sigmoid(wa[:, (2 + e) * bw:(3 + e) * bw])
        lw_scr[e] = -B_DECAY_SCALE * _sigmoid(wa[:, e * bw:(e + 1) * bw])
        kka_scr[e] = kk * a
        kd = k * (1.0 + (a - 1.0) * ka)
        kd_scr[e] = kd
        kb = kb + 0.5 * kd
    bonus_ref[0] = head_sum(r * kb * rk_ref[...]) * v

    lanes2 = _iota2((1, LANES), 1)
    m0 = lanes2 < CHUNK
    tr = _iota2((CHUNK, CHUNK), 0)
    tc = _iota2((CHUNK, CHUNK), 1)
    cum_sel = (jnp.where(tc <= tr, 1.0, 0.0).astype(BF16), jnp.where(tc >= tr, 1.0, 0.0).astype(BF16))
    masks = (_unit_masks(False, False), _unit_masks(True, True))

    def stack(a):
        return jnp.concatenate([jnp.where(m0, a, 0.0), jnp.where(m0, 0.0, a)], axis=0)

    def halves(a):
        return a[:CHUNK] + a[CHUNK:]

    def chunk_body(c, carry):
        r0 = pl.multiple_of(c * CHUNK, CHUNK)
        rows_c = pl.ds(r0, CHUNK)
        for e in range(2):
            incl, strict, blk, same, eye = masks[e]
            lw_c = lw_scr[e, rows_c, :]
            cs_c = _dot_sel(cum_sel[e], lw_c)
            last = CHUNK - 1 if e == 0 else 0
            for p in range(pairs):
                ls = slice(p * LANES, (p + 1) * LANES)
                lw = lw_c[:, ls]
                cs = cs_c[:, ls]
                cl = cs_c[last:last + 1, ls]
                rr = r_scr[rows_c, ls]
                vv = v_scr[rows_c, ls]
                kk_ = kk_scr[rows_c, ls]
                kka = kka_scr[e, rows_c, ls]
                kd = kd_scr[e, rows_c, ls]
                u = e * pairs + p
                e_neg = jnp.exp(-cs)
                e_last = jnp.exp(cl - cs)
                xs = stack(-kk_ * jnp.exp(cs - lw)).astype(BF16)
                rt = rr * jnp.exp(cs)
                bk = jnp.concatenate([stack(kka * e_neg), stack(kd * e_neg)], axis=0).astype(BF16)
                ga = _dot_nt(xs, bk)
                gr = _dot_nt(stack(rt), bk)
                _stage_unit(u, jnp.where(strict, ga[:, :UNIT], 0.0), blk, eye, p_scr, t_scr, z_scr)
                z_scr[u, :, UNIT:2 * UNIT] = xs
                kr_scr[u, :UNIT, :] = jnp.where(strict, ga[:, UNIT:], 0.0).astype(BF16)
                kr_scr[u, UNIT:, :] = jnp.where(incl, gr[:, UNIT:], 0.0).astype(BF16)
                vs_scr[u] = stack(vv).astype(BF16)
                aq_ref[e, 0, p, c] = halves(jnp.where(incl, gr[:, :UNIT], 0.0)).astype(BF16)
                wq_ref.at[e, 0, p, c][CHUNK:, :] = rt.astype(BF16)
                kv0_ref[e, 0, p, c] = jnp.where(same, _dot((kd * e_last).T, vv), 0.0)
                bg_ref[e, 0, p, c] = jnp.concatenate(
                    [kka * e_last, jnp.broadcast_to(jnp.exp(cl), (CHUNK, LANES))], axis=0).T
        for e in range(2):
            for p in range(pairs):
                u = e * pairs + p
                av = jnp.dot(kr_scr[u], vs_scr[u], preferred_element_type=F32)
                y0_ref[e, 0, p, c] = halves(av[UNIT:])
                z_scr[u, :, 2 * UNIT:] = av[:UNIT].astype(BF16)
        _solve_units(2 * pairs, p_scr, t_scr, z_scr, x_scr, y_scr)
        for e in range(2):
            for p in range(pairs):
                y = y_scr[e * pairs + p]
                wq_ref.at[e, 0, p, c][:CHUNK, :] = halves(y[:, :LANES]).astype(BF16)
                u0_ref[e, 0, p, c] = halves(y[:, LANES:])
        return carry

    lax.fori_loop(0, rows // CHUNK, chunk_body, 0)


def _rwkv_prep(rkv, lora, mu_x, mu_l, wcat, bias, g2p, k_k, k_a, r_k, lora_w, lora_a, rows=256):
    b, t, width = rkv.shape
    bw = width // 3
    pairs = bw // LANES
    lw_ = lora.shape[2]
    n = t // CHUNK
    nb = rows // CHUNK
    unit = lambda r, dt: jax.ShapeDtypeStruct((2, b, pairs, n, r, LANES), dt)
    ospec = lambda r: pl.BlockSpec((2, 1, pairs, nb, r, LANES), lambda bi, i: (0, bi, 0, i, 0, 0))
    row_spec = pl.BlockSpec((1, rows, bw), lambda bi, i: (bi, i, 0))
    return pl.pallas_call(
        functools.partial(_rwkv_prep_kernel, lora_w=lora_w, lora_a=lora_a),
        grid=(b, t // rows),
        in_specs=_halo_specs(rows, width, t) + _halo_specs(rows, lw_, t) + [
            _const_spec((1, width)), _const_spec((1, lw_)), _const_spec(wcat.shape), _const_spec(bias.shape),
            _const_spec(g2p.shape), _const_spec((1, bw)), _const_spec((1, bw)), _const_spec((1, bw))],
        out_specs=[ospec(UNIT), ospec(CHUNK), ospec(CHUNK), ospec(CHUNK), ospec(UNIT), ospec(UNIT), row_spec, row_spec],
        out_shape=[unit(UNIT, BF16), unit(CHUNK, F32), unit(CHUNK, F32), unit(CHUNK, BF16), unit(UNIT, F32),
                   unit(UNIT, F32), jax.ShapeDtypeStruct((b, t, bw), F32), jax.ShapeDtypeStruct((b, t, bw), F32)],
        scratch_shapes=[pltpu.VMEM((rows + 2 * HALO, width), F32), pltpu.VMEM((rows + 2 * HALO, lw_), F32)]
        + [pltpu.VMEM((rows, bw), F32)] * 3 + [pltpu.VMEM((2, rows, bw), F32)] * 3
        + [pltpu.VMEM((2 * pairs, 2 * UNIT, UNIT), BF16), pltpu.VMEM((2 * pairs, UNIT, LANES), BF16)]
        + _solve_scratch(2 * pairs),
        compiler_params=_params(("parallel", "arbitrary"), rows * width * 4, rows * lw_ * 4, wcat.size * 2,
                                2 * pairs * nb * (UNIT * 2 + CHUNK * 10 + UNIT * 8) * LANES, 2 * rows * bw * 4,
                                scratch=(rows + 2 * HALO) * (width + lw_) * 4 + 9 * rows * bw * 4
                                + 2 * pairs * (SOLVE_SCRATCH_BYTES_PER_UNIT + 3 * UNIT * LANES * 2),
                                temps=8 * rows * bw * 4 * 4),
        name="rwkv_prep",
    )(rkv, rkv, rkv, lora, lora, lora, mu_x, mu_l, wcat, bias, g2p, k_k, k_a, r_k)


def _seq_kernel(*refs, paired, ub):
    if paired:
        wq_ref, u_ref, aq_ref, bg_ref, y0_ref, kv0_ref, y_ref, s_ref, pq_ref = refs
    else:
        wq_ref, u_ref, aq_ref, bg_ref, y_ref, s_ref, pq_ref = refs

    @pl.when(pl.program_id(1) == 0)
    def _():
        s_ref[...] = jnp.zeros_like(s_ref)

    m0 = _iota2((1, LANES), 1) < CHUNK
    same = (_iota2((LANES, LANES), 0) // CHUNK) == (_iota2((LANES, LANES), 1) // CHUNK)
    for j in range(ub):
        pq_ref[j] = jnp.dot(wq_ref[j, 0], s_ref[j].astype(BF16), preferred_element_type=F32)
    for j in range(ub):
        s = s_ref[j]
        pq = pq_ref[j]
        un = u_ref[j, 0] + pq[:CHUNK]
        if paired:
            uns = jnp.concatenate([jnp.where(m0, un, 0.0), jnp.where(m0, 0.0, un)], axis=0)
        else:
            uns = jnp.concatenate([un, un], axis=0)
        y = pq[CHUNK:] + jnp.dot(aq_ref[j, 0], uns.astype(BF16), preferred_element_type=F32)
        bg = bg_ref[j, 0]
        upd = _dot(bg[:, :CHUNK], un)
        gam = jnp.broadcast_to(bg[:, CHUNK:CHUNK + 1], (LANES, LANES))
        if paired:
            y = y + y0_ref[j, 0]
            upd = jnp.where(same, upd, 0.0) + kv0_ref[j, 0]
        s_ref[j] = s * gam + upd
        y_ref[j, 0] = y


def _seq(ops, extra, per_dir, ub=8):
    ub = math.gcd(ub, per_dir)
    wq, u, aq, bg = ops
    g, n = wq.shape[0], wq.shape[1]
    paired = extra is not None
    assert per_dir % ub == 0

    def imap(gi, ni):
        back = (gi * ub) // per_dir
        return (gi, ni + back * (n - 1 - 2 * ni), 0, 0)

    spec = lambda r: pl.BlockSpec((ub, 1, r, LANES), imap)
    ins = [wq, u, aq, bg] + (list(extra) if paired else [])
    specs = [spec(UNIT), spec(CHUNK), spec(CHUNK), spec(UNIT)] + ([spec(CHUNK), spec(UNIT)] if paired else [])
    return pl.pallas_call(
        functools.partial(_seq_kernel, paired=paired, ub=ub),
        grid=(g // ub, n),
        in_specs=specs,
        out_specs=spec(CHUNK),
        out_shape=jax.ShapeDtypeStruct((g, n, CHUNK, LANES), F32),
        scratch_shapes=[pltpu.VMEM((ub, LANES, LANES), F32), pltpu.VMEM((ub, UNIT, LANES), F32)],
        compiler_params=_params(("parallel", "arbitrary"), ub * (UNIT * 2 + CHUNK * 4 + CHUNK * 2 + UNIT * 4) * LANES,
                                ub * (CHUNK + UNIT) * LANES * 4 if paired else 0, ub * CHUNK * LANES * 4,
                                scratch=ub * (LANES + UNIT) * LANES * 4),
        name="seq_rwkv" if paired else "seq_delta",
    )(*ins)


def _even_out_kernel(of_ref, ob_ref, yf_ref, yb_ref, z_ref, bonus_ref, gate_ref, h_ref, an_ref, lnw_ref, lnb_ref,
                     wo_ref, o_ref):
    rows = h_ref.shape[1]
    a_heads = of_ref.shape[2]
    pairs = yf_ref.shape[2]
    z = z_ref[0]
    pieces = []
    for hh in range(a_heads):
        o = (of_ref[0, 0, hh] + ob_ref[0, 0, hh]).reshape(rows, LANES)
        pieces.append(_rms(o, an_ref[...]) * _silu(z[:, hh * LANES:(hh + 1) * LANES]))
    sr = _iota2((LANES, LANES), 0) // CHUNK
    sc = _iota2((LANES, LANES), 1) // CHUNK
    head_sel = jnp.where(sr == sc, 1.0, 0.0).astype(BF16)
    inv_n = 1.0 / CHUNK
    for p in range(pairs):
        ls = slice(p * LANES, (p + 1) * LANES)
        y = (yf_ref[0, 0, p] + yb_ref[0, 0, p]).reshape(rows, LANES)
        mu = _dot_sel_r(y, head_sel) * inv_n
        yc = y - mu
        var = _dot_sel_r(yc * yc, head_sel) * inv_n
        yn = yc * lax.rsqrt(var + B_GN_EPS) * lnw_ref[:, ls] + lnb_ref[:, ls]
        pieces.append((yn + bonus_ref[0][:, ls]) * gate_ref[0][:, ls])
    mix = jnp.concatenate(pieces, axis=1).astype(BF16)
    o_ref[0] = h_ref[0] + jnp.dot(mix, wo_ref[...], preferred_element_type=F32)


def _even_out(od, yr, z, bonus, gate, h3, a_norm, ln_w, ln_b, wo, rows=256):
    b, t, d = h3.shape
    nb = rows // CHUNK
    a_heads, pairs = od.shape[2], yr.shape[2]
    aw, bw = z.shape[2], bonus.shape[2]
    dspec = lambda dr, nh: pl.BlockSpec((1, 1, nh, nb, CHUNK, LANES), lambda bi, i: (dr, bi, 0, i, 0, 0))
    rspec = lambda w: pl.BlockSpec((1, rows, w), lambda bi, i: (bi, i, 0))
    return pl.pallas_call(
        _even_out_kernel,
        grid=(b, t // rows),
        in_specs=[dspec(0, a_heads), dspec(1, a_heads), dspec(0, pairs), dspec(1, pairs), rspec(aw), rspec(bw),
                  rspec(bw), rspec(d), _const_spec((1, LANES)), _const_spec((1, bw)), _const_spec((1, bw)),
                  _const_spec(wo.shape)],
        out_specs=rspec(d),
        out_shape=jax.ShapeDtypeStruct((b, t, d), F32),
        compiler_params=_params(("parallel", "parallel"), 2 * rows * (aw + bw) * 4, rows * (aw + 2 * bw + 2 * d) * 4,
                                wo.size * 2, temps=6 * rows * d * 4),
        name="even_out",
    )(od, od, yr, yr, z, bonus, gate, h3, a_norm, ln_w, ln_b, wo)


def _even_layer(h3, nw, w_in, w_out, a_conv, a_log, a_dt_bias, a_norm, b_mu, b_w0, b_w2, b_a0, b_a2, b_g2,
                b_k_k, b_k_a, b_r_k, b_ln_w, b_ln_b):
    b, t, d = h3.shape
    a_heads = a_log.shape[1]
    hd = a_norm.shape[0]
    aw = a_heads * hd
    b_heads, b_hd = b_r_k.shape
    bw = b_heads * b_hd
    lora_w, lora_a, lora_g = b_w2.shape[1], b_a2.shape[1], b_g2.shape[0]
    assert b_hd == CHUNK and hd == LANES and 2 * (lora_w + lora_a) == LANES and lora_g <= LANES
    a_cols = 4 * aw + 4 * a_heads
    n_gate = 4 * a_heads
    pad_cols = lambda w, n: jnp.pad(w, ((0, 0), (0, n - w.shape[1])))
    wa, wb = w_in[:, :a_cols], w_in[:, a_cols:]
    w_all = jnp.concatenate([wa[:, :4 * aw], pad_cols(wa[:, 4 * aw:], LANES),
                             wb[:, :3 * bw], pad_cols(wb[:, 3 * bw:], 2 * LANES)], axis=1).astype(BF16)
    splits = (3 * aw, aw, LANES, 3 * bw, 2 * LANES)
    qkv, z, ab, rkv, lora = _proj(h3.reshape(b * t, d), nw.reshape(1, d), w_all, splits)
    to3 = lambda a: a.reshape(b, t, a.shape[-1])

    alog_row = jnp.zeros((1, LANES), F32).at[0, :2 * a_heads].set(a_log.reshape(-1))
    dtb_row = jnp.zeros((1, LANES), F32).at[0, :2 * a_heads].set(a_dt_bias.reshape(-1))
    d_ops = _delta_prep(to3(qkv), to3(ab), a_conv, alog_row, dtb_row, a_heads)
    n = t // CHUNK
    flat = lambda a: a.reshape((-1, n) + a.shape[4:])
    od = _seq([flat(a) for a in d_ops], None, b * a_heads)
    od = od.reshape(2, b, a_heads, n, CHUNK, LANES)

    wcat = jnp.zeros((LANES, 4 * bw), F32)
    for e in range(2):
        wcat = wcat.at[e * lora_w:(e + 1) * lora_w, e * bw:(e + 1) * bw].set(b_w2[e])
        wcat = wcat.at[2 * lora_w + e * lora_a:2 * lora_w + (e + 1) * lora_a, (2 + e) * bw:(3 + e) * bw].set(b_a2[e])
    bias = jnp.concatenate([b_w0[0], b_w0[1], b_a0[0], b_a0[1]]).reshape(1, 4 * bw)
    g2p = jnp.zeros((LANES, bw), F32).at[:lora_g].set(b_g2).astype(BF16)
    mu_x = b_mu[:3 * bw].reshape(1, 3 * bw)
    mu_l = jnp.pad(b_mu[3 * bw:], (0, 2 * LANES - (b_mu.shape[0] - 3 * bw))).reshape(1, 2 * LANES)
    row = lambda a: a.reshape(1, bw)
    r_ops = _rwkv_prep(to3(rkv), to3(lora), mu_x, mu_l, wcat.astype(BF16), bias, g2p, row(b_k_k), row(b_k_a),
                       row(b_r_k), lora_w, lora_a)
    wq, u0, y0, aq, bg, kv0, bonus, gate = r_ops
    pairs = bw // LANES
    yr = _seq([flat(a) for a in (wq, u0, aq, bg)], (flat(y0), flat(kv0)), b * pairs)
    yr = yr.reshape(2, b, pairs, n, CHUNK, LANES)

    return _even_out(od, yr, to3(z), bonus, gate, h3, a_norm.reshape(1, hd), row(b_ln_w), row(b_ln_b),
                     w_out.astype(BF16))


def _odd_layer(h3, nw, w_in, ln_w, ln_b, ws, bs, w_out):
    b, t, d = h3.shape
    groups, chunk, _ = ws.shape
    cw = w_out.shape[0]
    bs_full = jnp.repeat(bs.T, cw // groups, axis=1)
    out = _gmlp(h3.reshape(b * t, d), nw.reshape(1, d), w_in.astype(BF16), ln_w.reshape(1, cw), ln_b.reshape(1, cw),
                ws.astype(BF16), bs_full, w_out.astype(BF16))
    return out.reshape(b, t, d)


def kernel(x, p, norm_mix, norm_ffn, norm_ple, norm_final, w_in_even, w_out_even, a_conv, a_log, a_dt_bias, a_norm,
           b_mu, b_w0, b_w2, b_a0, b_a2, b_g2, b_k_k, b_k_a, b_r_k, b_ln_w, b_ln_b, w_in_odd, c_ln_w, c_ln_b, c_ws,
           c_bs, w_out_odd, w_gate, w_up, w_down, w_ple, w_ple_gate):
    b, t, d = x.shape
    depth = p.shape[0]
    h = x
    for i in range(depth):
        j = i // 2
        if i % 2 == 0:
            h = _even_layer(h, norm_mix[i], w_in_even[j], w_out_even[j], a_conv[j], a_log[j], a_dt_bias[j], a_norm[j],
                            b_mu[j], b_w0[j], b_w2[j], b_a0[j], b_a2[j], b_g2[j], b_k_k[j], b_k_a[j], b_r_k[j],
                            b_ln_w[j], b_ln_b[j])
        else:
            h = _odd_layer(h, norm_mix[i], w_in_odd[j], c_ln_w[j], c_ln_b[j], c_ws[j], c_bs[j], w_out_odd[j])
        h2 = h.reshape(b * t, d)
        h2 = _ffn(h2, norm_ffn[i].reshape(1, d), w_gate[i].astype(BF16), w_up[i].astype(BF16),
                  w_down[i].astype(BF16))
        h2 = _ple(h2, p[i].reshape(b * t, -1), norm_ple[i].reshape(1, d), w_ple[i].astype(BF16),
                  w_ple_gate[i].astype(BF16), norm_final.reshape(1, d) if i == depth - 1 else None)
        h = h2.reshape(b, t, d)
    return h
```

```python
import functools
import math

import jax
import jax.numpy as jnp
from jax import lax
from jax.experimental import pallas as pl
from jax.experimental.pallas import tpu as pltpu

F32 = jnp.float32
BF16 = jnp.bfloat16

NORM_EPS = 1e-6
L2_EPS = 1e-6
B_GN_EPS = 64e-5
B_DECAY_SCALE = 0.606531
C_LN_EPS = 1e-5

LANES = 128
SUBLANES = 8
CHUNK = 64
UNIT = 2 * CHUNK
INV_BLOCK = 16
GAM_LANES = 8
UNITS_IN_FLIGHT = 8
HALO = SUBLANES
VMEM_BUDGET = 56 * 1024 * 1024


def _params(sem, *block_bytes, scratch=0, temps=0):
    need = 2 * sum(block_bytes) + scratch + temps
    return pltpu.CompilerParams(dimension_semantics=sem, vmem_limit_bytes=int(min(max(need, 16 << 20), VMEM_BUDGET)))


def _dot(a, b):
    return jnp.dot(a.astype(BF16), b.astype(BF16), preferred_element_type=F32)


def _dot_nt(a, b):
    return lax.dot_general(a.astype(BF16), b.astype(BF16), (((1,), (1,)), ((), ())), preferred_element_type=F32)


def _split3(x):
    hi = x.astype(BF16)
    r1 = x - hi.astype(F32)
    mid = r1.astype(BF16)
    lo = (r1 - mid.astype(F32)).astype(BF16)
    return hi, mid, lo


def _dot_sel(sel, x):
    hi, mid, lo = _split3(x)
    d = lambda p: jnp.dot(sel, p, preferred_element_type=F32)
    return d(hi) + d(mid) + d(lo)


def _dot_sel_r(x, sel):
    hi, mid, lo = _split3(x)
    d = lambda p: jnp.dot(p, sel, preferred_element_type=F32)
    return d(hi) + d(mid) + d(lo)


def _sigmoid(x):
    return 1.0 / (1.0 + jnp.exp(-x))


def _silu(x):
    return x * _sigmoid(x)


def _softplus(x):
    return jnp.maximum(x, 0.0) + jnp.log(1.0 + jnp.exp(-jnp.abs(x)))


def _gelu_tanh(x):
    return 0.5 * x * (1.0 + jnp.tanh(math.sqrt(2.0 / math.pi) * (x + 0.044715 * (x * x * x))))


def _rms(x, w, eps=NORM_EPS):
    return x * lax.rsqrt(jnp.mean(x * x, axis=-1, keepdims=True) + eps) * w


def _iota2(shape, dim):
    return lax.broadcasted_iota(jnp.int32, shape, dim)


def _unit_masks(upper_top, upper_bot):
    r = _iota2((UNIT, UNIT), 0)
    c = _iota2((UNIT, UNIT), 1)
    same = (r // CHUNK) == (c // CHUNK)
    sign = jnp.where(r < CHUNK, -1 if upper_top else 1, -1 if upper_bot else 1)
    d = jnp.where(same, (r - c) * sign, -1)
    incl = d >= 0
    strict = d > 0
    blk = (r // INV_BLOCK) == (c // INV_BLOCK)
    eye = jnp.where(r == c, 1.0, 0.0).astype(F32)
    return incl, strict, blk, same, eye


def _stage_unit(u, a, blk, eye, p_scr, t_scr, z_scr):
    ad = jnp.where(blk, a, 0.0)
    p_scr[u] = ad.astype(BF16)
    t_scr[u] = eye + ad
    z_scr[u, :, :UNIT] = jnp.where(blk, 0.0, a).astype(BF16)


def _solve_units(nu, p_scr, t_scr, z_scr, x_scr, y_scr):
    assert CHUNK // INV_BLOCK == 4
    dot = lambda a, b: jnp.dot(a, b, preferred_element_type=F32)
    for _ in range(int(math.log2(INV_BLOCK)) - 1):
        for u in range(nu):
            p = p_scr[u]
            p_scr[u] = dot(p, p).astype(BF16)
        for u in range(nu):
            t = t_scr[u]
            t_scr[u] = t + dot(t.astype(BF16), p_scr[u])
    for u in range(nu):
        t = t_scr[u].astype(BF16)
        x_scr[0, u] = dot(t, z_scr[u, :, :UNIT]).astype(BF16)
        y_scr[u] = dot(t, z_scr[u, :, UNIT:])
    for u in range(nu):
        x = x_scr[0, u]
        x_scr[1, u] = dot(x, x).astype(BF16)
        y = y_scr[u]
        y_scr[u] = y + dot(x, y.astype(BF16))
    for u in range(nu):
        y = y_scr[u]
        y_scr[u] = y + dot(x_scr[1, u], y.astype(BF16))


def _solve_scratch(nu):
    return [pltpu.VMEM((nu, UNIT, UNIT), BF16), pltpu.VMEM((nu, UNIT, UNIT), F32),
            pltpu.VMEM((nu, UNIT, 3 * UNIT), BF16), pltpu.VMEM((2, nu, UNIT, UNIT), BF16),
            pltpu.VMEM((nu, UNIT, 2 * UNIT), F32)]


SOLVE_SCRATCH_BYTES_PER_UNIT = UNIT * UNIT * (2 + 4 + 6 + 4 + 8)


def _fill_halo(xe_ref, x_ref, prev_ref, next_ref, i, nblk):
    rows = x_ref.shape[1]
    pf = jnp.where(i > 0, 1.0, 0.0).astype(F32)
    nf = jnp.where(i < nblk - 1, 1.0, 0.0).astype(F32)
    xe_ref[0:HALO, :] = prev_ref[0] * pf
    xe_ref[HALO:HALO + rows, :] = x_ref[0]
    xe_ref[HALO + rows:2 * HALO + rows, :] = next_ref[0] * nf


def _halo_specs(rows, width, t_len):
    per = rows // HALO
    last = t_len // HALO - 1
    return [
        pl.BlockSpec((1, rows, width), lambda b, i: (b, i, 0)),
        pl.BlockSpec((1, HALO, width), lambda b, i: (b, jnp.maximum(i * per - 1, 0), 0)),
        pl.BlockSpec((1, HALO, width), lambda b, i: (b, jnp.minimum((i + 1) * per, last), 0)),
    ]


def _const_spec(shape):
    nd = len(shape)
    return pl.BlockSpec(shape, lambda *_: (0,) * nd)


def _proj_kernel(h_ref, nw_ref, w_ref, *o_refs, splits):
    hn = _rms(h_ref[...], nw_ref[...]).astype(BF16)
    off = 0
    for o_ref, n in zip(o_refs, splits):
        o_ref[...] = jnp.dot(hn, w_ref[:, off:off + n], preferred_element_type=F32)
        off += n


def _proj(h2, nw, w, splits, tm=512):
    m, d = h2.shape
    n = w.shape[1]
    return pl.pallas_call(
        functools.partial(_proj_kernel, splits=splits),
        grid=(m // tm,),
        in_specs=[pl.BlockSpec((tm, d), lambda i: (i, 0)), _const_spec((1, d)), _const_spec((d, n))],
        out_specs=[pl.BlockSpec((tm, s), lambda i: (i, 0)) for s in splits],
        out_shape=[jax.ShapeDtypeStruct((m, s), F32) for s in splits],
        compiler_params=_params(("parallel",), tm * d * 4, d * n * 2, tm * n * 4, temps=tm * d * 8),
        name="proj",
    )(h2, nw, w)


def _ffn_ple_kernel(h_ref, p_ref, nw_ref, wg_ref, wu_ref, wd_ref, npl_ref, wp_ref, wpg_ref, *rest, final):
    if final:
        nf_ref, o_ref, hn_ref, acc_ref = rest
    else:
        o_ref, hn_ref, acc_ref = rest
    f = pl.program_id(1)

    @pl.when(f == 0)
    def _():
        hn_ref[...] = _rms(h_ref[...], nw_ref[...]).astype(BF16)
        acc_ref[...] = h_ref[...]

    hn = hn_ref[...]
    g = jnp.dot(hn, wg_ref[...], preferred_element_type=F32)
    u = jnp.dot(hn, wu_ref[...], preferred_element_type=F32)
    acc_ref[...] += jnp.dot((_silu(g) * u).astype(BF16), wd_ref[...], preferred_element_type=F32)

    @pl.when(f == pl.num_programs(1) - 1)
    def _():
        h = acc_ref[...]
        gate = _sigmoid(jnp.dot(_rms(h, npl_ref[...]).astype(BF16), wpg_ref[...], preferred_element_type=F32))
        e = jnp.dot(p_ref[...].astype(BF16), wp_ref[...], preferred_element_type=F32)
        out = h + e * gate
        if final:
            out = _rms(out, nf_ref[...])
        o_ref[...] = out


def _ffn_ple(h2, p2, nw, wg, wu, wd, npl, wp, wpg, nf=None, tm=512):
    m, d = h2.shape
    ff = wg.shape[1]
    pd = p2.shape[1]
    tf = ff // 2 if (ff // 2) % LANES == 0 else ff
    final = nf is not None
    ins = [h2, p2, nw, wg, wu, wd, npl, wp, wpg] + ([nf] if final else [])
    specs = [
        pl.BlockSpec((tm, d), lambda i, f: (i, 0)),
        pl.BlockSpec((tm, pd), lambda i, f: (i, 0)),
        _const_spec((1, d)),
        pl.BlockSpec((d, tf), lambda i, f: (0, f)),
        pl.BlockSpec((d, tf), lambda i, f: (0, f)),
        pl.BlockSpec((tf, d), lambda i, f: (f, 0)),
        _const_spec((1, d)), _const_spec((pd, d)), _const_spec((d, d)),
    ] + ([_const_spec((1, d))] if final else [])
    return pl.pallas_call(
        functools.partial(_ffn_ple_kernel, final=final),
        grid=(m // tm, ff // tf),
        in_specs=specs,
        out_specs=pl.BlockSpec((tm, d), lambda i, f: (i, 0)),
        out_shape=jax.ShapeDtypeStruct((m, d), F32),
        scratch_shapes=[pltpu.VMEM((tm, d), BF16), pltpu.VMEM((tm, d), F32)],
        compiler_params=_params(("parallel", "arbitrary"), tm * d * 4, tm * pd * 4, 3 * d * tf * 2, pd * d * 2,
                                d * d * 2, tm * d * 4, scratch=tm * d * 6, temps=3 * tm * tf * 4),
        name="ffn_ple",
    )(*ins)


def _gmlp_kernel(h_ref, nw_ref, wi_ref, lnw_ref, lnb_ref, ws_ref, bs_ref, wo_ref, o_ref, *, chunk, groups):
    h = h_ref[...]
    rows, d = h.shape
    cw = wo_ref.shape[0]
    gd = cw // groups
    hn = _rms(h, nw_ref[...]).astype(BF16)
    uv = _gelu_tanh(jnp.dot(hn, wi_ref[...], preferred_element_type=F32))
    u = uv[:, :cw]
    v = uv[:, cw:]
    mu = jnp.mean(v, axis=-1, keepdims=True)
    vc = v - mu
    var = jnp.mean(vc * vc, axis=-1, keepdims=True)
    vn = (vc * lax.rsqrt(var + C_LN_EPS) * lnw_ref[...] + lnb_ref[...]).astype(BF16)
    bs = bs_ref[...]
    pieces = []
    for c in range(rows // chunk):
        vrow = vn[c * chunk:(c + 1) * chunk]
        sv = jnp.concatenate(
            [jnp.dot(ws_ref[g], vrow[:, g * gd:(g + 1) * gd], preferred_element_type=F32) for g in range(groups)],
            axis=1)
        pieces.append((u[c * chunk:(c + 1) * chunk] * (sv + bs)).astype(BF16))
    gated = jnp.concatenate(pieces, axis=0)
    o_ref[...] = h + jnp.dot(gated, wo_ref[...], preferred_element_type=F32)


def _gmlp(h2, nw, wi, lnw, lnb, ws, bs_full, wo, tm=512):
    m, d = h2.shape
    groups, chunk, _ = ws.shape
    cw = wo.shape[0]
    return pl.pallas_call(
        functools.partial(_gmlp_kernel, chunk=chunk, groups=groups),
        grid=(m // tm,),
        in_specs=[pl.BlockSpec((tm, d), lambda i: (i, 0)), _const_spec((1, d)), _const_spec(wi.shape),
                  _const_spec((1, cw)), _const_spec((1, cw)), _const_spec(ws.shape), _const_spec(bs_full.shape),
                  _const_spec(wo.shape)],
        out_specs=pl.BlockSpec((tm, d), lambda i: (i, 0)),
        out_shape=jax.ShapeDtypeStruct((m, d), F32),
        compiler_params=_params(("parallel",), tm * d * 4, wi.size * 2, ws.size * 2, bs_full.size * 4, wo.size * 2,
                                tm * d * 4, temps=5 * tm * 2 * cw * 4),
        name="gmlp",
    )(h2, nw, wi, lnw, lnb, ws, bs_full, wo)


def _delta_prep_kernel(x_ref, xp_ref, xn_ref, ab_ref, cw_ref, alog_ref, dtb_ref,
                       wq_ref, u_ref, aq_ref, bt_ref, gm_ref,
                       xe_ref, q_scr, k_scr, v_scr, g_scr, b_scr, p_scr, t_scr, z_scr, x_scr, y_scr,
                       *, heads, taps, group):
    i = pl.program_id(1)
    rows = x_ref.shape[1]
    width = x_ref.shape[2]
    aw = width // 3
    hd = aw // heads
    _fill_halo(xe_ref, x_ref, xp_ref, xn_ref, i, pl.num_programs(1))

    acc = jnp.zeros((rows, width), F32)
    for j in range(taps):
        s = HALO - taps // 2 + j
        acc = acc + cw_ref[j:j + 1, :] * xe_ref[s:s + rows, :]
    y = _silu(acc)
    for hh in range(heads):
        qh = y[:, hh * hd:(hh + 1) * hd]
        kh = y[:, aw + hh * hd:aw + (hh + 1) * hd]
        q_scr[hh] = qh * lax.rsqrt(jnp.sum(qh * qh, axis=-1, keepdims=True) + L2_EPS) * (hd ** -0.5)
        k_scr[hh] = kh * lax.rsqrt(jnp.sum(kh * kh, axis=-1, keepdims=True) + L2_EPS)
        v_scr[hh] = y[:, 2 * aw + hh * hd:2 * aw + (hh + 1) * hd]
    ab = ab_ref[0]
    g_scr[...] = -jnp.exp(alog_ref[...]) * _softplus(ab + dtb_ref[...])
    b_scr[...] = _sigmoid(ab)

    incl, strict, blk, _, eye = _unit_masks(False, True)
    r1 = _iota2((UNIT, 1), 0)
    top_col = r1 < CHUNK
    top_row = _iota2((1, UNIT), 1) < CHUNK
    tri_r = _iota2((UNIT, CHUNK), 0)
    tri_c = _iota2((UNIT, CHUNK), 1)
    cum_sel = jnp.where(jnp.where(tri_r < CHUNK, tri_r - tri_c, tri_c - tri_r + CHUNK) >= 0, 1.0, 0.0).astype(BF16)

    def stage_chunk(c, u0):
        r0 = pl.multiple_of(c * CHUNK, CHUNK)
        gst = _dot_sel(cum_sel, g_scr[pl.ds(r0, CHUNK), :])
        gst_t = gst.T
        bch = b_scr[pl.ds(r0, CHUNK), :]
        bst = jnp.concatenate([bch, bch], axis=0)
        for hh in range(heads):
            fw, bw = hh, heads + hh
            gcs = jnp.where(top_col, gst[:, fw:fw + 1], gst[:, bw:bw + 1])
            grow = jnp.where(top_row, gst_t[fw:fw + 1, :], gst_t[bw:bw + 1, :])
            bcol = jnp.where(top_col, bst[:, 2 * heads + fw:2 * heads + fw + 1],
                             bst[:, 2 * heads + bw:2 * heads + bw + 1])
            glast = jnp.where(top_col, gst[CHUNK - 1:CHUNK, fw:fw + 1], gst[CHUNK:CHUNK + 1, bw:bw + 1])
            q = q_scr[hh, pl.ds(r0, CHUNK), :]
            k = k_scr[hh, pl.ds(r0, CHUNK), :]
            v = v_scr[hh, pl.ds(r0, CHUNK), :]
            qst = jnp.concatenate([q, q], axis=0)
            kst = jnp.concatenate([k, k], axis=0)
            vst = jnp.concatenate([v, v], axis=0)
            kst_b = kst.astype(BF16)
            decay = jnp.where(incl, jnp.exp(jnp.where(incl, gcs - grow, 0.0)), 0.0)
            kk = _dot_nt(kst_b, kst_b)
            m = jnp.where(strict, bcol * kk * decay, 0.0)
            _stage_unit(u0 + hh, -m, blk, eye, p_scr, t_scr, z_scr)
            eg = jnp.exp(gcs)
            z_scr[u0 + hh, :, UNIT:2 * UNIT] = (vst * bcol).astype(BF16)
            z_scr[u0 + hh, :, 2 * UNIT:] = (kst * (bcol * eg)).astype(BF16)
            qk = _dot_nt(qst, kst_b) * decay
            qd = qst * eg
            kdec = kst * jnp.exp(glast - gcs)
            gam = jnp.exp(glast)
            for dr in range(2):
                sl = slice(dr * CHUNK, (dr + 1) * CHUNK)
                wq_ref.at[dr, 0, hh, c][CHUNK:, :] = qd[sl].astype(BF16)
                aq_ref[dr, 0, hh, c] = qk[sl].astype(BF16)
                bt_ref[dr, 0, hh, c] = kdec[sl].T.astype(BF16)
                gm_ref[dr, 0, hh, c] = jnp.broadcast_to(gam[dr * CHUNK:dr * CHUNK + 1], (UNIT, GAM_LANES))

    def finish_chunk(c, u0):
        for hh in range(heads):
            y = y_scr[u0 + hh]
            for dr in range(2):
                sl = slice(dr * CHUNK, (dr + 1) * CHUNK)
                wq_ref.at[dr, 0, hh, c][:CHUNK, :] = (-y[sl, hd:]).astype(BF16)
                u_ref[dr, 0, hh, c] = y[sl, :hd]

    def group_body(gi, carry):
        for cc in range(group):
            stage_chunk(gi * group + cc, cc * heads)
        _solve_units(group * heads, p_scr, t_scr, z_scr, x_scr, y_scr)
        for cc in range(group):
            finish_chunk(gi * group + cc, cc * heads)
        return carry

    lax.fori_loop(0, rows // (CHUNK * group), group_body, 0)


def _delta_prep(qkv, ab, conv_w, alog_row, dtb_row, heads, rows=256):
    b, t, width = qkv.shape
    hd = width // 3 // heads
    assert hd == LANES
    n = t // CHUNK
    nb = rows // CHUNK
    taps = conv_w.shape[0]
    cw = jnp.zeros((SUBLANES, width), F32).at[:taps].set(conv_w)
    group = UNITS_IN_FLIGHT // heads
    assert nb % group == 0
    unit = lambda r, dt, w=LANES: jax.ShapeDtypeStruct((2, b, heads, n, r, w), dt)
    ospec = lambda r, w=LANES: pl.BlockSpec((2, 1, heads, nb, r, w), lambda bi, i: (0, bi, 0, i, 0, 0))
    return pl.pallas_call(
        functools.partial(_delta_prep_kernel, heads=heads, taps=taps, group=group),
        grid=(b, t // rows),
        in_specs=_halo_specs(rows, width, t) + [
            pl.BlockSpec((1, rows, LANES), lambda bi, i: (bi, i, 0)),
            _const_spec((SUBLANES, width)), _const_spec((1, LANES)), _const_spec((1, LANES))],
        out_specs=[ospec(UNIT), ospec(CHUNK), ospec(CHUNK), ospec(UNIT, CHUNK), ospec(UNIT, GAM_LANES)],
        out_shape=[unit(UNIT, BF16), unit(CHUNK, F32), unit(CHUNK, BF16), unit(UNIT, BF16, CHUNK),
                   unit(UNIT, F32, GAM_LANES)],
        scratch_shapes=[pltpu.VMEM((rows + 2 * HALO, width), F32)]
        + [pltpu.VMEM((heads, rows, LANES), F32)] * 3 + [pltpu.VMEM((rows, LANES), F32)] * 2
        + _solve_scratch(group * heads),
        compiler_params=_params(("parallel", "arbitrary"), rows * width * 4, rows * LANES * 4,
                                2 * heads * nb * (UNIT * 2 + CHUNK * 4 + CHUNK * 2 + UNIT * 2 + UNIT * 4) * LANES,
                                scratch=(2 * rows + 2 * HALO) * width * 4
                                + group * heads * SOLVE_SCRATCH_BYTES_PER_UNIT, temps=4 * rows * width * 4),
        name="delta_prep",
    )(qkv, qkv, qkv, ab, cw, alog_row, dtb_row)


def _rwkv_prep_kernel(x_ref, xp_ref, xn_ref, l_ref, lp_ref, ln_ref, mux_ref, mul_ref, wcat_ref, bias_ref, g2_ref,
                      kk_w_ref, ka_ref, rk_ref,
                      wq_ref, u0_ref, aq_ref, bk_ref, vb_ref, gm_ref, bonus_ref, gate_ref,
                      xe_ref, le_ref, r_scr, v_scr, kk_scr, lw_scr, kka_scr, kd_scr, kr_scr, vs_scr,
                      p_scr, t_scr, z_scr, x_scr, y_scr, *, lora_w, lora_a):
    i = pl.program_id(1)
    nblk = pl.num_programs(1)
    rows = x_ref.shape[1]
    bw = x_ref.shape[2] // 3
    pairs = bw // LANES
    _fill_halo(xe_ref, x_ref, xp_ref, xn_ref, i, nblk)
    _fill_halo(le_ref, l_ref, lp_ref, ln_ref, i, nblk)

    def shifted(e_ref, mu):
        x = e_ref[HALO:HALO + rows, :]
        nb = 0.5 * (e_ref[HALO - 1:HALO - 1 + rows, :] + e_ref[HALO + 1:HALO + 1 + rows, :])
        return x + (nb - x) * mu

    x = shifted(xe_ref, mux_ref[...])
    lo = shifted(le_ref, mul_ref[...])
    r = x[:, :bw]
    k = x[:, bw:2 * bw]
    v = x[:, 2 * bw:]
    l01 = lo[:, :LANES]
    lane = _iota2((1, LANES), 1)
    xin = jnp.where(lane < 2 * lora_w, jnp.tanh(l01), l01)
    wa = _dot(xin, wcat_ref[...]) + bias_ref[...]
    gate_ref[0] = _dot(_sigmoid(lo[:, LANES:]), g2_ref[...])

    sr = _iota2((LANES, LANES), 0) // CHUNK
    sc = _iota2((LANES, LANES), 1) // CHUNK
    head_sel = jnp.where(sr == sc, 1.0, 0.0).astype(BF16)

    def head_sum(a):
        return jnp.concatenate(
            [_dot_sel_r(a[:, p * LANES:(p + 1) * LANES], head_sel) for p in range(pairs)], axis=1)

    kkp = k * kk_w_ref[...]
    kk = kkp * lax.rsqrt(head_sum(kkp * kkp) + L2_EPS)
    r_scr[...] = r
    v_scr[...] = v
    kk_scr[...] = kk
    ka = ka_ref[...]
    kb = jnp.zeros_like(k)
    for e in range(2):
        a = _sigmoid(wa[:, (2 + e) * bw:(3 + e) * bw])
        lw_scr[e] = -B_DECAY_SCALE * _sigmoid(wa[:, e * bw:(e + 1) * bw])
        kka_scr[e] = kk * a
        kd = k * (1.0 + (a - 1.0) * ka)
        kd_scr[e] = kd
        kb = kb + 0.5 * kd
    bonus_ref[0] = head_sum(r * kb * rk_ref[...]) * v

    lanes2 = _iota2((1, LANES), 1)
    m0 = lanes2 < CHUNK
    tr = _iota2((CHUNK, CHUNK), 0)
    tc = _iota2((CHUNK, CHUNK), 1)
    cum_sel = (jnp.where(tc <= tr, 1.0, 0.0).astype(BF16), jnp.where(tc >= tr, 1.0, 0.0).astype(BF16))
    masks = (_unit_masks(False, False), _unit_masks(True, True))

    def stack(a):
        return jnp.concatenate([jnp.where(m0, a, 0.0), jnp.where(m0, 0.0, a)], axis=0)

    def halves(a):
        return a[:CHUNK] + a[CHUNK:]

    def chunk_body(c, carry):
        r0 = pl.multiple_of(c * CHUNK, CHUNK)
        rows_c = pl.ds(r0, CHUNK)
        for e in range(2):
            incl, strict, blk, same, eye = masks[e]
            lw_c = lw_scr[e, rows_c, :]
            cs_c = _dot_sel(cum_sel[e], lw_c)
            last = CHUNK - 1 if e == 0 else 0
            for p in range(pairs):
                ls = slice(p * LANES, (p + 1) * LANES)
                lw = lw_c[:, ls]
                cs = cs_c[:, ls]
                cl = cs_c[last:last + 1, ls]
                rr = r_scr[rows_c, ls]
                vv = v_scr[rows_c, ls]
                kk_ = kk_scr[rows_c, ls]
                kka = kka_scr[e, rows_c, ls]
                kd = kd_scr[e, rows_c, ls]
                u = e * pairs + p
                e_neg = jnp.exp(-cs)
                e_last = jnp.exp(cl - cs)
                xs = stack(-kk_ * jnp.exp(cs - lw)).astype(BF16)
                rt = rr * jnp.exp(cs)
                bk = jnp.concatenate([stack(kka * e_neg), stack(kd * e_neg)], axis=0).astype(BF16)
                ga = _dot_nt(xs, bk)
                gr = _dot_nt(stack(rt), bk)
                _stage_unit(u, jnp.where(strict, ga[:, :UNIT], 0.0), blk, eye, p_scr, t_scr, z_scr)
                z_scr[u, :, UNIT:2 * UNIT] = xs
                kr_scr[u] = jnp.where(strict, ga[:, UNIT:], 0.0).astype(BF16)
                vs_scr[u] = stack(vv).astype(BF16)
                aq_ref[e, 0, p, c] = jnp.concatenate(
                    [halves(jnp.where(incl, gr[:, :UNIT], 0.0)), halves(jnp.where(incl, gr[:, UNIT:], 0.0))],
                    axis=1).astype(BF16)
                wq_ref.at[e, 0, p, c][CHUNK:, :] = rt.astype(BF16)
                vb_ref[e, 0, p, c] = vv.astype(BF16)
                bk_ref[e, 0, p, c] = jnp.concatenate([kka * e_last, kd * e_last], axis=0).T.astype(BF16)
                gm_ref[e, 0, p, c] = jnp.broadcast_to(jnp.exp(cl), (UNIT, LANES)).T[:, :GAM_LANES]
        for e in range(2):
            for p in range(pairs):
                u = e * pairs + p
                av = jnp.dot(kr_scr[u], vs_scr[u], preferred_element_type=F32)
                z_scr[u, :, 2 * UNIT:] = av.astype(BF16)
        _solve_units(2 * pairs, p_scr, t_scr, z_scr, x_scr, y_scr)
        for e in range(2):
            for p in range(pairs):
                y = y_scr[e * pairs + p]
                wq_ref.at[e, 0, p, c][:CHUNK, :] = halves(y[:, :LANES]).astype(BF16)
                u0_ref[e, 0, p, c] = halves(y[:, LANES:])
        return carry

    lax.fori_loop(0, rows // CHUNK, chunk_body, 0)


def _rwkv_prep(rkv, lora, mu_x, mu_l, wcat, bias, g2p, k_k, k_a, r_k, lora_w, lora_a, rows=256):
    b, t, width = rkv.shape
    bw = width // 3
    pairs = bw // LANES
    lw_ = lora.shape[2]
    n = t // CHUNK
    nb = rows // CHUNK
    unit = lambda r, dt, w=LANES: jax.ShapeDtypeStruct((2, b, pairs, n, r, w), dt)
    ospec = lambda r, w=LANES: pl.BlockSpec((2, 1, pairs, nb, r, w), lambda bi, i: (0, bi, 0, i, 0, 0))
    row_spec = pl.BlockSpec((1, rows, bw), lambda bi, i: (bi, i, 0))
    return pl.pallas_call(
        functools.partial(_rwkv_prep_kernel, lora_w=lora_w, lora_a=lora_a),
        grid=(b, t // rows),
        in_specs=_halo_specs(rows, width, t) + _halo_specs(rows, lw_, t) + [
            _const_spec((1, width)), _const_spec((1, lw_)), _const_spec(wcat.shape), _const_spec(bias.shape),
            _const_spec(g2p.shape), _const_spec((1, bw)), _const_spec((1, bw)), _const_spec((1, bw))],
        out_specs=[ospec(UNIT), ospec(CHUNK), ospec(CHUNK, 2 * LANES), ospec(UNIT), ospec(CHUNK),
                   ospec(UNIT, GAM_LANES), row_spec, row_spec],
        out_shape=[unit(UNIT, BF16), unit(CHUNK, F32), unit(CHUNK, BF16, 2 * LANES), unit(UNIT, BF16),
                   unit(CHUNK, BF16), unit(UNIT, F32, GAM_LANES),
                   jax.ShapeDtypeStruct((b, t, bw), F32), jax.ShapeDtypeStruct((b, t, bw), F32)],
        scratch_shapes=[pltpu.VMEM((rows + 2 * HALO, width), F32), pltpu.VMEM((rows + 2 * HALO, lw_), F32)]
        + [pltpu.VMEM((rows, bw), F32)] * 3 + [pltpu.VMEM((2, rows, bw), F32)] * 3
        + [pltpu.VMEM((2 * pairs, UNIT, UNIT), BF16), pltpu.VMEM((2 * pairs, UNIT, LANES), BF16)]
        + _solve_scratch(2 * pairs),
        compiler_params=_params(("parallel", "arbitrary"), rows * width * 4, rows * lw_ * 4, wcat.size * 2,
                                2 * pairs * nb * (UNIT * 2 + CHUNK * 4 + CHUNK * 4 + UNIT * 2 + CHUNK * 2 + UNIT * 4)
                                * LANES, 2 * rows * bw * 4,
                                scratch=(rows + 2 * HALO) * (width + lw_) * 4 + 9 * rows * bw * 4
                                + 2 * pairs * (SOLVE_SCRATCH_BYTES_PER_UNIT + 2 * UNIT * LANES * 2),
                                temps=8 * rows * bw * 4 * 4),
        name="rwkv_prep",
    )(rkv, rkv, rkv, lora, lora, lora, mu_x, mu_l, wcat, bias, g2p, k_k, k_a, r_k)


def _seq_kernel(*refs, paired, ub):
    if paired:
        wq_ref, u_ref, aq_ref, bt_ref, gm_ref, vb_ref, y_ref, s_ref, pq_ref = refs
    else:
        wq_ref, u_ref, aq_ref, bt_ref, gm_ref, y_ref, s_ref, pq_ref = refs

    @pl.when(pl.program_id(1) == 0)
    def _():
        s_ref[...] = jnp.zeros_like(s_ref)

    m0 = _iota2((1, LANES), 1) < CHUNK
    same = (_iota2((LANES, LANES), 0) // CHUNK) == (_iota2((LANES, LANES), 1) // CHUNK)
    for j in range(ub):
        pq_ref[j] = jnp.dot(wq_ref[j, 0], s_ref[j].astype(BF16), preferred_element_type=F32)
    for j in range(ub):
        s = s_ref[j]
        pq = pq_ref[j]
        un = u_ref[j, 0] + pq[:CHUNK]
        unb = un.astype(BF16)
        if paired:
            vb = vb_ref[j, 0]
            zero = jnp.zeros_like(unb)
            rhs_y = jnp.concatenate([jnp.where(m0, unb, zero), jnp.where(m0, zero, unb),
                                     jnp.where(m0, vb, zero), jnp.where(m0, zero, vb)], axis=0)
            rhs_s = jnp.concatenate([unb, vb], axis=0)
        else:
            rhs_y = jnp.concatenate([unb, unb], axis=0)
            rhs_s = unb
        y_ref[j, 0] = pq[CHUNK:] + jnp.dot(aq_ref[j, 0], rhs_y, preferred_element_type=F32)
        upd = jnp.dot(bt_ref[j, 0], rhs_s, preferred_element_type=F32)
        if paired:
            upd = jnp.where(same, upd, 0.0)
        s_ref[j] = s * jnp.broadcast_to(gm_ref[j, 0][:, 0:1], (LANES, LANES)) + upd


def _seq(ops, per_dir, paired, ub=16):
    ub = math.gcd(ub, per_dir)
    g, n = ops[0].shape[0], ops[0].shape[1]

    def imap(gi, ni):
        back = (gi * ub) // per_dir
        return (gi, ni + back * (n - 1 - 2 * ni), 0, 0)

    specs = [pl.BlockSpec((ub, 1) + a.shape[2:], imap) for a in ops]
    block_bytes = [ub * math.prod(a.shape[2:]) * a.dtype.itemsize for a in ops]
    return pl.pallas_call(
        functools.partial(_seq_kernel, paired=paired, ub=ub),
        grid=(g // ub, n),
        in_specs=specs,
        out_specs=pl.BlockSpec((ub, 1, CHUNK, LANES), imap),
        out_shape=jax.ShapeDtypeStruct((g, n, CHUNK, LANES), F32),
        scratch_shapes=[pltpu.VMEM((ub, LANES, LANES), F32), pltpu.VMEM((ub, UNIT, LANES), F32)],
        compiler_params=_params(("parallel", "arbitrary"), *block_bytes, ub * UNIT * LANES * 4,
                                ub * CHUNK * LANES * 4, scratch=ub * (LANES + UNIT) * LANES * 4),
        name="seq_rwkv" if paired else "seq_delta",
    )(*ops)


def _even_out_kernel(of_ref, ob_ref, yf_ref, yb_ref, z_ref, bonus_ref, gate_ref, h_ref, an_ref, lnw_ref, lnb_ref,
                     wo_ref, o_ref):
    rows = h_ref.shape[1]
    a_heads = of_ref.shape[2]
    pairs = yf_ref.shape[2]
    z = z_ref[0]
    pieces = []
    for hh in range(a_heads):
        o = (of_ref[0, 0, hh] + ob_ref[0, 0, hh]).reshape(rows, LANES)
        pieces.append(_rms(o, an_ref[...]) * _silu(z[:, hh * LANES:(hh + 1) * LANES]))
    sr = _iota2((LANES, LANES), 0) // CHUNK
    sc = _iota2((LANES, LANES), 1) // CHUNK
    head_sel = jnp.where(sr == sc, 1.0, 0.0).astype(BF16)
    inv_n = 1.0 / CHUNK
    for p in range(pairs):
        ls = slice(p * LANES, (p + 1) * LANES)
        y = (yf_ref[0, 0, p] + yb_ref[0, 0, p]).reshape(rows, LANES)
        mu = _dot_sel_r(y, head_sel) * inv_n
        yc = y - mu
        var = _dot_sel_r(yc * yc, head_sel) * inv_n
        yn = yc * lax.rsqrt(var + B_GN_EPS) * lnw_ref[:, ls] + lnb_ref[:, ls]
        pieces.append((yn + bonus_ref[0][:, ls]) * gate_ref[0][:, ls])
    mix = jnp.concatenate(pieces, axis=1).astype(BF16)
    o_ref[0] = h_ref[0] + jnp.dot(mix, wo_ref[...], preferred_element_type=F32)


def _even_out(od, yr, z, bonus, gate, h3, a_norm, ln_w, ln_b, wo, rows=256):
    b, t, d = h3.shape
    nb = rows // CHUNK
    a_heads, pairs = od.shape[2], yr.shape[2]
    aw, bw = z.shape[2], bonus.shape[2]
    dspec = lambda dr, nh: pl.BlockSpec((1, 1, nh, nb, CHUNK, LANES), lambda bi, i: (dr, bi, 0, i, 0, 0))
    rspec = lambda w: pl.BlockSpec((1, rows, w), lambda bi, i: (bi, i, 0))
    return pl.pallas_call(
        _even_out_kernel,
        grid=(b, t // rows),
        in_specs=[dspec(0, a_heads), dspec(1, a_heads), dspec(0, pairs), dspec(1, pairs), rspec(aw), rspec(bw),
                  rspec(bw), rspec(d), _const_spec((1, LANES)), _const_spec((1, bw)), _const_spec((1, bw)),
                  _const_spec(wo.shape)],
        out_specs=rspec(d),
        out_shape=jax.ShapeDtypeStruct((b, t, d), F32),
        compiler_params=_params(("parallel", "parallel"), 2 * rows * (aw + bw) * 4, rows * (aw + 2 * bw + 2 * d) * 4,
                                wo.size * 2, temps=6 * rows * d * 4),
        name="even_out",
    )(od, od, yr, yr, z, bonus, gate, h3, a_norm, ln_w, ln_b, wo)


def _even_layer(h3, nw, w_in, w_out, a_conv, a_log, a_dt_bias, a_norm, b_mu, b_w0, b_w2, b_a0, b_a2, b_g2,
                b_k_k, b_k_a, b_r_k, b_ln_w, b_ln_b):
    b, t, d = h3.shape
    a_heads = a_log.shape[1]
    hd = a_norm.shape[0]
    aw = a_heads * hd
    b_heads, b_hd = b_r_k.shape
    bw = b_heads * b_hd
    lora_w, lora_a, lora_g = b_w2.shape[1], b_a2.shape[1], b_g2.shape[0]
    assert b_hd == CHUNK and hd == LANES and 2 * (lora_w + lora_a) == LANES and lora_g <= LANES
    a_cols = 4 * aw + 4 * a_heads
    n_gate = 4 * a_heads
    pad_cols = lambda w, n: jnp.pad(w, ((0, 0), (0, n - w.shape[1])))
    wa, wb = w_in[:, :a_cols], w_in[:, a_cols:]
    w_all = jnp.concatenate([wa[:, :4 * aw], pad_cols(wa[:, 4 * aw:], LANES),
                             wb[:, :3 * bw], pad_cols(wb[:, 3 * bw:], 2 * LANES)], axis=1).astype(BF16)
    splits = (3 * aw, aw, LANES, 3 * bw, 2 * LANES)
    qkv, z, ab, rkv, lora = _proj(h3.reshape(b * t, d), nw.reshape(1, d), w_all, splits)
    to3 = lambda a: a.reshape(b, t, a.shape[-1])

    alog_row = jnp.zeros((1, LANES), F32).at[0, :2 * a_heads].set(a_log.reshape(-1))
    dtb_row = jnp.zeros((1, LANES), F32).at[0, :2 * a_heads].set(a_dt_bias.reshape(-1))
    d_ops = _delta_prep(to3(qkv), to3(ab), a_conv, alog_row, dtb_row, a_heads)
    n = t // CHUNK
    flat = lambda a: a.reshape((-1, n) + a.shape[4:])
    od = _seq([flat(a) for a in d_ops], b * a_heads, paired=False)
    od = od.reshape(2, b, a_heads, n, CHUNK, LANES)

    wcat = jnp.zeros((LANES, 4 * bw), F32)
    for e in range(2):
        wcat = wcat.at[e * lora_w:(e + 1) * lora_w, e * bw:(e + 1) * bw].set(b_w2[e])
        wcat = wcat.at[2 * lora_w + e * lora_a:2 * lora_w + (e + 1) * lora_a, (2 + e) * bw:(3 + e) * bw].set(b_a2[e])
    bias = jnp.concatenate([b_w0[0], b_w0[1], b_a0[0], b_a0[1]]).reshape(1, 4 * bw)
    g2p = jnp.zeros((LANES, bw), F32).at[:lora_g].set(b_g2).astype(BF16)
    mu_x = b_mu[:3 * bw].reshape(1, 3 * bw)
    mu_l = jnp.pad(b_mu[3 * bw:], (0, 2 * LANES - (b_mu.shape[0] - 3 * bw))).reshape(1, 2 * LANES)
    row = lambda a: a.reshape(1, bw)
    r_ops = _rwkv_prep(to3(rkv), to3(lora), mu_x, mu_l, wcat.astype(BF16), bias, g2p, row(b_k_k), row(b_k_a),
                       row(b_r_k), lora_w, lora_a)
    wq, u0, aq, bk, vb, gm, bonus, gate = r_ops
    pairs = bw // LANES
    yr = _seq([flat(a) for a in (wq, u0, aq, bk, gm, vb)], b * pairs, paired=True)
    yr = yr.reshape(2, b, pairs, n, CHUNK, LANES)

    return _even_out(od, yr, to3(z), bonus, gate, h3, a_norm.reshape(1, hd), row(b_ln_w), row(b_ln_b),
                     w_out.astype(BF16))


def _odd_layer(h3, nw, w_in, ln_w, ln_b, ws, bs, w_out):
    b, t, d = h3.shape
    groups, chunk, _ = ws.shape
    cw = w_out.shape[0]
    bs_full = jnp.repeat(bs.T, cw // groups, axis=1)
    out = _gmlp(h3.reshape(b * t, d), nw.reshape(1, d), w_in.astype(BF16), ln_w.reshape(1, cw), ln_b.reshape(1, cw),
                ws.astype(BF16), bs_full, w_out.astype(BF16))
    return out.reshape(b, t, d)


def kernel(x, p, norm_mix, norm_ffn, norm_ple, norm_final, w_in_even, w_out_even, a_conv, a_log, a_dt_bias, a_norm,
           b_mu, b_w0, b_w2, b_a0, b_a2, b_g2, b_k_k, b_k_a, b_r_k, b_ln_w, b_ln_b, w_in_odd, c_ln_w, c_ln_b, c_ws,
           c_bs, w_out_odd, w_gate, w_up, w_down, w_ple, w_ple_gate):
    b, t, d = x.shape
    depth = p.shape[0]
    h = x
    for i in range(depth):
        j = i // 2
        if i % 2 == 0:
            h = _even_layer(h, norm_mix[i], w_in_even[j], w_out_even[j], a_conv[j], a_log[j], a_dt_bias[j], a_norm[j],
                            b_mu[j], b_w0[j], b_w2[j], b_a0[j], b_a2[j], b_g2[j], b_k_k[j], b_k_a[j], b_r_k[j],
                            b_ln_w[j], b_ln_b[j])
        else:
            h = _odd_layer(h, norm_mix[i], w_in_odd[j], c_ln_w[j], c_ln_b[j], c_ws[j], c_bs[j], w_out_odd[j])
        h2 = h.reshape(b * t, d)
        h2 = _ffn_ple(h2, p[i].reshape(b * t, -1), norm_ffn[i].reshape(1, d), w_gate[i].astype(BF16),
                      w_up[i].astype(BF16), w_down[i].astype(BF16), norm_ple[i].reshape(1, d),
                      w_ple[i].astype(BF16), w_ple_gate[i].astype(BF16),
                      norm_final.reshape(1, d) if i == depth - 1 else None)
        h = h2.reshape(b, t, d)
    return h
```

```python
import functools
import math

import jax
import jax.numpy as jnp
from jax import lax
from jax.experimental import pallas as pl
from jax.experimental.pallas import tpu as pltpu

F32 = jnp.float32
BF16 = jnp.bfloat16

NORM_EPS = 1e-6
L2_EPS = 1e-6
B_GN_EPS = 64e-5
B_DECAY_SCALE = 0.606531
C_LN_EPS = 1e-5

LANES = 128
SUBLANES = 8
CHUNK = 64
UNIT = 2 * CHUNK
INV_BLOCK = 16
GAM_LANES = 8
UNITS_IN_FLIGHT = 8
HALO = SUBLANES
VMEM_BUDGET = 56 * 1024 * 1024


def _params(sem, *block_bytes, scratch=0, temps=0, flags=None):
    need = 2 * sum(block_bytes) + scratch + temps
    return pltpu.CompilerParams(dimension_semantics=sem, vmem_limit_bytes=int(min(max(need, 16 << 20), VMEM_BUDGET)),
                                flags=flags)


def _dot(a, b):
    return jnp.dot(a.astype(BF16), b.astype(BF16), preferred_element_type=F32)


def _dot_nt(a, b):
    return lax.dot_general(a.astype(BF16), b.astype(BF16), (((1,), (1,)), ((), ())), preferred_element_type=F32)


def _split3(x):
    hi = x.astype(BF16)
    r1 = x - hi.astype(F32)
    mid = r1.astype(BF16)
    lo = (r1 - mid.astype(F32)).astype(BF16)
    return hi, mid, lo


def _dot_sel(sel, x):
    hi, mid, lo = _split3(x)
    d = lambda p: jnp.dot(sel, p, preferred_element_type=F32)
    return d(hi) + d(mid) + d(lo)


def _dot_sel_r(x, sel, pieces=3):
    d = lambda p: jnp.dot(p, sel, preferred_element_type=F32)
    return sum(d(p) for p in _split3(x)[:pieces])


def _sigmoid(x):
    return 1.0 / (1.0 + jnp.exp(-x))


def _silu(x):
    return x * _sigmoid(x)


def _softplus(x):
    return jnp.maximum(x, 0.0) + jnp.log(1.0 + jnp.exp(-jnp.abs(x)))


def _gelu_tanh(x):
    return 0.5 * x * (1.0 + jnp.tanh(math.sqrt(2.0 / math.pi) * (x + 0.044715 * (x * x * x))))


def _rms(x, w, eps=NORM_EPS):
    return x * lax.rsqrt(jnp.mean(x * x, axis=-1, keepdims=True) + eps) * w


def _iota2(shape, dim):
    return lax.broadcasted_iota(jnp.int32, shape, dim)


def _unit_masks(upper_top, upper_bot):
    r = _iota2((UNIT, UNIT), 0)
    c = _iota2((UNIT, UNIT), 1)
    same = (r // CHUNK) == (c // CHUNK)
    sign = jnp.where(r < CHUNK, -1 if upper_top else 1, -1 if upper_bot else 1)
    d = jnp.where(same, (r - c) * sign, -1)
    incl = d >= 0
    strict = d > 0
    blk = (r // INV_BLOCK) == (c // INV_BLOCK)
    eye = jnp.where(r == c, 1.0, 0.0).astype(F32)
    return incl, strict, blk, same, eye


class _UnitSolver:
    def __init__(self, refs):
        self.p, self.t, self.z, self.x, self.y = refs

    @staticmethod
    def scratch(nu):
        return [pltpu.VMEM((nu, UNIT, UNIT), BF16), pltpu.VMEM((nu, UNIT, UNIT), F32),
                pltpu.VMEM((nu, UNIT, 3 * UNIT), BF16), pltpu.VMEM((2, nu, UNIT, UNIT), BF16),
                pltpu.VMEM((nu, UNIT, 2 * UNIT), F32)]

    BYTES_PER_UNIT = UNIT * UNIT * (2 + 4 + 6 + 4 + 8)

    def stage(self, u, a, blk, eye):
        ad = jnp.where(blk, a, 0.0)
        self.p[u] = ad.astype(BF16)
        self.t[u] = eye + ad
        self.z[u, :, :UNIT] = jnp.where(blk, 0.0, a).astype(BF16)

    def set_rhs(self, u, tile, val):
        self.z[u, :, (tile + 1) * UNIT:(tile + 2) * UNIT] = val.astype(BF16)

    def result(self, u, tile):
        return self.y[u, :, tile * UNIT:(tile + 1) * UNIT]

    def items(self, units):
        assert CHUNK // INV_BLOCK == 4
        dot = lambda a, b: jnp.dot(a, b, preferred_element_type=F32)

        def square(u):
            p = self.p[u]
            self.p[u] = dot(p, p).astype(BF16)

        def grow(u):
            t = self.t[u]
            self.t[u] = t + dot(t.astype(BF16), self.p[u])

        def apply_t(u):
            t = self.t[u].astype(BF16)
            self.x[0, u] = dot(t, self.z[u, :, :UNIT]).astype(BF16)
            self.y[u] = dot(t, self.z[u, :, UNIT:])

        def apply_x(u):
            x = self.x[0, u]
            self.x[1, u] = dot(x, x).astype(BF16)
            y = self.y[u]
            self.y[u] = y + dot(x, y.astype(BF16))

        def apply_x2(u):
            y = self.y[u]
            self.y[u] = y + dot(self.x[1, u], y.astype(BF16))

        stages = [square, grow] * (int(math.log2(INV_BLOCK)) - 1) + [apply_t, apply_x, apply_x2]
        return [functools.partial(stage, u) for stage in stages for u in units]


def _emit(main, side=()):
    n, m = len(main), len(side)
    j = 0
    for idx, step in enumerate(main):
        step()
        while j < m and (j + 1) * n <= (idx + 1) * m:
            side[j]()
            j += 1
    for step in side[j:]:
        step()


def _fill_halo(xe_ref, x_ref, prev_ref, next_ref, i, nblk):
    rows = x_ref.shape[1]
    pf = jnp.where(i > 0, 1.0, 0.0).astype(F32)
    nf = jnp.where(i < nblk - 1, 1.0, 0.0).astype(F32)
    xe_ref[0:HALO, :] = prev_ref[0] * pf
    xe_ref[HALO:HALO + rows, :] = x_ref[0]
    xe_ref[HALO + rows:2 * HALO + rows, :] = next_ref[0] * nf


def _halo_specs(rows, width, t_len):
    per = rows // HALO
    last = t_len // HALO - 1
    return [
        pl.BlockSpec((1, rows, width), lambda b, i: (b, i, 0)),
        pl.BlockSpec((1, HALO, width), lambda b, i: (b, jnp.maximum(i * per - 1, 0), 0)),
        pl.BlockSpec((1, HALO, width), lambda b, i: (b, jnp.minimum((i + 1) * per, last), 0)),
    ]


def _const_spec(shape):
    nd = len(shape)
    return pl.BlockSpec(shape, lambda *_: (0,) * nd)


def _proj_kernel(h_ref, nw_ref, w_ref, *o_refs, splits):
    hn = _rms(h_ref[...], nw_ref[...]).astype(BF16)
    off = 0
    for o_ref, n in zip(o_refs, splits):
        o_ref[...] = jnp.dot(hn, w_ref[:, off:off + n], preferred_element_type=F32)
        off += n


def _proj(h2, nw, w, splits, tm=512):
    m, d = h2.shape
    n = w.shape[1]
    return pl.pallas_call(
        functools.partial(_proj_kernel, splits=splits),
        grid=(m // tm,),
        in_specs=[pl.BlockSpec((tm, d), lambda i: (i, 0)), _const_spec((1, d)), _const_spec((d, n))],
        out_specs=[pl.BlockSpec((tm, s), lambda i: (i, 0)) for s in splits],
        out_shape=[jax.ShapeDtypeStruct((m, s), F32) for s in splits],
        compiler_params=_params(("parallel",), tm * d * 4, d * n * 2, tm * n * 4, temps=tm * d * 8),
        name="proj",
    )(h2, nw, w)


def _ffn_ple_kernel(h_ref, p_ref, nw_ref, wg_ref, wu_ref, wd_ref, npl_ref, wp_ref, wpg_ref, *rest, final):
    if final:
        nf_ref, o_ref, hn_ref, acc_ref = rest
    else:
        o_ref, hn_ref, acc_ref = rest
    f = pl.program_id(1)

    @pl.when(f == 0)
    def _():
        hn_ref[...] = _rms(h_ref[...], nw_ref[...]).astype(BF16)
        acc_ref[...] = h_ref[...]

    hn = hn_ref[...]
    g = jnp.dot(hn, wg_ref[...], preferred_element_type=F32)
    u = jnp.dot(hn, wu_ref[...], preferred_element_type=F32)
    acc_ref[...] += jnp.dot((_silu(g) * u).astype(BF16), wd_ref[...], preferred_element_type=F32)

    @pl.when(f == pl.num_programs(1) - 1)
    def _():
        h = acc_ref[...]
        gate = _sigmoid(jnp.dot(_rms(h, npl_ref[...]).astype(BF16), wpg_ref[...], preferred_element_type=F32))
        e = jnp.dot(p_ref[...].astype(BF16), wp_ref[...], preferred_element_type=F32)
        out = h + e * gate
        if final:
            out = _rms(out, nf_ref[...])
        o_ref[...] = out


def _ffn_ple(h2, p2, nw, wg, wu, wd, npl, wp, wpg, nf=None, tm=512):
    m, d = h2.shape
    ff = wg.shape[1]
    pd = p2.shape[1]
    tf = ff // 2 if (ff // 2) % LANES == 0 else ff
    final = nf is not None
    ins = [h2, p2, nw, wg, wu, wd, npl, wp, wpg] + ([nf] if final else [])
    specs = [
        pl.BlockSpec((tm, d), lambda i, f: (i, 0)),
        pl.BlockSpec((tm, pd), lambda i, f: (i, 0)),
        _const_spec((1, d)),
        pl.BlockSpec((d, tf), lambda i, f: (0, f)),
        pl.BlockSpec((d, tf), lambda i, f: (0, f)),
        pl.BlockSpec((tf, d), lambda i, f: (f, 0)),
        _const_spec((1, d)), _const_spec((pd, d)), _const_spec((d, d)),
    ] + ([_const_spec((1, d))] if final else [])
    return pl.pallas_call(
        functools.partial(_ffn_ple_kernel, final=final),
        grid=(m // tm, ff // tf),
        in_specs=specs,
        out_specs=pl.BlockSpec((tm, d), lambda i, f: (i, 0)),
        out_shape=jax.ShapeDtypeStruct((m, d), F32),
        scratch_shapes=[pltpu.VMEM((tm, d), BF16), pltpu.VMEM((tm, d), F32)],
        compiler_params=_params(("parallel", "arbitrary"), tm * d * 4, tm * pd * 4, 3 * d * tf * 2, pd * d * 2,
                                d * d * 2, tm * d * 4, scratch=tm * d * 6, temps=3 * tm * tf * 4),
        name="ffn_ple",
    )(*ins)


def _gmlp_kernel(h_ref, nw_ref, wi_ref, lnw_ref, lnb_ref, ws_ref, bs_ref, wo_ref, o_ref, *, chunk, groups):
    h = h_ref[...]
    rows, d = h.shape
    cw = wo_ref.shape[0]
    gd = cw // groups
    hn = _rms(h, nw_ref[...]).astype(BF16)
    uv = _gelu_tanh(jnp.dot(hn, wi_ref[...], preferred_element_type=F32))
    u = uv[:, :cw]
    v = uv[:, cw:]
    mu = jnp.mean(v, axis=-1, keepdims=True)
    vc = v - mu
    var = jnp.mean(vc * vc, axis=-1, keepdims=True)
    vn = (vc * lax.rsqrt(var + C_LN_EPS) * lnw_ref[...] + lnb_ref[...]).astype(BF16)
    bs = bs_ref[...]
    pieces = []
    for c in range(rows // chunk):
        vrow = vn[c * chunk:(c + 1) * chunk]
        sv = jnp.concatenate(
            [jnp.dot(ws_ref[g], vrow[:, g * gd:(g + 1) * gd], preferred_element_type=F32) for g in range(groups)],
            axis=1)
        pieces.append((u[c * chunk:(c + 1) * chunk] * (sv + bs)).astype(BF16))
    gated = jnp.concatenate(pieces, axis=0)
    o_ref[...] = h + jnp.dot(gated, wo_ref[...], preferred_element_type=F32)


def _gmlp(h2, nw, wi, lnw, lnb, ws, bs_full, wo, tm=512):
    m, d = h2.shape
    groups, chunk, _ = ws.shape
    cw = wo.shape[0]
    return pl.pallas_call(
        functools.partial(_gmlp_kernel, chunk=chunk, groups=groups),
        grid=(m // tm,),
        in_specs=[pl.BlockSpec((tm, d), lambda i: (i, 0)), _const_spec((1, d)), _const_spec(wi.shape),
                  _const_spec((1, cw)), _const_spec((1, cw)), _const_spec(ws.shape), _const_spec(bs_full.shape),
                  _const_spec(wo.shape)],
        out_specs=pl.BlockSpec((tm, d), lambda i: (i, 0)),
        out_shape=jax.ShapeDtypeStruct((m, d), F32),
        compiler_params=_params(("parallel",), tm * d * 4, wi.size * 2, ws.size * 2, bs_full.size * 4, wo.size * 2,
                                tm * d * 4, temps=5 * tm * 2 * cw * 4),
        name="gmlp",
    )(h2, nw, wi, lnw, lnb, ws, bs_full, wo)


def _delta_prep_kernel(x_ref, xp_ref, xn_ref, ab_ref, cw_ref, alog_ref, dtb_ref,
                       wq_ref, u_ref, aq_ref, bt_ref, gm_ref,
                       xe_ref, q_scr, k_scr, v_scr, g_scr, b_scr, *solver_refs, heads, taps, group):
    i = pl.program_id(1)
    rows = x_ref.shape[1]
    width = x_ref.shape[2]
    aw = width // 3
    hd = aw // heads
    sol = _UnitSolver(solver_refs)
    _fill_halo(xe_ref, x_ref, xp_ref, xn_ref, i, pl.num_programs(1))

    def elementwise_items(c):
        r0 = c * CHUNK
        rs = slice(r0, r0 + CHUNK)

        def conv(cols):
            acc = jnp.zeros((CHUNK, hd), F32)
            for j in range(taps):
                s = HALO - taps // 2 + j + r0
                acc = acc + cw_ref[j:j + 1, cols] * xe_ref[s:s + CHUNK, cols]
            return _silu(acc)

        def head(hh):
            qh = conv(slice(hh * hd, (hh + 1) * hd))
            kh = conv(slice(aw + hh * hd, aw + (hh + 1) * hd))
            q_scr[hh, rs, :] = qh * lax.rsqrt(jnp.sum(qh * qh, axis=-1, keepdims=True) + L2_EPS) * (hd ** -0.5)
            k_scr[hh, rs, :] = kh * lax.rsqrt(jnp.sum(kh * kh, axis=-1, keepdims=True) + L2_EPS)
            v_scr[hh, rs, :] = conv(slice(2 * aw + hh * hd, 2 * aw + (hh + 1) * hd))

        def gates():
            ab = ab_ref[0, rs, :]
            g_scr[rs, :] = -jnp.exp(alog_ref[...]) * _softplus(ab + dtb_ref[...])
            b_scr[rs, :] = _sigmoid(ab)

        return [functools.partial(head, hh) for hh in range(heads)] + [gates]

    incl, strict, blk, _, eye = _unit_masks(False, True)
    r1 = _iota2((UNIT, 1), 0)
    top_col = r1 < CHUNK
    top_row = _iota2((1, UNIT), 1) < CHUNK
    tri_r = _iota2((UNIT, CHUNK), 0)
    tri_c = _iota2((UNIT, CHUNK), 1)
    cum_sel = jnp.where(jnp.where(tri_r < CHUNK, tri_r - tri_c, tri_c - tri_r + CHUNK) >= 0, 1.0, 0.0).astype(BF16)

    def stage_items(c, u0):
        rows_c = slice(c * CHUNK, (c + 1) * CHUNK)
        st = {}

        def cumulate():
            st["gst"] = _dot_sel(cum_sel, g_scr[rows_c, :])
            st["gst_t"] = st["gst"].T
            bch = b_scr[rows_c, :]
            st["bst"] = jnp.concatenate([bch, bch], axis=0)

        def head(hh):
            gst, gst_t, bst = st["gst"], st["gst_t"], st["bst"]
            fw, bw = hh, heads + hh
            gcs = jnp.where(top_col, gst[:, fw:fw + 1], gst[:, bw:bw + 1])
            grow = jnp.where(top_row, gst_t[fw:fw + 1, :], gst_t[bw:bw + 1, :])
            bcol = jnp.where(top_col, bst[:, 2 * heads + fw:2 * heads + fw + 1],
                             bst[:, 2 * heads + bw:2 * heads + bw + 1])
            glast = jnp.where(top_col, gst[CHUNK - 1:CHUNK, fw:fw + 1], gst[CHUNK:CHUNK + 1, bw:bw + 1])
            q = q_scr[hh, rows_c, :]
            k = k_scr[hh, rows_c, :]
            v = v_scr[hh, rows_c, :]
            qst = jnp.concatenate([q, q], axis=0)
            kst = jnp.concatenate([k, k], axis=0)
            vst = jnp.concatenate([v, v], axis=0)
            kst_b = kst.astype(BF16)
            decay = jnp.where(incl, jnp.exp(jnp.where(incl, gcs - grow, 0.0)), 0.0)
            kk = _dot_nt(kst_b, kst_b)
            m = jnp.where(strict, bcol * kk * decay, 0.0)
            sol.stage(u0 + hh, -m, blk, eye)
            eg = jnp.exp(gcs)
            sol.set_rhs(u0 + hh, 0, vst * bcol)
            sol.set_rhs(u0 + hh, 1, kst * (bcol * eg))
            qk = _dot_nt(qst, kst_b) * decay
            qd = qst * eg
            kdec = kst * jnp.exp(glast - gcs)
            gam = jnp.exp(glast)
            for dr in range(2):
                sl = slice(dr * CHUNK, (dr + 1) * CHUNK)
                wq_ref.at[dr, 0, hh, c][CHUNK:, :] = qd[sl].astype(BF16)
                aq_ref[dr, 0, hh, c] = qk[sl].astype(BF16)
                bt_ref[dr, 0, hh, c] = kdec[sl].T.astype(BF16)
                gm_ref[dr, 0, hh, c] = jnp.broadcast_to(gam[dr * CHUNK:dr * CHUNK + 1], (UNIT, GAM_LANES))

        return [cumulate] + [functools.partial(head, hh) for hh in range(heads)]

    def finish_items(c, u0):
        def head(hh):
            un, w = sol.result(u0 + hh, 0), sol.result(u0 + hh, 1)
            for dr in range(2):
                sl = slice(dr * CHUNK, (dr + 1) * CHUNK)
                wq_ref.at[dr, 0, hh, c][:CHUNK, :] = (-w[sl]).astype(BF16)
                u_ref[dr, 0, hh, c] = un[sl]

        return [functools.partial(head, hh) for hh in range(heads)]

    nu = group * heads
    chunks = lambda gi: range(gi * group, (gi + 1) * group)

    def prepare(gi, base):
        items = []
        for cc, c in enumerate(chunks(gi)):
            items += elementwise_items(c) + stage_items(c, base + cc * heads)
        return items

    def finish(gi, base):
        return [it for cc, c in enumerate(chunks(gi)) for it in finish_items(c, base + cc * heads)]

    n_groups = rows // (CHUNK * group)
    _emit(prepare(0, 0))
    for gi in range(n_groups):
        base, other = (gi % 2) * nu, ((gi + 1) % 2) * nu
        side = finish(gi - 1, other) if gi > 0 else []
        if gi + 1 < n_groups:
            side = side + prepare(gi + 1, other)
        _emit(sol.items(range(base, base + nu)), side)
    _emit(finish(n_groups - 1, ((n_groups - 1) % 2) * nu))


def _delta_prep(qkv, ab, conv_w, alog_row, dtb_row, heads, rows=256):
    b, t, width = qkv.shape
    hd = width // 3 // heads
    assert hd == LANES
    n = t // CHUNK
    nb = rows // CHUNK
    taps = conv_w.shape[0]
    cw = jnp.zeros((SUBLANES, width), F32).at[:taps].set(conv_w)
    group = UNITS_IN_FLIGHT // heads
    assert nb % group == 0
    unit = lambda r, dt, w=LANES: jax.ShapeDtypeStruct((2, b, heads, n, r, w), dt)
    ospec = lambda r, w=LANES: pl.BlockSpec((2, 1, heads, nb, r, w), lambda bi, i: (0, bi, 0, i, 0, 0))
    return pl.pallas_call(
        functools.partial(_delta_prep_kernel, heads=heads, taps=taps, group=group),
        grid=(b, t // rows),
        in_specs=_halo_specs(rows, width, t) + [
            pl.BlockSpec((1, rows, LANES), lambda bi, i: (bi, i, 0)),
            _const_spec((SUBLANES, width)), _const_spec((1, LANES)), _const_spec((1, LANES))],
        out_specs=[ospec(UNIT), ospec(CHUNK), ospec(CHUNK), ospec(UNIT, CHUNK), ospec(UNIT, GAM_LANES)],
        out_shape=[unit(UNIT, BF16), unit(CHUNK, F32), unit(CHUNK, BF16), unit(UNIT, BF16, CHUNK),
                   unit(UNIT, F32, GAM_LANES)],
        scratch_shapes=[pltpu.VMEM((rows + 2 * HALO, width), F32)]
        + [pltpu.VMEM((heads, rows, LANES), F32)] * 3 + [pltpu.VMEM((rows, LANES), F32)] * 2
        + _UnitSolver.scratch(2 * group * heads),
        compiler_params=_params(("parallel", "arbitrary"), rows * width * 4, rows * LANES * 4,
                                2 * heads * nb * (UNIT * 2 + CHUNK * 4 + CHUNK * 2 + UNIT * 2 + UNIT * 4) * LANES,
                                scratch=(2 * rows + 2 * HALO) * width * 4
                                + 2 * group * heads * _UnitSolver.BYTES_PER_UNIT, temps=4 * rows * width * 4),
        name="delta_prep",
    )(qkv, qkv, qkv, ab, cw, alog_row, dtb_row)


def _rwkv_prep_kernel(x_ref, xp_ref, xn_ref, l_ref, lp_ref, ln_ref, mux_ref, mul_ref, wcat_ref, bias_ref, g2_ref,
                      kk_w_ref, ka_ref, rk_ref,
                      wq_ref, u0_ref, aq_ref, bk_ref, vb_ref, gm_ref, bonus_ref, gate_ref,
                      xe_ref, le_ref, r_scr, v_scr, kk_scr, lw_scr, kka_scr, kd_scr, kr_scr, vs_scr,
                      *solver_refs, lora_w, lora_a):
    i = pl.program_id(1)
    nblk = pl.num_programs(1)
    sol = _UnitSolver(solver_refs)
    rows = x_ref.shape[1]
    bw = x_ref.shape[2] // 3
    pairs = bw // LANES
    _fill_halo(xe_ref, x_ref, xp_ref, xn_ref, i, nblk)
    _fill_halo(le_ref, l_ref, lp_ref, ln_ref, i, nblk)

    sr = _iota2((LANES, LANES), 0) // CHUNK
    sc = _iota2((LANES, LANES), 1) // CHUNK
    head_sel = jnp.where(sr == sc, 1.0, 0.0).astype(BF16)
    lane = _iota2((1, LANES), 1)

    def head_sum(a, pieces):
        return jnp.concatenate(
            [_dot_sel_r(a[:, p * LANES:(p + 1) * LANES], head_sel, pieces) for p in range(pairs)], axis=1)

    def elementwise_items(c):
        r0 = c * CHUNK
        rs = slice(r0, r0 + CHUNK)
        st = {}

        def shifted(e_ref, mu, cols):
            x = e_ref[HALO + r0:HALO + r0 + CHUNK, cols]
            nbr = 0.5 * (e_ref[HALO - 1 + r0:HALO - 1 + r0 + CHUNK, cols]
                         + e_ref[HALO + 1 + r0:HALO + 1 + r0 + CHUNK, cols])
            return x + (nbr - x) * mu[:, cols]

        def shift_rv():
            st["r"] = shifted(xe_ref, mux_ref, slice(0, bw))
            st["v"] = shifted(xe_ref, mux_ref, slice(2 * bw, 3 * bw))
            r_scr[rs, :] = st["r"]
            v_scr[rs, :] = st["v"]

        def shift_k():
            st["k"] = shifted(xe_ref, mux_ref, slice(bw, 2 * bw))
            kkp = st["k"] * kk_w_ref[...]
            st["kk"] = kkp * lax.rsqrt(head_sum(kkp * kkp, 1) + L2_EPS)
            kk_scr[rs, :] = st["kk"]
            st["kb"] = jnp.zeros_like(st["k"])

        def lora():
            lo = shifted(le_ref, mul_ref, slice(None))
            l01 = lo[:, :LANES]
            xin = jnp.where(lane < 2 * lora_w, jnp.tanh(l01), l01)
            st["wa"] = _dot(xin, wcat_ref[...]) + bias_ref[...]
            gate_ref[0, rs, :] = _dot(_sigmoid(lo[:, LANES:]), g2_ref[...])

        def direction(e):
            wa = st["wa"]
            a = _sigmoid(wa[:, (2 + e) * bw:(3 + e) * bw])
            lw_scr[e, rs, :] = -B_DECAY_SCALE * _sigmoid(wa[:, e * bw:(e + 1) * bw])
            kka_scr[e, rs, :] = st["kk"] * a
            kd = st["k"] * (1.0 + (a - 1.0) * ka_ref[...])
            kd_scr[e, rs, :] = kd
            st["kb"] = st["kb"] + 0.5 * kd

        def bonus():
            bonus_ref[0, rs, :] = head_sum(st["r"] * st["kb"] * rk_ref[...], 2) * st["v"]

        return [shift_rv, shift_k, lora, functools.partial(direction, 0), functools.partial(direction, 1), bonus]

    lanes2 = _iota2((1, LANES), 1)
    m0 = lanes2 < CHUNK
    tr = _iota2((CHUNK, CHUNK), 0)
    tc = _iota2((CHUNK, CHUNK), 1)
    cum_sel = (jnp.where(tc <= tr, 1.0, 0.0).astype(BF16), jnp.where(tc >= tr, 1.0, 0.0).astype(BF16))
    masks = (_unit_masks(False, False), _unit_masks(True, True))

    def stack(a):
        return jnp.concatenate([jnp.where(m0, a, 0.0), jnp.where(m0, 0.0, a)], axis=0)

    def halves(a):
        return a[:CHUNK] + a[CHUNK:]

    nu = 2 * pairs

    def stage_items(c, base):
        rows_c = slice(c * CHUNK, (c + 1) * CHUNK)
        cum = {}

        def cumulate(e):
            cum[e] = _dot_sel(cum_sel[e], lw_scr[e, rows_c, :])

        def unit_steps(e, p):
            incl, strict, blk, same, eye = masks[e]
            last = CHUNK - 1 if e == 0 else 0
            ls = slice(p * LANES, (p + 1) * LANES)
            u = base + e * pairs + p
            st = {}

            def decayed_rows():
                cs = cum[e][:, ls]
                e_neg = jnp.exp(-cs)
                st["xs"] = stack(-kk_scr[rows_c, ls] * jnp.exp(cs - lw_scr[e, rows_c, ls])).astype(BF16)
                st["bk"] = jnp.concatenate([stack(kka_scr[e, rows_c, ls] * e_neg), stack(kd_scr[e, rows_c, ls] * e_neg)],
                                           axis=0).astype(BF16)
                sol.set_rhs(u, 0, st["xs"])

            def pair_a():
                ga = _dot_nt(st["xs"], st["bk"])
                sol.stage(u, jnp.where(strict, ga[:, :UNIT], 0.0), blk, eye)
                kr_scr[u] = jnp.where(strict, ga[:, UNIT:], 0.0).astype(BF16)

            def pair_r():
                rt = r_scr[rows_c, ls] * jnp.exp(cum[e][:, ls])
                gr = _dot_nt(stack(rt), st["bk"])
                aq_ref[e, 0, p, c] = jnp.concatenate(
                    [halves(jnp.where(incl, gr[:, :UNIT], 0.0)), halves(jnp.where(incl, gr[:, UNIT:], 0.0))],
                    axis=1).astype(BF16)
                wq_ref.at[e, 0, p, c][CHUNK:, :] = rt.astype(BF16)

            def seq_operands():
                cs = cum[e][:, ls]
                cl = cum[e][last:last + 1, ls]
                e_last = jnp.exp(cl - cs)
                vv = v_scr[rows_c, ls]
                vs_scr[u] = stack(vv).astype(BF16)
                vb_ref[e, 0, p, c] = vv.astype(BF16)
                bk_ref[e, 0, p, c] = jnp.concatenate(
                    [kka_scr[e, rows_c, ls] * e_last, kd_scr[e, rows_c, ls] * e_last], axis=0).T.astype(BF16)
                gm_ref[e, 0, p, c] = jnp.broadcast_to(jnp.exp(cl), (UNIT, LANES)).T[:, :GAM_LANES]

            return [decayed_rows, pair_a, pair_r, seq_operands]

        items = []
        for e in range(2):
            items.append(functools.partial(cumulate, e))
            for p in range(pairs):
                items += unit_steps(e, p)
        return items

    def solve_items(base):
        def av(u):
            sol.set_rhs(u, 1, jnp.dot(kr_scr[u], vs_scr[u], preferred_element_type=F32))

        return ([functools.partial(av, u) for u in range(base, base + nu)]
                + sol.items(range(base, base + nu)))

    def finish_items(c, base):
        def unit(e, p):
            u = base + e * pairs + p
            wq_ref.at[e, 0, p, c][:CHUNK, :] = halves(sol.result(u, 0)).astype(BF16)
            u0_ref[e, 0, p, c] = halves(sol.result(u, 1))

        return [functools.partial(unit, e, p) for e in range(2) for p in range(pairs)]

    n_chunks = rows // CHUNK
    _emit(elementwise_items(0) + stage_items(0, 0))
    for c in range(n_chunks):
        base, other = (c % 2) * nu, ((c + 1) % 2) * nu
        side = finish_items(c - 1, other) if c > 0 else []
        if c + 1 < n_chunks:
            side = side + elementwise_items(c + 1) + stage_items(c + 1, other)
        _emit(solve_items(base), side)
    _emit(finish_items(n_chunks - 1, ((n_chunks - 1) % 2) * nu))


def _rwkv_prep(rkv, lora, mu_x, mu_l, wcat, bias, g2p, k_k, k_a, r_k, lora_w, lora_a, rows=256):
    b, t, width = rkv.shape
    bw = width // 3
    pairs = bw // LANES
    lw_ = lora.shape[2]
    n = t // CHUNK
    nb = rows // CHUNK
    unit = lambda r, dt, w=LANES: jax.ShapeDtypeStruct((2, b, pairs, n, r, w), dt)
    ospec = lambda r, w=LANES: pl.BlockSpec((2, 1, pairs, nb, r, w), lambda bi, i: (0, bi, 0, i, 0, 0))
    row_spec = pl.BlockSpec((1, rows, bw), lambda bi, i: (bi, i, 0))
    return pl.pallas_call(
        functools.partial(_rwkv_prep_kernel, lora_w=lora_w, lora_a=lora_a),
        grid=(b, t // rows),
        in_specs=_halo_specs(rows, width, t) + _halo_specs(rows, lw_, t) + [
            _const_spec((1, width)), _const_spec((1, lw_)), _const_spec(wcat.shape), _const_spec(bias.shape),
            _const_spec(g2p.shape), _const_spec((1, bw)), _const_spec((1, bw)), _const_spec((1, bw))],
        out_specs=[ospec(UNIT), ospec(CHUNK), ospec(CHUNK, 2 * LANES), ospec(UNIT), ospec(CHUNK),
                   ospec(UNIT, GAM_LANES), row_spec, row_spec],
        out_shape=[unit(UNIT, BF16), unit(CHUNK, F32), unit(CHUNK, BF16, 2 * LANES), unit(UNIT, BF16),
                   unit(CHUNK, BF16), unit(UNIT, F32, GAM_LANES),
                   jax.ShapeDtypeStruct((b, t, bw), F32), jax.ShapeDtypeStruct((b, t, bw), F32)],
        scratch_shapes=[pltpu.VMEM((rows + 2 * HALO, width), F32), pltpu.VMEM((rows + 2 * HALO, lw_), F32)]
        + [pltpu.VMEM((rows, bw), F32)] * 3 + [pltpu.VMEM((2, rows, bw), F32)] * 3
        + [pltpu.VMEM((4 * pairs, UNIT, UNIT), BF16), pltpu.VMEM((4 * pairs, UNIT, LANES), BF16)]
        + _UnitSolver.scratch(4 * pairs),
        compiler_params=_params(("parallel", "arbitrary"), rows * width * 4, rows * lw_ * 4, wcat.size * 2,
                                2 * pairs * nb * (UNIT * 2 + CHUNK * 4 + CHUNK * 4 + UNIT * 2 + CHUNK * 2 + UNIT * 4)
                                * LANES, 2 * rows * bw * 4,
                                scratch=(rows + 2 * HALO) * (width + lw_) * 4 + 9 * rows * bw * 4
                                + 4 * pairs * (_UnitSolver.BYTES_PER_UNIT + 2 * UNIT * LANES * 2),
                                temps=8 * rows * bw * 4 * 4),
        name="rwkv_prep",
    )(rkv, rkv, rkv, lora, lora, lora, mu_x, mu_l, wcat, bias, g2p, k_k, k_a, r_k)


def _seq_kernel(*refs, paired, ub):
    if paired:
        wq_ref, u_ref, aq_ref, bt_ref, gm_ref, vb_ref, y_ref, s_ref, pq_ref = refs
    else:
        wq_ref, u_ref, aq_ref, bt_ref, gm_ref, y_ref, s_ref, pq_ref = refs

    @pl.when(pl.program_id(1) == 0)
    def _():
        s_ref[...] = jnp.zeros_like(s_ref)

    m0 = _iota2((1, LANES), 1) < CHUNK
    same = (_iota2((LANES, LANES), 0) // CHUNK) == (_iota2((LANES, LANES), 1) // CHUNK)
    for j in range(ub):
        pq_ref[j] = jnp.dot(wq_ref[j, 0], s_ref[j].astype(BF16), preferred_element_type=F32)
    for j in range(ub):
        s = s_ref[j]
        pq = pq_ref[j]
        un = u_ref[j, 0] + pq[:CHUNK]
        unb = un.astype(BF16)
        if paired:
            vb = vb_ref[j, 0]
            zero = jnp.zeros_like(unb)
            rhs_y = jnp.concatenate([jnp.where(m0, unb, zero), jnp.where(m0, zero, unb),
                                     jnp.where(m0, vb, zero), jnp.where(m0, zero, vb)], axis=0)
            rhs_s = jnp.concatenate([unb, vb], axis=0)
        else:
            rhs_y = jnp.concatenate([unb, unb], axis=0)
            rhs_s = unb
        y_ref[j, 0] = pq[CHUNK:] + jnp.dot(aq_ref[j, 0], rhs_y, preferred_element_type=F32)
        upd = jnp.dot(bt_ref[j, 0], rhs_s, preferred_element_type=F32)
        if paired:
            upd = jnp.where(same, upd, 0.0)
        s_ref[j] = s * jnp.broadcast_to(gm_ref[j, 0][:, 0:1], (LANES, LANES)) + upd


def _seq(ops, per_dir, paired, ub=16):
    ub = math.gcd(ub, per_dir)
    g, n = ops[0].shape[0], ops[0].shape[1]

    def imap(gi, ni):
        back = (gi * ub) // per_dir
        return (gi, ni + back * (n - 1 - 2 * ni), 0, 0)

    specs = [pl.BlockSpec((ub, 1) + a.shape[2:], imap) for a in ops]
    block_bytes = [ub * math.prod(a.shape[2:]) * a.dtype.itemsize for a in ops]
    return pl.pallas_call(
        functools.partial(_seq_kernel, paired=paired, ub=ub),
        grid=(g // ub, n),
        in_specs=specs,
        out_specs=pl.BlockSpec((ub, 1, CHUNK, LANES), imap),
        out_shape=jax.ShapeDtypeStruct((g, n, CHUNK, LANES), F32),
        scratch_shapes=[pltpu.VMEM((ub, LANES, LANES), F32), pltpu.VMEM((ub, UNIT, LANES), F32)],
        compiler_params=_params(("parallel", "arbitrary"), *block_bytes, ub * UNIT * LANES * 4,
                                ub * CHUNK * LANES * 4, scratch=ub * (LANES + UNIT) * LANES * 4),
        name="seq_rwkv" if paired else "seq_delta",
    )(*ops)


def _even_out_kernel(of_ref, ob_ref, yf_ref, yb_ref, z_ref, bonus_ref, gate_ref, h_ref, an_ref, lnw_ref, lnb_ref,
                     wo_ref, o_ref):
    rows = h_ref.shape[1]
    a_heads = of_ref.shape[2]
    pairs = yf_ref.shape[2]
    z = z_ref[0]
    pieces = []
    for hh in range(a_heads):
        o = (of_ref[0, 0, hh] + ob_ref[0, 0, hh]).reshape(rows, LANES)
        pieces.append(_rms(o, an_ref[...]) * _silu(z[:, hh * LANES:(hh + 1) * LANES]))
    sr = _iota2((LANES, LANES), 0) // CHUNK
    sc = _iota2((LANES, LANES), 1) // CHUNK
    head_sel = jnp.where(sr == sc, 1.0, 0.0).astype(BF16)
    inv_n = 1.0 / CHUNK
    for p in range(pairs):
        ls = slice(p * LANES, (p + 1) * LANES)
        y = (yf_ref[0, 0, p] + yb_ref[0, 0, p]).reshape(rows, LANES)
        mu = _dot_sel_r(y, head_sel, 2) * inv_n
        yc = y - mu
        var = _dot_sel_r(yc * yc, head_sel, 2) * inv_n
        yn = yc * lax.rsqrt(var + B_GN_EPS) * lnw_ref[:, ls] + lnb_ref[:, ls]
        pieces.append((yn + bonus_ref[0][:, ls]) * gate_ref[0][:, ls])
    mix = jnp.concatenate(pieces, axis=1).astype(BF16)
    o_ref[0] = h_ref[0] + jnp.dot(mix, wo_ref[...], preferred_element_type=F32)


def _even_out(od, yr, z, bonus, gate, h3, a_norm, ln_w, ln_b, wo, rows=256):
    b, t, d = h3.shape
    nb = rows // CHUNK
    a_heads, pairs = od.shape[2], yr.shape[2]
    aw, bw = z.shape[2], bonus.shape[2]
    dspec = lambda dr, nh: pl.BlockSpec((1, 1, nh, nb, CHUNK, LANES), lambda bi, i: (dr, bi, 0, i, 0, 0))
    rspec = lambda w: pl.BlockSpec((1, rows, w), lambda bi, i: (bi, i, 0))
    return pl.pallas_call(
        _even_out_kernel,
        grid=(b, t // rows),
        in_specs=[dspec(0, a_heads), dspec(1, a_heads), dspec(0, pairs), dspec(1, pairs), rspec(aw), rspec(bw),
                  rspec(bw), rspec(d), _const_spec((1, LANES)), _const_spec((1, bw)), _const_spec((1, bw)),
                  _const_spec(wo.shape)],
        out_specs=rspec(d),
        out_shape=jax.ShapeDtypeStruct((b, t, d), F32),
        compiler_params=_params(("parallel", "parallel"), 2 * rows * (aw + bw) * 4, rows * (aw + 2 * bw + 2 * d) * 4,
                                wo.size * 2, temps=6 * rows * d * 4),
        name="even_out",
    )(od, od, yr, yr, z, bonus, gate, h3, a_norm, ln_w, ln_b, wo)


def _even_layer(h3, nw, w_in, w_out, a_conv, a_log, a_dt_bias, a_norm, b_mu, b_w0, b_w2, b_a0, b_a2, b_g2,
                b_k_k, b_k_a, b_r_k, b_ln_w, b_ln_b):
    b, t, d = h3.shape
    a_heads = a_log.shape[1]
    hd = a_norm.shape[0]
    aw = a_heads * hd
    b_heads, b_hd = b_r_k.shape
    bw = b_heads * b_hd
    lora_w, lora_a, lora_g = b_w2.shape[1], b_a2.shape[1], b_g2.shape[0]
    assert b_hd == CHUNK and hd == LANES and 2 * (lora_w + lora_a) == LANES and lora_g <= LANES
    a_cols = 4 * aw + 4 * a_heads
    n_gate = 4 * a_heads
    pad_cols = lambda w, n: jnp.pad(w, ((0, 0), (0, n - w.shape[1])))
    wa, wb = w_in[:, :a_cols], w_in[:, a_cols:]
    w_all = jnp.concatenate([wa[:, :4 * aw], pad_cols(wa[:, 4 * aw:], LANES),
                             wb[:, :3 * bw], pad_cols(wb[:, 3 * bw:], 2 * LANES)], axis=1).astype(BF16)
    splits = (3 * aw, aw, LANES, 3 * bw, 2 * LANES)
    qkv, z, ab, rkv, lora = _proj(h3.reshape(b * t, d), nw.reshape(1, d), w_all, splits)
    to3 = lambda a: a.reshape(b, t, a.shape[-1])

    alog_row = jnp.zeros((1, LANES), F32).at[0, :2 * a_heads].set(a_log.reshape(-1))
    dtb_row = jnp.zeros((1, LANES), F32).at[0, :2 * a_heads].set(a_dt_bias.reshape(-1))
    d_ops = _delta_prep(to3(qkv), to3(ab), a_conv, alog_row, dtb_row, a_heads)
    n = t // CHUNK
    flat = lambda a: a.reshape((-1, n) + a.shape[4:])
    od = _seq([flat(a) for a in d_ops], b * a_heads, paired=False)
    od = od.reshape(2, b, a_heads, n, CHUNK, LANES)

    wcat = jnp.zeros((LANES, 4 * bw), F32)
    for e in range(2):
        wcat = wcat.at[e * lora_w:(e + 1) * lora_w, e * bw:(e + 1) * bw].set(b_w2[e])
        wcat = wcat.at[2 * lora_w + e * lora_a:2 * lora_w + (e + 1) * lora_a, (2 + e) * bw:(3 + e) * bw].set(b_a2[e])
    bias = jnp.concatenate([b_w0[0], b_w0[1], b_a0[0], b_a0[1]]).reshape(1, 4 * bw)
    g2p = jnp.zeros((LANES, bw), F32).at[:lora_g].set(b_g2).astype(BF16)
    mu_x = b_mu[:3 * bw].reshape(1, 3 * bw)
    mu_l = jnp.pad(b_mu[3 * bw:], (0, 2 * LANES - (b_mu.shape[0] - 3 * bw))).reshape(1, 2 * LANES)
    row = lambda a: a.reshape(1, bw)
    r_ops = _rwkv_prep(to3(rkv), to3(lora), mu_x, mu_l, wcat.astype(BF16), bias, g2p, row(b_k_k), row(b_k_a),
                       row(b_r_k), lora_w, lora_a)
    wq, u0, aq, bk, vb, gm, bonus, gate = r_ops
    pairs = bw // LANES
    yr = _seq([flat(a) for a in (wq, u0, aq, bk, gm, vb)], b * pairs, paired=True)
    yr = yr.reshape(2, b, pairs, n, CHUNK, LANES)

    return _even_out(od, yr, to3(z), bonus, gate, h3, a_norm.reshape(1, hd), row(b_ln_w), row(b_ln_b),
                     w_out.astype(BF16))


def _odd_layer(h3, nw, w_in, ln_w, ln_b, ws, bs, w_out):
    b, t, d = h3.shape
    groups, chunk, _ = ws.shape
    cw = w_out.shape[0]
    bs_full = jnp.repeat(bs.T, cw // groups, axis=1)
    out = _gmlp(h3.reshape(b * t, d), nw.reshape(1, d), w_in.astype(BF16), ln_w.reshape(1, cw), ln_b.reshape(1, cw),
                ws.astype(BF16), bs_full, w_out.astype(BF16))
    return out.reshape(b, t, d)


def kernel(x, p, norm_mix, norm_ffn, norm_ple, norm_final, w_in_even, w_out_even, a_conv, a_log, a_dt_bias, a_norm,
           b_mu, b_w0, b_w2, b_a0, b_a2, b_g2, b_k_k, b_k_a, b_r_k, b_ln_w, b_ln_b, w_in_odd, c_ln_w, c_ln_b, c_ws,
           c_bs, w_out_odd, w_gate, w_up, w_down, w_ple, w_ple_gate):
    b, t, d = x.shape
    depth = p.shape[0]
    h = x
    for i in range(depth):
        j = i // 2
        if i % 2 == 0:
            h = _even_layer(h, norm_mix[i], w_in_even[j], w_out_even[j], a_conv[j], a_log[j], a_dt_bias[j], a_norm[j],
                            b_mu[j], b_w0[j], b_w2[j], b_a0[j], b_a2[j], b_g2[j], b_k_k[j], b_k_a[j], b_r_k[j],
                            b_ln_w[j], b_ln_b[j])
        else:
            h = _odd_layer(h, norm_mix[i], w_in_odd[j], c_ln_w[j], c_ln_b[j], c_ws[j], c_bs[j], w_out_odd[j])
        h2 = h.reshape(b * t, d)
        h2 = _ffn_ple(h2, p[i].reshape(b * t, -1), norm_ffn[i].reshape(1, d), w_gate[i].astype(BF16),
                      w_up[i].astype(BF16), w_down[i].astype(BF16), norm_ple[i].reshape(1, d),
                      w_ple[i].astype(BF16), w_ple_gate[i].astype(BF16),
                      norm_final.reshape(1, d) if i == depth - 1 else None)
        h = h2.reshape(b, t, d)
    return h
```

```python
import functools
import math

import jax
import jax.numpy as jnp
from jax import lax
from jax.experimental import pallas as pl
from jax.experimental.pallas import tpu as pltpu

F32 = jnp.float32
BF16 = jnp.bfloat16

NORM_EPS = 1e-6
L2_EPS = 1e-6
B_GN_EPS = 64e-5
B_DECAY_SCALE = 0.606531
C_LN_EPS = 1e-5

LANES = 128
SUBLANES = 8
CHUNK = 64
UNIT = 2 * CHUNK
INV_BLOCK = 16
GAM_LANES = 8
UNITS_IN_FLIGHT = 8
HALO = SUBLANES
VMEM_BUDGET = 56 * 1024 * 1024


def _params(sem, *block_bytes, scratch=0, temps=0, flags=None):
    need = 2 * sum(block_bytes) + scratch + temps
    return pltpu.CompilerParams(dimension_semantics=sem, vmem_limit_bytes=int(min(max(need, 16 << 20), VMEM_BUDGET)),
                                flags=flags)


def _dot(a, b):
    return jnp.dot(a.astype(BF16), b.astype(BF16), preferred_element_type=F32)


def _dot_nt(a, b):
    return lax.dot_general(a.astype(BF16), b.astype(BF16), (((1,), (1,)), ((), ())), preferred_element_type=F32)


def _split3(x):
    hi = x.astype(BF16)
    r1 = x - hi.astype(F32)
    mid = r1.astype(BF16)
    lo = (r1 - mid.astype(F32)).astype(BF16)
    return hi, mid, lo


def _dot_sel(sel, x):
    hi, mid, lo = _split3(x)
    d = lambda p: jnp.dot(sel, p, preferred_element_type=F32)
    return d(hi) + d(mid) + d(lo)


def _dot_sel_r(x, sel, pieces=3):
    d = lambda p: jnp.dot(p, sel, preferred_element_type=F32)
    return sum(d(p) for p in _split3(x)[:pieces])


def _sigmoid(x):
    return 1.0 / (1.0 + jnp.exp(-x))


def _silu(x):
    return x * _sigmoid(x)


def _softplus(x):
    return jnp.maximum(x, 0.0) + jnp.log(1.0 + jnp.exp(-jnp.abs(x)))


def _gelu_tanh(x):
    return 0.5 * x * (1.0 + jnp.tanh(math.sqrt(2.0 / math.pi) * (x + 0.044715 * (x * x * x))))


def _rms(x, w, eps=NORM_EPS):
    return x * lax.rsqrt(jnp.mean(x * x, axis=-1, keepdims=True) + eps) * w


def _iota2(shape, dim):
    return lax.broadcasted_iota(jnp.int32, shape, dim)


def _unit_masks(upper_top, upper_bot):
    r = _iota2((UNIT, UNIT), 0)
    c = _iota2((UNIT, UNIT), 1)
    same = (r // CHUNK) == (c // CHUNK)
    sign = jnp.where(r < CHUNK, -1 if upper_top else 1, -1 if upper_bot else 1)
    d = jnp.where(same, (r - c) * sign, -1)
    incl = d >= 0
    strict = d > 0
    blk = (r // INV_BLOCK) == (c // INV_BLOCK)
    eye = jnp.where(r == c, 1.0, 0.0).astype(F32)
    return incl, strict, blk, same, eye


class _UnitSolver:
    def __init__(self, refs):
        self.p, self.t, self.z, self.x, self.y = refs

    @staticmethod
    def scratch(nu):
        return [pltpu.VMEM((nu, UNIT, UNIT), BF16), pltpu.VMEM((nu, UNIT, UNIT), F32),
                pltpu.VMEM((nu, UNIT, 3 * UNIT), BF16), pltpu.VMEM((2, nu, UNIT, UNIT), BF16),
                pltpu.VMEM((nu, UNIT, 2 * UNIT), F32)]

    BYTES_PER_UNIT = UNIT * UNIT * (2 + 4 + 6 + 4 + 8)

    def stage(self, u, a, blk, eye):
        ad = jnp.where(blk, a, 0.0)
        self.p[u] = ad.astype(BF16)
        self.t[u] = eye + ad
        self.z[u, :, :UNIT] = jnp.where(blk, 0.0, a).astype(BF16)

    def set_rhs(self, u, tile, val):
        self.z[u, :, (tile + 1) * UNIT:(tile + 2) * UNIT] = val.astype(BF16)

    def result(self, u, tile):
        return self.y[u, :, tile * UNIT:(tile + 1) * UNIT]

    def items(self, units):
        assert CHUNK // INV_BLOCK == 4
        dot = lambda a, b: jnp.dot(a, b, preferred_element_type=F32)

        def square(u):
            p = self.p[u]
            self.p[u] = dot(p, p).astype(BF16)

        def grow(u):
            t = self.t[u]
            self.t[u] = t + dot(t.astype(BF16), self.p[u])

        def apply_t(u):
            t = self.t[u].astype(BF16)
            self.x[0, u] = dot(t, self.z[u, :, :UNIT]).astype(BF16)
            self.y[u] = dot(t, self.z[u, :, UNIT:])

        def apply_x(u):
            x = self.x[0, u]
            self.x[1, u] = dot(x, x).astype(BF16)
            y = self.y[u]
            self.y[u] = y + dot(x, y.astype(BF16))

        def apply_x2(u):
            y = self.y[u]
            self.y[u] = y + dot(self.x[1, u], y.astype(BF16))

        stages = [square, grow] * (int(math.log2(INV_BLOCK)) - 1) + [apply_t, apply_x, apply_x2]
        return [functools.partial(stage, u) for stage in stages for u in units]


def _emit(main, side=()):
    n, m = len(main), len(side)
    j = 0
    for idx, step in enumerate(main):
        step()
        while j < m and (j + 1) * n <= (idx + 1) * m:
            side[j]()
            j += 1
    for step in side[j:]:
        step()


def _fill_halo(xe_ref, x_ref, prev_ref, next_ref, i, nblk):
    rows = x_ref.shape[1]
    pf = jnp.where(i > 0, 1.0, 0.0).astype(F32)
    nf = jnp.where(i < nblk - 1, 1.0, 0.0).astype(F32)
    xe_ref[0:HALO, :] = prev_ref[0] * pf
    xe_ref[HALO:HALO + rows, :] = x_ref[0]
    xe_ref[HALO + rows:2 * HALO + rows, :] = next_ref[0] * nf


def _halo_specs(rows, width, t_len):
    per = rows // HALO
    last = t_len // HALO - 1
    return [
        pl.BlockSpec((1, rows, width), lambda b, i: (b, i, 0)),
        pl.BlockSpec((1, HALO, width), lambda b, i: (b, jnp.maximum(i * per - 1, 0), 0)),
        pl.BlockSpec((1, HALO, width), lambda b, i: (b, jnp.minimum((i + 1) * per, last), 0)),
    ]


def _const_spec(shape, single_buffer=False):
    nd = len(shape)
    if single_buffer:
        return pl.BlockSpec(shape, lambda *_: (0,) * nd, pipeline_mode=pl.Buffered(1))
    return pl.BlockSpec(shape, lambda *_: (0,) * nd)


def _proj_kernel(h_ref, nw_ref, w_ref, *o_refs, splits):
    hn = _rms(h_ref[...], nw_ref[...]).astype(BF16)
    off = 0
    for o_ref, n in zip(o_refs, splits):
        o_ref[...] = jnp.dot(hn, w_ref[:, off:off + n], preferred_element_type=F32)
        off += n


def _proj(h2, nw, w, splits, tm=512):
    m, d = h2.shape
    n = w.shape[1]
    return pl.pallas_call(
        functools.partial(_proj_kernel, splits=splits),
        grid=(m // tm,),
        in_specs=[pl.BlockSpec((tm, d), lambda i: (i, 0)), _const_spec((1, d)), _const_spec((d, n))],
        out_specs=[pl.BlockSpec((tm, s), lambda i: (i, 0)) for s in splits],
        out_shape=[jax.ShapeDtypeStruct((m, s), F32) for s in splits],
        compiler_params=_params(("parallel",), tm * d * 4, d * n * 2, tm * n * 4, temps=tm * d * 8),
        name="proj",
    )(h2, nw, w)


def _ffn_ple_kernel(h_ref, p_ref, nw_ref, wg_ref, wu_ref, wd_ref, npl_ref, wp_ref, wpg_ref, *rest, final):
    if final:
        nf_ref, o_ref = rest
    else:
        (o_ref,) = rest
    h = h_ref[...]
    hn = _rms(h, nw_ref[...]).astype(BF16)
    g = jnp.dot(hn, wg_ref[...], preferred_element_type=F32)
    u = jnp.dot(hn, wu_ref[...], preferred_element_type=F32)
    h = h + jnp.dot((_silu(g) * u).astype(BF16), wd_ref[...], preferred_element_type=F32)
    gate = _sigmoid(jnp.dot(_rms(h, npl_ref[...]).astype(BF16), wpg_ref[...], preferred_element_type=F32))
    e = jnp.dot(p_ref[...].astype(BF16), wp_ref[...], preferred_element_type=F32)
    out = h + e * gate
    if final:
        out = _rms(out, nf_ref[...])
    o_ref[...] = out


def _ffn_ple(h2, p2, nw, wg, wu, wd, npl, wp, wpg, nf=None, tm=512):
    m, d = h2.shape
    ff = wg.shape[1]
    pd = p2.shape[1]
    final = nf is not None
    ins = [h2, p2, nw, wg, wu, wd, npl, wp, wpg] + ([nf] if final else [])
    specs = [
        pl.BlockSpec((tm, d), lambda i: (i, 0)),
        pl.BlockSpec((tm, pd), lambda i: (i, 0)),
        _const_spec((1, d)),
        _const_spec((d, ff), True), _const_spec((d, ff), True), _const_spec((ff, d), True),
        _const_spec((1, d)), _const_spec((pd, d), True), _const_spec((d, d), True),
    ] + ([_const_spec((1, d))] if final else [])
    weights = (3 * d * ff + pd * d + d * d) * 2
    return pl.pallas_call(
        functools.partial(_ffn_ple_kernel, final=final),
        grid=(m // tm,),
        in_specs=specs,
        out_specs=pl.BlockSpec((tm, d), lambda i: (i, 0)),
        out_shape=jax.ShapeDtypeStruct((m, d), F32),
        compiler_params=_params(("parallel",), tm * d * 4, tm * pd * 4, tm * d * 4,
                                scratch=weights, temps=tm * ff * (4 + 4 + 2) + 3 * tm * d * 4),
        name="ffn_ple",
    )(*ins)


def _gmlp_kernel(h_ref, nw_ref, wi_ref, lnw_ref, lnb_ref, ws_ref, bs_ref, wo_ref, o_ref, *, chunk, groups):
    h = h_ref[...]
    rows, d = h.shape
    cw = wo_ref.shape[0]
    gd = cw // groups
    hn = _rms(h, nw_ref[...]).astype(BF16)
    uv = _gelu_tanh(jnp.dot(hn, wi_ref[...], preferred_element_type=F32))
    u = uv[:, :cw]
    v = uv[:, cw:]
    mu = jnp.mean(v, axis=-1, keepdims=True)
    vc = v - mu
    var = jnp.mean(vc * vc, axis=-1, keepdims=True)
    vn = (vc * lax.rsqrt(var + C_LN_EPS) * lnw_ref[...] + lnb_ref[...]).astype(BF16)
    bs = bs_ref[...]
    pieces = []
    for c in range(rows // chunk):
        vrow = vn[c * chunk:(c + 1) * chunk]
        sv = jnp.concatenate(
            [jnp.dot(ws_ref[g], vrow[:, g * gd:(g + 1) * gd], preferred_element_type=F32) for g in range(groups)],
            axis=1)
        pieces.append((u[c * chunk:(c + 1) * chunk] * (sv + bs)).astype(BF16))
    gated = jnp.concatenate(pieces, axis=0)
    o_ref[...] = h + jnp.dot(gated, wo_ref[...], preferred_element_type=F32)


def _gmlp(h2, nw, wi, lnw, lnb, ws, bs_full, wo, tm=512):
    m, d = h2.shape
    groups, chunk, _ = ws.shape
    cw = wo.shape[0]
    return pl.pallas_call(
        functools.partial(_gmlp_kernel, chunk=chunk, groups=groups),
        grid=(m // tm,),
        in_specs=[pl.BlockSpec((tm, d), lambda i: (i, 0)), _const_spec((1, d)), _const_spec(wi.shape),
                  _const_spec((1, cw)), _const_spec((1, cw)), _const_spec(ws.shape), _const_spec(bs_full.shape),
                  _const_spec(wo.shape)],
        out_specs=pl.BlockSpec((tm, d), lambda i: (i, 0)),
        out_shape=jax.ShapeDtypeStruct((m, d), F32),
        compiler_params=_params(("parallel",), tm * d * 4, wi.size * 2, ws.size * 2, bs_full.size * 4, wo.size * 2,
                                tm * d * 4, temps=5 * tm * 2 * cw * 4),
        name="gmlp",
    )(h2, nw, wi, lnw, lnb, ws, bs_full, wo)


def _delta_prep_kernel(x_ref, xp_ref, xn_ref, ab_ref, cw_ref, alog_ref, dtb_ref,
                       wq_ref, u_ref, aq_ref, bt_ref, gm_ref,
                       xe_ref, q_scr, k_scr, v_scr, g_scr, b_scr, *solver_refs, heads, taps, group):
    i = pl.program_id(1)
    rows = x_ref.shape[1]
    width = x_ref.shape[2]
    aw = width // 3
    hd = aw // heads
    sol = _UnitSolver(solver_refs)
    _fill_halo(xe_ref, x_ref, xp_ref, xn_ref, i, pl.num_programs(1))

    def elementwise_items(c):
        r0 = c * CHUNK
        rs = slice(r0, r0 + CHUNK)

        def conv(cols):
            acc = jnp.zeros((CHUNK, hd), F32)
            for j in range(taps):
                s = HALO - taps // 2 + j + r0
                acc = acc + cw_ref[j:j + 1, cols] * xe_ref[s:s + CHUNK, cols]
            return _silu(acc)

        def head(hh):
            qh = conv(slice(hh * hd, (hh + 1) * hd))
            kh = conv(slice(aw + hh * hd, aw + (hh + 1) * hd))
            q_scr[hh, rs, :] = qh * lax.rsqrt(jnp.sum(qh * qh, axis=-1, keepdims=True) + L2_EPS) * (hd ** -0.5)
            k_scr[hh, rs, :] = kh * lax.rsqrt(jnp.sum(kh * kh, axis=-1, keepdims=True) + L2_EPS)
            v_scr[hh, rs, :] = conv(slice(2 * aw + hh * hd, 2 * aw + (hh + 1) * hd))

        def gates():
            ab = ab_ref[0, rs, :]
            g_scr[rs, :] = -jnp.exp(alog_ref[...]) * _softplus(ab + dtb_ref[...])
            b_scr[rs, :] = _sigmoid(ab)

        return [functools.partial(head, hh) for hh in range(heads)] + [gates]

    incl, strict, blk, _, eye = _unit_masks(False, True)
    r1 = _iota2((UNIT, 1), 0)
    top_col = r1 < CHUNK
    top_row = _iota2((1, UNIT), 1) < CHUNK
    tri_r = _iota2((UNIT, CHUNK), 0)
    tri_c = _iota2((UNIT, CHUNK), 1)
    cum_sel = jnp.where(jnp.where(tri_r < CHUNK, tri_r - tri_c, tri_c - tri_r + CHUNK) >= 0, 1.0, 0.0).astype(BF16)

    def stage_items(c, u0):
        rows_c = slice(c * CHUNK, (c + 1) * CHUNK)
        st = {}

        def cumulate():
            st["gst"] = _dot_sel(cum_sel, g_scr[rows_c, :])
            st["gst_t"] = st["gst"].T
            bch = b_scr[rows_c, :]
            st["bst"] = jnp.concatenate([bch, bch], axis=0)

        def head(hh):
            gst, gst_t, bst = st["gst"], st["gst_t"], st["bst"]
            fw, bw = hh, heads + hh
            gcs = jnp.where(top_col, gst[:, fw:fw + 1], gst[:, bw:bw + 1])
            grow = jnp.where(top_row, gst_t[fw:fw + 1, :], gst_t[bw:bw + 1, :])
            bcol = jnp.where(top_col, bst[:, 2 * heads + fw:2 * heads + fw + 1],
                             bst[:, 2 * heads + bw:2 * heads + bw + 1])
            glast = jnp.where(top_col, gst[CHUNK - 1:CHUNK, fw:fw + 1], gst[CHUNK:CHUNK + 1, bw:bw + 1])
            q = q_scr[hh, rows_c, :]
            k = k_scr[hh, rows_c, :]
            v = v_scr[hh, rows_c, :]
            qst = jnp.concatenate([q, q], axis=0)
            kst = jnp.concatenate([k, k], axis=0)
            vst = jnp.concatenate([v, v], axis=0)
            kst_b = kst.astype(BF16)
            decay = jnp.where(incl, jnp.exp(jnp.where(incl, gcs - grow, 0.0)), 0.0)
            kk = _dot_nt(kst_b, kst_b)
            m = jnp.where(strict, bcol * kk * decay, 0.0)
            sol.stage(u0 + hh, -m, blk, eye)
            eg = jnp.exp(gcs)
            sol.set_rhs(u0 + hh, 0, vst * bcol)
            sol.set_rhs(u0 + hh, 1, kst * (bcol * eg))
            qk = _dot_nt(qst, kst_b) * decay
            qd = qst * eg
            kdec = kst * jnp.exp(glast - gcs)
            gam = jnp.exp(glast)
            for dr in range(2):
                sl = slice(dr * CHUNK, (dr + 1) * CHUNK)
                wq_ref.at[dr, 0, c, hh][CHUNK:, :] = qd[sl].astype(BF16)
                aq_ref[dr, 0, c, hh] = qk[sl].astype(BF16)
                bt_ref[dr, 0, c, hh] = kdec[sl].T.astype(BF16)
                gm_ref[dr, 0, c, hh] = jnp.broadcast_to(gam[dr * CHUNK:dr * CHUNK + 1], (UNIT, GAM_LANES))

        return [cumulate] + [functools.partial(head, hh) for hh in range(heads)]

    def finish_items(c, u0):
        def head(hh):
            un, w = sol.result(u0 + hh, 0), sol.result(u0 + hh, 1)
            for dr in range(2):
                sl = slice(dr * CHUNK, (dr + 1) * CHUNK)
                wq_ref.at[dr, 0, c, hh][:CHUNK, :] = (-w[sl]).astype(BF16)
                u_ref[dr, 0, c, hh] = un[sl]

        return [functools.partial(head, hh) for hh in range(heads)]

    nu = group * heads
    chunks = lambda gi: range(gi * group, (gi + 1) * group)

    def prepare(gi, base):
        items = []
        for cc, c in enumerate(chunks(gi)):
            items += elementwise_items(c) + stage_items(c, base + cc * heads)
        return items

    def finish(gi, base):
        return [it for cc, c in enumerate(chunks(gi)) for it in finish_items(c, base + cc * heads)]

    n_groups = rows // (CHUNK * group)
    _emit(prepare(0, 0))
    for gi in range(n_groups):
        base, other = (gi % 2) * nu, ((gi + 1) % 2) * nu
        side = finish(gi - 1, other) if gi > 0 else []
        if gi + 1 < n_groups:
            side = side + prepare(gi + 1, other)
        _emit(sol.items(range(base, base + nu)), side)
    _emit(finish(n_groups - 1, ((n_groups - 1) % 2) * nu))


def _delta_prep(qkv, ab, conv_w, alog_row, dtb_row, heads, rows=256):
    b, t, width = qkv.shape
    hd = width // 3 // heads
    assert hd == LANES
    n = t // CHUNK
    nb = rows // CHUNK
    taps = conv_w.shape[0]
    cw = jnp.zeros((SUBLANES, width), F32).at[:taps].set(conv_w)
    group = UNITS_IN_FLIGHT // heads
    assert nb % group == 0
    unit = lambda r, dt, w=LANES: jax.ShapeDtypeStruct((2, b, n, heads, r, w), dt)
    ospec = lambda r, w=LANES: pl.BlockSpec((2, 1, nb, heads, r, w), lambda bi, i: (0, bi, i, 0, 0, 0))
    return pl.pallas_call(
        functools.partial(_delta_prep_kernel, heads=heads, taps=taps, group=group),
        grid=(b, t // rows),
        in_specs=_halo_specs(rows, width, t) + [
            pl.BlockSpec((1, rows, LANES), lambda bi, i: (bi, i, 0)),
            _const_spec((SUBLANES, width)), _const_spec((1, LANES)), _const_spec((1, LANES))],
        out_specs=[ospec(UNIT), ospec(CHUNK), ospec(CHUNK), ospec(UNIT, CHUNK), ospec(UNIT, GAM_LANES)],
        out_shape=[unit(UNIT, BF16), unit(CHUNK, F32), unit(CHUNK, BF16), unit(UNIT, BF16, CHUNK),
                   unit(UNIT, F32, GAM_LANES)],
        scratch_shapes=[pltpu.VMEM((rows + 2 * HALO, width), F32)]
        + [pltpu.VMEM((heads, rows, LANES), F32)] * 3 + [pltpu.VMEM((rows, LANES), F32)] * 2
        + _UnitSolver.scratch(2 * group * heads),
        compiler_params=_params(("parallel", "arbitrary"), rows * width * 4, rows * LANES * 4,
                                2 * heads * nb * (UNIT * 2 + CHUNK * 4 + CHUNK * 2 + UNIT * 2 + UNIT * 4) * LANES,
                                scratch=(2 * rows + 2 * HALO) * width * 4
                                + 2 * group * heads * _UnitSolver.BYTES_PER_UNIT, temps=4 * rows * width * 4),
        name="delta_prep",
    )(qkv, qkv, qkv, ab, cw, alog_row, dtb_row)


def _rwkv_prep_kernel(x_ref, xp_ref, xn_ref, l_ref, lp_ref, ln_ref, mux_ref, mul_ref, wcat_ref, bias_ref, g2_ref,
                      kk_w_ref, ka_ref, rk_ref,
                      wq_ref, u0_ref, aq_ref, bk_ref, vb_ref, gm_ref, bonus_ref, gate_ref,
                      xe_ref, le_ref, r_scr, v_scr, kk_scr, lw_scr, kka_scr, kd_scr, kr_scr, vs_scr,
                      *solver_refs, lora_w, lora_a):
    i = pl.program_id(1)
    nblk = pl.num_programs(1)
    sol = _UnitSolver(solver_refs)
    rows = x_ref.shape[1]
    bw = x_ref.shape[2] // 3
    pairs = bw // LANES
    _fill_halo(xe_ref, x_ref, xp_ref, xn_ref, i, nblk)
    _fill_halo(le_ref, l_ref, lp_ref, ln_ref, i, nblk)

    sr = _iota2((LANES, LANES), 0) // CHUNK
    sc = _iota2((LANES, LANES), 1) // CHUNK
    head_sel = jnp.where(sr == sc, 1.0, 0.0).astype(BF16)
    lane = _iota2((1, LANES), 1)

    def head_sum(a, pieces):
        return jnp.concatenate(
            [_dot_sel_r(a[:, p * LANES:(p + 1) * LANES], head_sel, pieces) for p in range(pairs)], axis=1)

    def elementwise_items(c):
        r0 = c * CHUNK
        rs = slice(r0, r0 + CHUNK)
        st = {}

        def shifted(e_ref, mu, cols):
            x = e_ref[HALO + r0:HALO + r0 + CHUNK, cols]
            nbr = 0.5 * (e_ref[HALO - 1 + r0:HALO - 1 + r0 + CHUNK, cols]
                         + e_ref[HALO + 1 + r0:HALO + 1 + r0 + CHUNK, cols])
            return x + (nbr - x) * mu[:, cols]

        def shift_rv():
            st["r"] = shifted(xe_ref, mux_ref, slice(0, bw))
            st["v"] = shifted(xe_ref, mux_ref, slice(2 * bw, 3 * bw))
            r_scr[rs, :] = st["r"]
            v_scr[rs, :] = st["v"]

        def shift_k():
            st["k"] = shifted(xe_ref, mux_ref, slice(bw, 2 * bw))
            kkp = st["k"] * kk_w_ref[...]
            st["kk"] = kkp * lax.rsqrt(head_sum(kkp * kkp, 1) + L2_EPS)
            kk_scr[rs, :] = st["kk"]
            st["kb"] = jnp.zeros_like(st["k"])

        def lora():
            lo = shifted(le_ref, mul_ref, slice(None))
            l01 = lo[:, :LANES]
            xin = jnp.where(lane < 2 * lora_w, jnp.tanh(l01), l01)
            st["wa"] = _dot(xin, wcat_ref[...]) + bias_ref[...]
            gate_ref[0, rs, :] = _dot(_sigmoid(lo[:, LANES:]), g2_ref[...])

        def direction(e):
            wa = st["wa"]
            a = _sigmoid(wa[:, (2 + e) * bw:(3 + e) * bw])
            lw_scr[e, rs, :] = -B_DECAY_SCALE * _sigmoid(wa[:, e * bw:(e + 1) * bw])
            kka_scr[e, rs, :] = st["kk"] * a
            kd = st["k"] * (1.0 + (a - 1.0) * ka_ref[...])
            kd_scr[e, rs, :] = kd
            st["kb"] = st["kb"] + 0.5 * kd

        def bonus():
            bonus_ref[0, rs, :] = head_sum(st["r"] * st["kb"] * rk_ref[...], 2) * st["v"]

        return [shift_rv, shift_k, lora, functools.partial(direction, 0), functools.partial(direction, 1), bonus]

    lanes2 = _iota2((1, LANES), 1)
    m0 = lanes2 < CHUNK
    tr = _iota2((CHUNK, CHUNK), 0)
    tc = _iota2((CHUNK, CHUNK), 1)
    cum_sel = (jnp.where(tc <= tr, 1.0, 0.0).astype(BF16), jnp.where(tc >= tr, 1.0, 0.0).astype(BF16))
    masks = (_unit_masks(False, False), _unit_masks(True, True))

    def stack(a):
        return jnp.concatenate([jnp.where(m0, a, 0.0), jnp.where(m0, 0.0, a)], axis=0)

    def halves(a):
        return a[:CHUNK] + a[CHUNK:]

    nu = 2 * pairs

    def stage_items(c, base):
        rows_c = slice(c * CHUNK, (c + 1) * CHUNK)
        cum = {}

        def cumulate(e):
            cum[e] = _dot_sel(cum_sel[e], lw_scr[e, rows_c, :])

        def unit_steps(e, p):
            incl, strict, blk, same, eye = masks[e]
            last = CHUNK - 1 if e == 0 else 0
            ls = slice(p * LANES, (p + 1) * LANES)
            u = base + e * pairs + p
            st = {}

            def decayed_rows():
                cs = cum[e][:, ls]
                e_neg = jnp.exp(-cs)
                st["xs"] = stack(-kk_scr[rows_c, ls] * jnp.exp(cs - lw_scr[e, rows_c, ls])).astype(BF16)
                st["bk"] = jnp.concatenate([stack(kka_scr[e, rows_c, ls] * e_neg), stack(kd_scr[e, rows_c, ls] * e_neg)],
                                           axis=0).astype(BF16)
                sol.set_rhs(u, 0, st["xs"])

            def pair_a():
                ga = _dot_nt(st["xs"], st["bk"])
                sol.stage(u, jnp.where(strict, ga[:, :UNIT], 0.0), blk, eye)
                kr_scr[u] = jnp.where(strict, ga[:, UNIT:], 0.0).astype(BF16)

            def pair_r():
                rt = r_scr[rows_c, ls] * jnp.exp(cum[e][:, ls])
                gr = _dot_nt(stack(rt), st["bk"])
                aq_ref[e, 0, c, p] = jnp.concatenate(
                    [halves(jnp.where(incl, gr[:, :UNIT], 0.0)), halves(jnp.where(incl, gr[:, UNIT:], 0.0))],
                    axis=1).astype(BF16)
                wq_ref.at[e, 0, c, p][CHUNK:, :] = rt.astype(BF16)

            def seq_operands():
                cs = cum[e][:, ls]
                cl = cum[e][last:last + 1, ls]
                e_last = jnp.exp(cl - cs)
                vv = v_scr[rows_c, ls]
                vs_scr[u] = stack(vv).astype(BF16)
                vb_ref[e, 0, c, p] = vv.astype(BF16)
                bk_ref[e, 0, c, p] = jnp.concatenate(
                    [kka_scr[e, rows_c, ls] * e_last, kd_scr[e, rows_c, ls] * e_last], axis=0).T.astype(BF16)
                gm_ref[e, 0, c, p] = jnp.broadcast_to(jnp.exp(cl), (UNIT, LANES)).T[:, :GAM_LANES]

            return [decayed_rows, pair_a, pair_r, seq_operands]

        items = []
        for e in range(2):
            items.append(functools.partial(cumulate, e))
            for p in range(pairs):
                items += unit_steps(e, p)
        return items

    def solve_items(base):
        def av(u):
            sol.set_rhs(u, 1, jnp.dot(kr_scr[u], vs_scr[u], preferred_element_type=F32))

        return ([functools.partial(av, u) for u in range(base, base + nu)]
                + sol.items(range(base, base + nu)))

    def finish_items(c, base):
        def unit(e, p):
            u = base + e * pairs + p
            wq_ref.at[e, 0, c, p][:CHUNK, :] = halves(sol.result(u, 0)).astype(BF16)
            u0_ref[e, 0, c, p] = halves(sol.result(u, 1))

        return [functools.partial(unit, e, p) for e in range(2) for p in range(pairs)]

    n_chunks = rows // CHUNK
    _emit(elementwise_items(0) + stage_items(0, 0))
    for c in range(n_chunks):
        base, other = (c % 2) * nu, ((c + 1) % 2) * nu
        side = finish_items(c - 1, other) if c > 0 else []
        if c + 1 < n_chunks:
            side = side + elementwise_items(c + 1) + stage_items(c + 1, other)
        _emit(solve_items(base), side)
    _emit(finish_items(n_chunks - 1, ((n_chunks - 1) % 2) * nu))


def _rwkv_prep(rkv, lora, mu_x, mu_l, wcat, bias, g2p, k_k, k_a, r_k, lora_w, lora_a, rows=256):
    b, t, width = rkv.shape
    bw = width // 3
    pairs = bw // LANES
    lw_ = lora.shape[2]
    n = t // CHUNK
    nb = rows // CHUNK
    unit = lambda r, dt, w=LANES: jax.ShapeDtypeStruct((2, b, n, pairs, r, w), dt)
    ospec = lambda r, w=LANES: pl.BlockSpec((2, 1, nb, pairs, r, w), lambda bi, i: (0, bi, i, 0, 0, 0))
    row_spec = pl.BlockSpec((1, rows, bw), lambda bi, i: (bi, i, 0))
    return pl.pallas_call(
        functools.partial(_rwkv_prep_kernel, lora_w=lora_w, lora_a=lora_a),
        grid=(b, t // rows),
        in_specs=_halo_specs(rows, width, t) + _halo_specs(rows, lw_, t) + [
            _const_spec((1, width)), _const_spec((1, lw_)), _const_spec(wcat.shape), _const_spec(bias.shape),
            _const_spec(g2p.shape), _const_spec((1, bw)), _const_spec((1, bw)), _const_spec((1, bw))],
        out_specs=[ospec(UNIT), ospec(CHUNK), ospec(CHUNK, 2 * LANES), ospec(UNIT), ospec(CHUNK),
                   ospec(UNIT, GAM_LANES), row_spec, row_spec],
        out_shape=[unit(UNIT, BF16), unit(CHUNK, F32), unit(CHUNK, BF16, 2 * LANES), unit(UNIT, BF16),
                   unit(CHUNK, BF16), unit(UNIT, F32, GAM_LANES),
                   jax.ShapeDtypeStruct((b, t, bw), F32), jax.ShapeDtypeStruct((b, t, bw), F32)],
        scratch_shapes=[pltpu.VMEM((rows + 2 * HALO, width), F32), pltpu.VMEM((rows + 2 * HALO, lw_), F32)]
        + [pltpu.VMEM((rows, bw), F32)] * 3 + [pltpu.VMEM((2, rows, bw), F32)] * 3
        + [pltpu.VMEM((4 * pairs, UNIT, UNIT), BF16), pltpu.VMEM((4 * pairs, UNIT, LANES), BF16)]
        + _UnitSolver.scratch(4 * pairs),
        compiler_params=_params(("parallel", "arbitrary"), rows * width * 4, rows * lw_ * 4, wcat.size * 2,
                                2 * pairs * nb * (UNIT * 2 + CHUNK * 4 + CHUNK * 4 + UNIT * 2 + CHUNK * 2 + UNIT * 4)
                                * LANES, 2 * rows * bw * 4,
                                scratch=(rows + 2 * HALO) * (width + lw_) * 4 + 9 * rows * bw * 4
                                + 4 * pairs * (_UnitSolver.BYTES_PER_UNIT + 2 * UNIT * LANES * 2),
                                temps=8 * rows * bw * 4 * 4),
        name="rwkv_prep",
    )(rkv, rkv, rkv, lora, lora, lora, mu_x, mu_l, wcat, bias, g2p, k_k, k_a, r_k)


def _seq_kernel(*refs, paired):
    if paired:
        wq_ref, u_ref, aq_ref, bt_ref, gm_ref, vb_ref, y_ref, s_ref, pq_ref = refs
    else:
        wq_ref, u_ref, aq_ref, bt_ref, gm_ref, y_ref, s_ref, pq_ref = refs
    nb_, nu_ = wq_ref.shape[1], wq_ref.shape[3]
    units = [(bi, ui) for bi in range(nb_) for ui in range(nu_)]
    at = lambda ref, bi, ui: ref[0, bi, 0, ui]

    @pl.when(pl.program_id(1) == 0)
    def _():
        s_ref[...] = jnp.zeros_like(s_ref)

    m0 = _iota2((1, LANES), 1) < CHUNK
    same = (_iota2((LANES, LANES), 0) // CHUNK) == (_iota2((LANES, LANES), 1) // CHUNK)
    for j, (bi, ui) in enumerate(units):
        pq_ref[j] = jnp.dot(at(wq_ref, bi, ui), s_ref[j].astype(BF16), preferred_element_type=F32)
    for j, (bi, ui) in enumerate(units):
        s = s_ref[j]
        pq = pq_ref[j]
        un = at(u_ref, bi, ui) + pq[:CHUNK]
        unb = un.astype(BF16)
        if paired:
            vb = at(vb_ref, bi, ui)
            zero = jnp.zeros_like(unb)
            rhs_y = jnp.concatenate([jnp.where(m0, unb, zero), jnp.where(m0, zero, unb),
                                     jnp.where(m0, vb, zero), jnp.where(m0, zero, vb)], axis=0)
            rhs_s = jnp.concatenate([unb, vb], axis=0)
        else:
            rhs_y = jnp.concatenate([unb, unb], axis=0)
            rhs_s = unb
        y_ref[0, bi, 0, ui] = pq[CHUNK:] + jnp.dot(at(aq_ref, bi, ui), rhs_y, preferred_element_type=F32)
        upd = jnp.dot(at(bt_ref, bi, ui), rhs_s, preferred_element_type=F32)
        if paired:
            upd = jnp.where(same, upd, 0.0)
        s_ref[j] = s * jnp.broadcast_to(at(gm_ref, bi, ui)[:, 0:1], (LANES, LANES)) + upd


def _seq(ops, paired):
    _, b, n, nu = ops[0].shape[:4]
    ub = b * nu

    def imap(d, ni):
        return (d, 0, ni + d * (n - 1 - 2 * ni), 0, 0, 0)

    specs = [pl.BlockSpec((1, b, 1, nu) + a.shape[4:], imap) for a in ops]
    block_bytes = [ub * math.prod(a.shape[4:]) * a.dtype.itemsize for a in ops]
    return pl.pallas_call(
        functools.partial(_seq_kernel, paired=paired),
        grid=(2, n),
        in_specs=specs,
        out_specs=pl.BlockSpec((1, b, 1, nu, CHUNK, LANES), imap),
        out_shape=jax.ShapeDtypeStruct((2, b, n, nu, CHUNK, LANES), F32),
        scratch_shapes=[pltpu.VMEM((ub, LANES, LANES), F32), pltpu.VMEM((ub, UNIT, LANES), F32)],
        compiler_params=_params(("parallel", "arbitrary"), *block_bytes, ub * UNIT * LANES * 4,
                                ub * CHUNK * LANES * 4, scratch=ub * (LANES + UNIT) * LANES * 4),
        name="seq_rwkv" if paired else "seq_delta",
    )(*ops)


def _even_out_kernel(of_ref, ob_ref, yf_ref, yb_ref, z_ref, bonus_ref, gate_ref, h_ref, an_ref, lnw_ref, lnb_ref,
                     wo_ref, o_ref):
    rows = h_ref.shape[1]
    a_heads = of_ref.shape[3]
    pairs = yf_ref.shape[3]
    z = z_ref[0]
    pieces = []
    for hh in range(a_heads):
        o = (of_ref[0, 0, :, hh] + ob_ref[0, 0, :, hh]).reshape(rows, LANES)
        pieces.append(_rms(o, an_ref[...]) * _silu(z[:, hh * LANES:(hh + 1) * LANES]))
    sr = _iota2((LANES, LANES), 0) // CHUNK
    sc = _iota2((LANES, LANES), 1) // CHUNK
    head_sel = jnp.where(sr == sc, 1.0, 0.0).astype(BF16)
    inv_n = 1.0 / CHUNK
    for p in range(pairs):
        ls = slice(p * LANES, (p + 1) * LANES)
        y = (yf_ref[0, 0, :, p] + yb_ref[0, 0, :, p]).reshape(rows, LANES)
        mu = _dot_sel_r(y, head_sel, 2) * inv_n
        yc = y - mu
        var = _dot_sel_r(yc * yc, head_sel, 2) * inv_n
        yn = yc * lax.rsqrt(var + B_GN_EPS) * lnw_ref[:, ls] + lnb_ref[:, ls]
        pieces.append((yn + bonus_ref[0][:, ls]) * gate_ref[0][:, ls])
    mix = jnp.concatenate(pieces, axis=1).astype(BF16)
    o_ref[0] = h_ref[0] + jnp.dot(mix, wo_ref[...], preferred_element_type=F32)


def _even_out(od, yr, z, bonus, gate, h3, a_norm, ln_w, ln_b, wo, rows=256):
    b, t, d = h3.shape
    nb = rows // CHUNK
    a_heads, pairs = od.shape[3], yr.shape[3]
    aw, bw = z.shape[2], bonus.shape[2]
    dspec = lambda dr, nh: pl.BlockSpec((1, 1, nb, nh, CHUNK, LANES), lambda bi, i: (dr, bi, i, 0, 0, 0))
    rspec = lambda w: pl.BlockSpec((1, rows, w), lambda bi, i: (bi, i, 0))
    return pl.pallas_call(
        _even_out_kernel,
        grid=(b, t // rows),
        in_specs=[dspec(0, a_heads), dspec(1, a_heads), dspec(0, pairs), dspec(1, pairs), rspec(aw), rspec(bw),
                  rspec(bw), rspec(d), _const_spec((1, LANES)), _const_spec((1, bw)), _const_spec((1, bw)),
                  _const_spec(wo.shape)],
        out_specs=rspec(d),
        out_shape=jax.ShapeDtypeStruct((b, t, d), F32),
        compiler_params=_params(("parallel", "parallel"), 2 * rows * (aw + bw) * 4, rows * (aw + 2 * bw + 2 * d) * 4,
                                wo.size * 2, temps=6 * rows * d * 4),
        name="even_out",
    )(od, od, yr, yr, z, bonus, gate, h3, a_norm, ln_w, ln_b, wo)


def _even_layer(h3, nw, w_in, w_out, a_conv, a_log, a_dt_bias, a_norm, b_mu, b_w0, b_w2, b_a0, b_a2, b_g2,
                b_k_k, b_k_a, b_r_k, b_ln_w, b_ln_b):
    b, t, d = h3.shape
    a_heads = a_log.shape[1]
    hd = a_norm.shape[0]
    aw = a_heads * hd
    b_heads, b_hd = b_r_k.shape
    bw = b_heads * b_hd
    lora_w, lora_a, lora_g = b_w2.shape[1], b_a2.shape[1], b_g2.shape[0]
    assert b_hd == CHUNK and hd == LANES and 2 * (lora_w + lora_a) == LANES and lora_g <= LANES
    a_cols = 4 * aw + 4 * a_heads
    n_gate = 4 * a_heads
    pad_cols = lambda w, n: jnp.pad(w, ((0, 0), (0, n - w.shape[1])))
    w_bf = w_in.astype(BF16)
    wa, wb = w_bf[:, :a_cols], w_bf[:, a_cols:]
    w_all = jnp.concatenate([wa[:, :4 * aw], pad_cols(wa[:, 4 * aw:], LANES),
                             wb[:, :3 * bw], pad_cols(wb[:, 3 * bw:], 2 * LANES)], axis=1)
    splits = (3 * aw, aw, LANES, 3 * bw, 2 * LANES)
    qkv, z, ab, rkv, lora = _proj(h3.reshape(b * t, d), nw.reshape(1, d), w_all, splits)
    to3 = lambda a: a.reshape(b, t, a.shape[-1])

    alog_row = jnp.zeros((1, LANES), F32).at[0, :2 * a_heads].set(a_log.reshape(-1))
    dtb_row = jnp.zeros((1, LANES), F32).at[0, :2 * a_heads].set(a_dt_bias.reshape(-1))
    d_ops = _delta_prep(to3(qkv), to3(ab), a_conv, alog_row, dtb_row, a_heads)
    od = _seq(d_ops, paired=False)

    wcat = jnp.zeros((LANES, 4 * bw), F32)
    for e in range(2):
        wcat = wcat.at[e * lora_w:(e + 1) * lora_w, e * bw:(e + 1) * bw].set(b_w2[e])
        wcat = wcat.at[2 * lora_w + e * lora_a:2 * lora_w + (e + 1) * lora_a, (2 + e) * bw:(3 + e) * bw].set(b_a2[e])
    bias = jnp.concatenate([b_w0[0], b_w0[1], b_a0[0], b_a0[1]]).reshape(1, 4 * bw)
    g2p = jnp.zeros((LANES, bw), F32).at[:lora_g].set(b_g2).astype(BF16)
    mu_x = b_mu[:3 * bw].reshape(1, 3 * bw)
    mu_l = jnp.pad(b_mu[3 * bw:], (0, 2 * LANES - (b_mu.shape[0] - 3 * bw))).reshape(1, 2 * LANES)
    row = lambda a: a.reshape(1, bw)
    r_ops = _rwkv_prep(to3(rkv), to3(lora), mu_x, mu_l, wcat.astype(BF16), bias, g2p, row(b_k_k), row(b_k_a),
                       row(b_r_k), lora_w, lora_a)
    wq, u0, aq, bk, vb, gm, bonus, gate = r_ops
    yr = _seq([wq, u0, aq, bk, gm, vb], paired=True)

    return _even_out(od, yr, to3(z), bonus, gate, h3, a_norm.reshape(1, hd), row(b_ln_w), row(b_ln_b),
                     w_out.astype(BF16))


def _odd_layer(h3, nw, w_in, ln_w, ln_b, ws, bs, w_out):
    b, t, d = h3.shape
    groups, chunk, _ = ws.shape
    cw = w_out.shape[0]
    bs_full = jnp.repeat(bs.T, cw // groups, axis=1)
    out = _gmlp(h3.reshape(b * t, d), nw.reshape(1, d), w_in.astype(BF16), ln_w.reshape(1, cw), ln_b.reshape(1, cw),
                ws.astype(BF16), bs_full, w_out.astype(BF16))
    return out.reshape(b, t, d)


def kernel(x, p, norm_mix, norm_ffn, norm_ple, norm_final, w_in_even, w_out_even, a_conv, a_log, a_dt_bias, a_norm,
           b_mu, b_w0, b_w2, b_a0, b_a2, b_g2, b_k_k, b_k_a, b_r_k, b_ln_w, b_ln_b, w_in_odd, c_ln_w, c_ln_b, c_ws,
           c_bs, w_out_odd, w_gate, w_up, w_down, w_ple, w_ple_gate):
    b, t, d = x.shape
    depth = p.shape[0]
    h = x
    for i in range(depth):
        j = i // 2
        if i % 2 == 0:
            h = _even_layer(h, norm_mix[i], w_in_even[j], w_out_even[j], a_conv[j], a_log[j], a_dt_bias[j], a_norm[j],
                            b_mu[j], b_w0[j], b_w2[j], b_a0[j], b_a2[j], b_g2[j], b_k_k[j], b_k_a[j], b_r_k[j],
                            b_ln_w[j], b_ln_b[j])
        else:
            h = _odd_layer(h, norm_mix[i], w_in_odd[j], c_ln_w[j], c_ln_b[j], c_ws[j], c_bs[j], w_out_odd[j])
        h2 = h.reshape(b * t, d)
        h2 = _ffn_ple(h2, p[i].reshape(b * t, -1), norm_ffn[i].reshape(1, d), w_gate[i].astype(BF16),
                      w_up[i].astype(BF16), w_down[i].astype(BF16), norm_ple[i].reshape(1, d),
                      w_ple[i].astype(BF16), w_ple_gate[i].astype(BF16),
                      norm_final.reshape(1, d) if i == depth - 1 else None)
        h = h2.reshape(b, t, d)
    return h
```

```python
import functools
import math

import jax
import jax.numpy as jnp
from jax import lax
from jax.experimental import pallas as pl
from jax.experimental.pallas import tpu as pltpu

F32 = jnp.float32
BF16 = jnp.bfloat16

NORM_EPS = 1e-6
L2_EPS = 1e-6
B_GN_EPS = 64e-5
B_DECAY_SCALE = 0.606531
C_LN_EPS = 1e-5

LANES = 128
SUBLANES = 8
CHUNK = 64
UNIT = 2 * CHUNK
INV_BLOCK = 16
GAM_ROWS = SUBLANES
UNITS_IN_FLIGHT = 8
HALO = SUBLANES
VMEM_BUDGET = 56 * 1024 * 1024


def _params(sem, *block_bytes, scratch=0, temps=0, flags=None):
    need = 2 * sum(block_bytes) + scratch + temps
    return pltpu.CompilerParams(dimension_semantics=sem, vmem_limit_bytes=int(min(max(need, 16 << 20), VMEM_BUDGET)),
                                flags=flags)


def _dot(a, b):
    return jnp.dot(a.astype(BF16), b.astype(BF16), preferred_element_type=F32)


def _dot_nt(a, b):
    return lax.dot_general(a.astype(BF16), b.astype(BF16), (((1,), (1,)), ((), ())), preferred_element_type=F32)


def _split3(x):
    hi = x.astype(BF16)
    r1 = x - hi.astype(F32)
    mid = r1.astype(BF16)
    lo = (r1 - mid.astype(F32)).astype(BF16)
    return hi, mid, lo


def _dot_sel(sel, x):
    hi, mid, lo = _split3(x)
    d = lambda p: jnp.dot(sel, p, preferred_element_type=F32)
    return d(hi) + d(mid) + d(lo)


def _dot_sel_r(x, sel, pieces=3):
    d = lambda p: jnp.dot(p, sel, preferred_element_type=F32)
    return sum(d(p) for p in _split3(x)[:pieces])


def _sigmoid(x):
    return 1.0 / (1.0 + jnp.exp(-x))


def _silu(x):
    return x * _sigmoid(x)


def _softplus(x):
    return jnp.maximum(x, 0.0) + jnp.log(1.0 + jnp.exp(-jnp.abs(x)))


def _gelu_tanh(x):
    return 0.5 * x * (1.0 + jnp.tanh(math.sqrt(2.0 / math.pi) * (x + 0.044715 * (x * x * x))))


def _rms(x, w, eps=NORM_EPS):
    return x * lax.rsqrt(jnp.mean(x * x, axis=-1, keepdims=True) + eps) * w


def _iota2(shape, dim):
    return lax.broadcasted_iota(jnp.int32, shape, dim)


def _unit_masks(upper_top, upper_bot):
    r = _iota2((UNIT, UNIT), 0)
    c = _iota2((UNIT, UNIT), 1)
    same = (r // CHUNK) == (c // CHUNK)
    sign = jnp.where(r < CHUNK, -1 if upper_top else 1, -1 if upper_bot else 1)
    d = jnp.where(same, (r - c) * sign, -1)
    incl = d >= 0
    strict = d > 0
    blk = (r // INV_BLOCK) == (c // INV_BLOCK)
    eye = jnp.where(r == c, 1.0, 0.0).astype(F32)
    return incl, strict, blk, same, eye


class _UnitSolver:
    def __init__(self, refs):
        self.p, self.t, self.z, self.x, self.y = refs

    @staticmethod
    def scratch(nu):
        return [pltpu.VMEM((nu, UNIT, UNIT), BF16), pltpu.VMEM((nu, UNIT, UNIT), F32),
                pltpu.VMEM((nu, UNIT, 3 * UNIT), BF16), pltpu.VMEM((2, nu, UNIT, UNIT), BF16),
                pltpu.VMEM((nu, UNIT, 2 * UNIT), F32)]

    BYTES_PER_UNIT = UNIT * UNIT * (2 + 4 + 6 + 4 + 8)

    def stage(self, u, a, blk, eye):
        ad = jnp.where(blk, a, 0.0)
        self.p[u] = ad.astype(BF16)
        self.t[u] = eye + ad
        self.z[u, :, :UNIT] = jnp.where(blk, 0.0, a).astype(BF16)

    def set_rhs(self, u, tile, val):
        self.z[u, :, (tile + 1) * UNIT:(tile + 2) * UNIT] = val.astype(BF16)

    def result(self, u, tile):
        return self.y[u, :, tile * UNIT:(tile + 1) * UNIT]

    def items(self, units):
        assert CHUNK // INV_BLOCK == 4
        dot = lambda a, b: jnp.dot(a, b, preferred_element_type=F32)

        def square(u):
            p = self.p[u]
            self.p[u] = dot(p, p).astype(BF16)

        def grow(u):
            t = self.t[u]
            self.t[u] = t + dot(t.astype(BF16), self.p[u])

        def apply_t(u):
            t = self.t[u].astype(BF16)
            self.x[0, u] = dot(t, self.z[u, :, :UNIT]).astype(BF16)
            self.y[u] = dot(t, self.z[u, :, UNIT:])

        def apply_x(u):
            x = self.x[0, u]
            self.x[1, u] = dot(x, x).astype(BF16)
            y = self.y[u]
            self.y[u] = y + dot(x, y.astype(BF16))

        def apply_x2(u):
            y = self.y[u]
            self.y[u] = y + dot(self.x[1, u], y.astype(BF16))

        stages = [square, grow] * (int(math.log2(INV_BLOCK)) - 1) + [apply_t, apply_x, apply_x2]
        return [functools.partial(stage, u) for stage in stages for u in units]


def _emit(main, side=()):
    n, m = len(main), len(side)
    j = 0
    for idx, step in enumerate(main):
        step()
        while j < m and (j + 1) * n <= (idx + 1) * m:
            side[j]()
            j += 1
    for step in side[j:]:
        step()


def _fill_halo(xe_ref, x_ref, prev_ref, next_ref, i, nblk):
    rows = x_ref.shape[1]
    pf = jnp.where(i > 0, 1.0, 0.0).astype(F32)
    nf = jnp.where(i < nblk - 1, 1.0, 0.0).astype(F32)
    xe_ref[0:HALO, :] = prev_ref[0] * pf
    xe_ref[HALO:HALO + rows, :] = x_ref[0]
    xe_ref[HALO + rows:2 * HALO + rows, :] = next_ref[0] * nf


def _halo_specs(rows, width, t_len):
    per = rows // HALO
    last = t_len // HALO - 1
    return [
        pl.BlockSpec((1, rows, width), lambda b, i: (b, i, 0)),
        pl.BlockSpec((1, HALO, width), lambda b, i: (b, jnp.maximum(i * per - 1, 0), 0)),
        pl.BlockSpec((1, HALO, width), lambda b, i: (b, jnp.minimum((i + 1) * per, last), 0)),
    ]


def _const_spec(shape, single_buffer=False):
    nd = len(shape)
    if single_buffer:
        return pl.BlockSpec(shape, lambda *_: (0,) * nd, pipeline_mode=pl.Buffered(1))
    return pl.BlockSpec(shape, lambda *_: (0,) * nd)


def _proj_kernel(h_ref, nw_ref, w_ref, *o_refs, splits):
    hn = _rms(h_ref[...], nw_ref[...]).astype(BF16)
    off = 0
    for o_ref, n in zip(o_refs, splits):
        o_ref[...] = jnp.dot(hn, w_ref[:, off:off + n], preferred_element_type=F32)
        off += n


def _proj(h2, nw, w, splits, tm=512):
    m, d = h2.shape
    n = w.shape[1]
    return pl.pallas_call(
        functools.partial(_proj_kernel, splits=splits),
        grid=(m // tm,),
        in_specs=[pl.BlockSpec((tm, d), lambda i: (i, 0)), _const_spec((1, d)), _const_spec((d, n))],
        out_specs=[pl.BlockSpec((tm, s), lambda i: (i, 0)) for s in splits],
        out_shape=[jax.ShapeDtypeStruct((m, s), F32) for s in splits],
        compiler_params=_params(("parallel",), tm * d * 4, d * n * 2, tm * n * 4, temps=tm * d * 8),
        name="proj",
    )(h2, nw, w)


def _ffn_ple_kernel(h_ref, p_ref, nw_ref, wg_ref, wu_ref, wd_ref, npl_ref, wp_ref, wpg_ref, *rest, final):
    if final:
        nf_ref, o_ref = rest
    else:
        (o_ref,) = rest
    h = h_ref[...]
    hn = _rms(h, nw_ref[...]).astype(BF16)
    g = jnp.dot(hn, wg_ref[...], preferred_element_type=F32)
    u = jnp.dot(hn, wu_ref[...], preferred_element_type=F32)
    h = h + jnp.dot((_silu(g) * u).astype(BF16), wd_ref[...], preferred_element_type=F32)
    gate = _sigmoid(jnp.dot(_rms(h, npl_ref[...]).astype(BF16), wpg_ref[...], preferred_element_type=F32))
    e = jnp.dot(p_ref[...].astype(BF16), wp_ref[...], preferred_element_type=F32)
    out = h + e * gate
    if final:
        out = _rms(out, nf_ref[...])
    o_ref[...] = out


def _ffn_ple(h2, p2, nw, wg, wu, wd, npl, wp, wpg, nf=None, tm=512):
    m, d = h2.shape
    ff = wg.shape[1]
    pd = p2.shape[1]
    final = nf is not None
    ins = [h2, p2, nw, wg, wu, wd, npl, wp, wpg] + ([nf] if final else [])
    specs = [
        pl.BlockSpec((tm, d), lambda i: (i, 0)),
        pl.BlockSpec((tm, pd), lambda i: (i, 0)),
        _const_spec((1, d)),
        _const_spec((d, ff), True), _const_spec((d, ff), True), _const_spec((ff, d), True),
        _const_spec((1, d)), _const_spec((pd, d), True), _const_spec((d, d), True),
    ] + ([_const_spec((1, d))] if final else [])
    weights = (3 * d * ff + pd * d + d * d) * 2
    return pl.pallas_call(
        functools.partial(_ffn_ple_kernel, final=final),
        grid=(m // tm,),
        in_specs=specs,
        out_specs=pl.BlockSpec((tm, d), lambda i: (i, 0)),
        out_shape=jax.ShapeDtypeStruct((m, d), F32),
        compiler_params=_params(("parallel",), tm * d * 4, tm * pd * 4, tm * d * 4,
                                scratch=weights, temps=tm * ff * (4 + 4 + 2) + 3 * tm * d * 4),
        name="ffn_ple",
    )(*ins)


def _gmlp_kernel(h_ref, nw_ref, wi_ref, lnw_ref, lnb_ref, ws_ref, bs_ref, wo_ref, o_ref, *, chunk, groups):
    h = h_ref[...]
    rows, d = h.shape
    cw = wo_ref.shape[0]
    gd = cw // groups
    hn = _rms(h, nw_ref[...]).astype(BF16)
    uv = _gelu_tanh(jnp.dot(hn, wi_ref[...], preferred_element_type=F32))
    u = uv[:, :cw]
    v = uv[:, cw:]
    mu = jnp.mean(v, axis=-1, keepdims=True)
    vc = v - mu
    var = jnp.mean(vc * vc, axis=-1, keepdims=True)
    vn = (vc * lax.rsqrt(var + C_LN_EPS) * lnw_ref[...] + lnb_ref[...]).astype(BF16)
    bs = bs_ref[...]
    pieces = []
    for c in range(rows // chunk):
        vrow = vn[c * chunk:(c + 1) * chunk]
        sv = jnp.concatenate(
            [jnp.dot(ws_ref[g], vrow[:, g * gd:(g + 1) * gd], preferred_element_type=F32) for g in range(groups)],
            axis=1)
        pieces.append((u[c * chunk:(c + 1) * chunk] * (sv + bs)).astype(BF16))
    gated = jnp.concatenate(pieces, axis=0)
    o_ref[...] = h + jnp.dot(gated, wo_ref[...], preferred_element_type=F32)


def _gmlp(h2, nw, wi, lnw, lnb, ws, bs_full, wo, tm=512):
    m, d = h2.shape
    groups, chunk, _ = ws.shape
    cw = wo.shape[0]
    return pl.pallas_call(
        functools.partial(_gmlp_kernel, chunk=chunk, groups=groups),
        grid=(m // tm,),
        in_specs=[pl.BlockSpec((tm, d), lambda i: (i, 0)), _const_spec((1, d)), _const_spec(wi.shape),
                  _const_spec((1, cw)), _const_spec((1, cw)), _const_spec(ws.shape), _const_spec(bs_full.shape),
                  _const_spec(wo.shape)],
        out_specs=pl.BlockSpec((tm, d), lambda i: (i, 0)),
        out_shape=jax.ShapeDtypeStruct((m, d), F32),
        compiler_params=_params(("parallel",), tm * d * 4, wi.size * 2, ws.size * 2, bs_full.size * 4, wo.size * 2,
                                tm * d * 4, temps=5 * tm * 2 * cw * 4),
        name="gmlp",
    )(h2, nw, wi, lnw, lnb, ws, bs_full, wo)


def _delta_prep_kernel(x_ref, xp_ref, xn_ref, ab_ref, cw_ref, alog_ref, dtb_ref,
                       wq_ref, u_ref, aq_ref, bt_ref, gm_ref,
                       xe_ref, q_scr, k_scr, v_scr, g_scr, b_scr, *solver_refs, heads, taps, group):
    i = pl.program_id(1)
    rows = x_ref.shape[1]
    width = x_ref.shape[2]
    aw = width // 3
    hd = aw // heads
    sol = _UnitSolver(solver_refs)
    _fill_halo(xe_ref, x_ref, xp_ref, xn_ref, i, pl.num_programs(1))

    def elementwise_items(c):
        r0 = c * CHUNK
        rs = slice(r0, r0 + CHUNK)

        def conv(cols):
            acc = jnp.zeros((CHUNK, hd), F32)
            for j in range(taps):
                s = HALO - taps // 2 + j + r0
                acc = acc + cw_ref[j:j + 1, cols] * xe_ref[s:s + CHUNK, cols]
            return _silu(acc)

        def head(hh):
            qh = conv(slice(hh * hd, (hh + 1) * hd))
            kh = conv(slice(aw + hh * hd, aw + (hh + 1) * hd))
            q_scr[hh, rs, :] = qh * lax.rsqrt(jnp.sum(qh * qh, axis=-1, keepdims=True) + L2_EPS) * (hd ** -0.5)
            k_scr[hh, rs, :] = kh * lax.rsqrt(jnp.sum(kh * kh, axis=-1, keepdims=True) + L2_EPS)
            v_scr[hh, rs, :] = conv(slice(2 * aw + hh * hd, 2 * aw + (hh + 1) * hd))

        def gates():
            ab = ab_ref[0, rs, :]
            g_scr[rs, :] = -jnp.exp(alog_ref[...]) * _softplus(ab + dtb_ref[...])
            b_scr[rs, :] = _sigmoid(ab)

        return [functools.partial(head, hh) for hh in range(heads)] + [gates]

    incl, strict, blk, _, eye = _unit_masks(False, True)
    r1 = _iota2((UNIT, 1), 0)
    top_col = r1 < CHUNK
    top_row = _iota2((1, UNIT), 1) < CHUNK
    tri_r = _iota2((UNIT, CHUNK), 0)
    tri_c = _iota2((UNIT, CHUNK), 1)
    cum_sel = jnp.where(jnp.where(tri_r < CHUNK, tri_r - tri_c, tri_c - tri_r + CHUNK) >= 0, 1.0, 0.0).astype(BF16)

    def stage_items(c, u0):
        rows_c = slice(c * CHUNK, (c + 1) * CHUNK)
        st = {}

        def cumulate():
            st["gst"] = _dot_sel(cum_sel, g_scr[rows_c, :])
            st["gst_t"] = st["gst"].T
            bch = b_scr[rows_c, :]
            st["bst"] = jnp.concatenate([bch, bch], axis=0)

        def head(hh):
            gst, gst_t, bst = st["gst"], st["gst_t"], st["bst"]
            fw, bw = hh, heads + hh
            gcs = jnp.where(top_col, gst[:, fw:fw + 1], gst[:, bw:bw + 1])
            grow = jnp.where(top_row, gst_t[fw:fw + 1, :], gst_t[bw:bw + 1, :])
            bcol = jnp.where(top_col, bst[:, 2 * heads + fw:2 * heads + fw + 1],
                             bst[:, 2 * heads + bw:2 * heads + bw + 1])
            glast = jnp.where(top_col, gst[CHUNK - 1:CHUNK, fw:fw + 1], gst[CHUNK:CHUNK + 1, bw:bw + 1])
            q = q_scr[hh, rows_c, :]
            k = k_scr[hh, rows_c, :]
            v = v_scr[hh, rows_c, :]
            qst = jnp.concatenate([q, q], axis=0)
            kst = jnp.concatenate([k, k], axis=0)
            vst = jnp.concatenate([v, v], axis=0)
            kst_b = kst.astype(BF16)
            decay = jnp.where(incl, jnp.exp(jnp.where(incl, gcs - grow, 0.0)), 0.0)
            kk = _dot_nt(kst_b, kst_b)
            m = jnp.where(strict, bcol * kk * decay, 0.0)
            sol.stage(u0 + hh, -m, blk, eye)
            eg = jnp.exp(gcs)
            sol.set_rhs(u0 + hh, 0, vst * bcol)
            sol.set_rhs(u0 + hh, 1, kst * (bcol * eg))
            qk = _dot_nt(qst, kst_b) * decay
            qd = qst * eg
            kdec = kst * jnp.exp(glast - gcs)
            gam = jnp.exp(glast)
            for dr in range(2):
                sl = slice(dr * CHUNK, (dr + 1) * CHUNK)
                wq_ref.at[dr, 0, c, hh][CHUNK:, :] = qd[sl].astype(BF16)
                aq_ref[dr, 0, c, hh] = qk[sl].astype(BF16)
                bt_ref[dr, 0, c, hh] = kdec[sl].T.astype(BF16)
                gm_ref[dr, 0, c, hh] = jnp.broadcast_to(gam[dr * CHUNK:dr * CHUNK + 1], (GAM_ROWS, LANES))

        return [cumulate] + [functools.partial(head, hh) for hh in range(heads)]

    def finish_items(c, u0):
        def head(hh):
            un, w = sol.result(u0 + hh, 0), sol.result(u0 + hh, 1)
            for dr in range(2):
                sl = slice(dr * CHUNK, (dr + 1) * CHUNK)
                wq_ref.at[dr, 0, c, hh][:CHUNK, :] = (-w[sl]).astype(BF16)
                u_ref[dr, 0, c, hh] = un[sl]

        return [functools.partial(head, hh) for hh in range(heads)]

    nu = group * heads
    chunks = lambda gi: range(gi * group, (gi + 1) * group)

    def prepare(gi, base):
        items = []
        for cc, c in enumerate(chunks(gi)):
            items += elementwise_items(c) + stage_items(c, base + cc * heads)
        return items

    def finish(gi, base):
        return [it for cc, c in enumerate(chunks(gi)) for it in finish_items(c, base + cc * heads)]

    n_groups = rows // (CHUNK * group)
    _emit(prepare(0, 0))
    for gi in range(n_groups):
        base, other = (gi % 2) * nu, ((gi + 1) % 2) * nu
        side = finish(gi - 1, other) if gi > 0 else []
        if gi + 1 < n_groups:
            side = side + prepare(gi + 1, other)
        _emit(sol.items(range(base, base + nu)), side)
    _emit(finish(n_groups - 1, ((n_groups - 1) % 2) * nu))


def _delta_prep(qkv, ab, conv_w, alog_row, dtb_row, heads, rows=256):
    b, t, width = qkv.shape
    hd = width // 3 // heads
    assert hd == LANES
    n = t // CHUNK
    nb = rows // CHUNK
    taps = conv_w.shape[0]
    cw = jnp.zeros((SUBLANES, width), F32).at[:taps].set(conv_w)
    group = UNITS_IN_FLIGHT // heads
    assert nb % group == 0
    unit = lambda r, dt, w=LANES: jax.ShapeDtypeStruct((2, b, n, heads, r, w), dt)
    ospec = lambda r, w=LANES: pl.BlockSpec((2, 1, nb, heads, r, w), lambda bi, i: (0, bi, i, 0, 0, 0))
    return pl.pallas_call(
        functools.partial(_delta_prep_kernel, heads=heads, taps=taps, group=group),
        grid=(b, t // rows),
        in_specs=_halo_specs(rows, width, t) + [
            pl.BlockSpec((1, rows, LANES), lambda bi, i: (bi, i, 0)),
            _const_spec((SUBLANES, width)), _const_spec((1, LANES)), _const_spec((1, LANES))],
        out_specs=[ospec(UNIT), ospec(CHUNK), ospec(CHUNK), ospec(UNIT, CHUNK), ospec(GAM_ROWS)],
        out_shape=[unit(UNIT, BF16), unit(CHUNK, F32), unit(CHUNK, BF16), unit(UNIT, BF16, CHUNK),
                   unit(GAM_ROWS, F32)],
        scratch_shapes=[pltpu.VMEM((rows + 2 * HALO, width), F32)]
        + [pltpu.VMEM((heads, rows, LANES), F32)] * 3 + [pltpu.VMEM((rows, LANES), F32)] * 2
        + _UnitSolver.scratch(2 * group * heads),
        compiler_params=_params(("parallel", "arbitrary"), rows * width * 4, rows * LANES * 4,
                                2 * heads * nb * (UNIT * 2 + CHUNK * 4 + CHUNK * 2 + UNIT * 2 + UNIT * 4) * LANES,
                                scratch=(2 * rows + 2 * HALO) * width * 4
                                + 2 * group * heads * _UnitSolver.BYTES_PER_UNIT, temps=4 * rows * width * 4),
        name="delta_prep",
    )(qkv, qkv, qkv, ab, cw, alog_row, dtb_row)


def _rwkv_prep_kernel(x_ref, xp_ref, xn_ref, l_ref, lp_ref, ln_ref, mux_ref, mul_ref, wcat_ref, bias_ref, g2_ref,
                      kk_w_ref, ka_ref, rk_ref,
                      wq_ref, u0_ref, aq_ref, bk_ref, vb_ref, gm_ref, bonus_ref, gate_ref,
                      xe_ref, le_ref, r_scr, v_scr, kk_scr, lw_scr, kka_scr, kd_scr, kr_scr, vs_scr,
                      *solver_refs, lora_w, lora_a):
    i = pl.program_id(1)
    nblk = pl.num_programs(1)
    sol = _UnitSolver(solver_refs)
    rows = x_ref.shape[1]
    bw = x_ref.shape[2] // 3
    pairs = bw // LANES
    _fill_halo(xe_ref, x_ref, xp_ref, xn_ref, i, nblk)
    _fill_halo(le_ref, l_ref, lp_ref, ln_ref, i, nblk)

    sr = _iota2((LANES, LANES), 0) // CHUNK
    sc = _iota2((LANES, LANES), 1) // CHUNK
    head_sel = jnp.where(sr == sc, 1.0, 0.0).astype(BF16)
    lane = _iota2((1, LANES), 1)

    def head_sum(a, pieces):
        return jnp.concatenate(
            [_dot_sel_r(a[:, p * LANES:(p + 1) * LANES], head_sel, pieces) for p in range(pairs)], axis=1)

    def elementwise_items(c):
        r0 = c * CHUNK
        rs = slice(r0, r0 + CHUNK)
        st = {}

        def shifted(e_ref, mu, cols):
            x = e_ref[HALO + r0:HALO + r0 + CHUNK, cols]
            nbr = 0.5 * (e_ref[HALO - 1 + r0:HALO - 1 + r0 + CHUNK, cols]
                         + e_ref[HALO + 1 + r0:HALO + 1 + r0 + CHUNK, cols])
            return x + (nbr - x) * mu[:, cols]

        def shift_rv():
            st["r"] = shifted(xe_ref, mux_ref, slice(0, bw))
            st["v"] = shifted(xe_ref, mux_ref, slice(2 * bw, 3 * bw))
            r_scr[rs, :] = st["r"]
            v_scr[rs, :] = st["v"]

        def shift_k():
            st["k"] = shifted(xe_ref, mux_ref, slice(bw, 2 * bw))
            kkp = st["k"] * kk_w_ref[...]
            st["kk"] = kkp * lax.rsqrt(head_sum(kkp * kkp, 1) + L2_EPS)
            kk_scr[rs, :] = st["kk"]
            st["kb"] = jnp.zeros_like(st["k"])

        def lora():
            lo = shifted(le_ref, mul_ref, slice(None))
            l01 = lo[:, :LANES]
            xin = jnp.where(lane < 2 * lora_w, jnp.tanh(l01), l01)
            st["wa"] = _dot(xin, wcat_ref[...]) + bias_ref[...]
            gate_ref[0, rs, :] = _dot(_sigmoid(lo[:, LANES:]), g2_ref[...])

        def direction(e):
            wa = st["wa"]
            a = _sigmoid(wa[:, (2 + e) * bw:(3 + e) * bw])
            lw_scr[e, rs, :] = -B_DECAY_SCALE * _sigmoid(wa[:, e * bw:(e + 1) * bw])
            kka_scr[e, rs, :] = st["kk"] * a
            kd = st["k"] * (1.0 + (a - 1.0) * ka_ref[...])
            kd_scr[e, rs, :] = kd
            st["kb"] = st["kb"] + 0.5 * kd

        def bonus():
            bonus_ref[0, rs, :] = head_sum(st["r"] * st["kb"] * rk_ref[...], 2) * st["v"]

        return [shift_rv, shift_k, lora, functools.partial(direction, 0), functools.partial(direction, 1), bonus]

    lanes2 = _iota2((1, LANES), 1)
    m0 = lanes2 < CHUNK
    tr = _iota2((CHUNK, CHUNK), 0)
    tc = _iota2((CHUNK, CHUNK), 1)
    cum_sel = (jnp.where(tc <= tr, 1.0, 0.0).astype(BF16), jnp.where(tc >= tr, 1.0, 0.0).astype(BF16))
    masks = (_unit_masks(False, False), _unit_masks(True, True))

    def stack(a):
        return jnp.concatenate([jnp.where(m0, a, 0.0), jnp.where(m0, 0.0, a)], axis=0)

    def halves(a):
        return a[:CHUNK] + a[CHUNK:]

    nu = 2 * pairs

    def stage_items(c, base):
        rows_c = slice(c * CHUNK, (c + 1) * CHUNK)
        cum = {}

        def cumulate(e):
            cum[e] = _dot_sel(cum_sel[e], lw_scr[e, rows_c, :])

        def unit_steps(e, p):
            incl, strict, blk, same, eye = masks[e]
            last = CHUNK - 1 if e == 0 else 0
            ls = slice(p * LANES, (p + 1) * LANES)
            u = base + e * pairs + p
            st = {}

            def decayed_rows():
                cs = cum[e][:, ls]
                e_neg = jnp.exp(-cs)
                st["xs"] = stack(-kk_scr[rows_c, ls] * jnp.exp(cs - lw_scr[e, rows_c, ls])).astype(BF16)
                st["bk"] = jnp.concatenate([stack(kka_scr[e, rows_c, ls] * e_neg), stack(kd_scr[e, rows_c, ls] * e_neg)],
                                           axis=0).astype(BF16)
                sol.set_rhs(u, 0, st["xs"])

            def pair_a():
                ga = _dot_nt(st["xs"], st["bk"])
                sol.stage(u, jnp.where(strict, ga[:, :UNIT], 0.0), blk, eye)
                kr_scr[u] = jnp.where(strict, ga[:, UNIT:], 0.0).astype(BF16)

            def pair_r():
                rt = r_scr[rows_c, ls] * jnp.exp(cum[e][:, ls])
                gr = _dot_nt(stack(rt), st["bk"])
                aq_ref[e, 0, c, p] = jnp.concatenate(
                    [halves(jnp.where(incl, gr[:, :UNIT], 0.0)), halves(jnp.where(incl, gr[:, UNIT:], 0.0))],
                    axis=1).astype(BF16)
                wq_ref.at[e, 0, c, p][CHUNK:, :] = rt.astype(BF16)

            def seq_operands():
                cs = cum[e][:, ls]
                cl = cum[e][last:last + 1, ls]
                e_last = jnp.exp(cl - cs)
                vv = v_scr[rows_c, ls]
                vs_scr[u] = stack(vv).astype(BF16)
                vb_ref[e, 0, c, p] = vv.astype(BF16)
                bk_ref[e, 0, c, p] = jnp.concatenate(
                    [kka_scr[e, rows_c, ls] * e_last, kd_scr[e, rows_c, ls] * e_last], axis=0).T.astype(BF16)
                gm_ref[e, 0, c, p] = jnp.broadcast_to(jnp.exp(cl), (GAM_ROWS, LANES))

            return [decayed_rows, pair_a, pair_r, seq_operands]

        items = []
        for e in range(2):
            items.append(functools.partial(cumulate, e))
            for p in range(pairs):
                items += unit_steps(e, p)
        return items

    def solve_items(base):
        def av(u):
            sol.set_rhs(u, 1, jnp.dot(kr_scr[u], vs_scr[u], preferred_element_type=F32))

        return ([functools.partial(av, u) for u in range(base, base + nu)]
                + sol.items(range(base, base + nu)))

    def finish_items(c, base):
        def unit(e, p):
            u = base + e * pairs + p
            wq_ref.at[e, 0, c, p][:CHUNK, :] = halves(sol.result(u, 0)).astype(BF16)
            u0_ref[e, 0, c, p] = halves(sol.result(u, 1))

        return [functools.partial(unit, e, p) for e in range(2) for p in range(pairs)]

    n_chunks = rows // CHUNK
    _emit(elementwise_items(0) + stage_items(0, 0))
    for c in range(n_chunks):
        base, other = (c % 2) * nu, ((c + 1) % 2) * nu
        side = finish_items(c - 1, other) if c > 0 else []
        if c + 1 < n_chunks:
            side = side + elementwise_items(c + 1) + stage_items(c + 1, other)
        _emit(solve_items(base), side)
    _emit(finish_items(n_chunks - 1, ((n_chunks - 1) % 2) * nu))


def _rwkv_prep(rkv, lora, mu_x, mu_l, wcat, bias, g2p, k_k, k_a, r_k, lora_w, lora_a, rows=256):
    b, t, width = rkv.shape
    bw = width // 3
    pairs = bw // LANES
    lw_ = lora.shape[2]
    n = t // CHUNK
    nb = rows // CHUNK
    unit = lambda r, dt, w=LANES: jax.ShapeDtypeStruct((2, b, n, pairs, r, w), dt)
    ospec = lambda r, w=LANES: pl.BlockSpec((2, 1, nb, pairs, r, w), lambda bi, i: (0, bi, i, 0, 0, 0))
    row_spec = pl.BlockSpec((1, rows, bw), lambda bi, i: (bi, i, 0))
    return pl.pallas_call(
        functools.partial(_rwkv_prep_kernel, lora_w=lora_w, lora_a=lora_a),
        grid=(b, t // rows),
        in_specs=_halo_specs(rows, width, t) + _halo_specs(rows, lw_, t) + [
            _const_spec((1, width)), _const_spec((1, lw_)), _const_spec(wcat.shape), _const_spec(bias.shape),
            _const_spec(g2p.shape), _const_spec((1, bw)), _const_spec((1, bw)), _const_spec((1, bw))],
        out_specs=[ospec(UNIT), ospec(CHUNK), ospec(CHUNK, 2 * LANES), ospec(UNIT), ospec(CHUNK),
                   ospec(GAM_ROWS), row_spec, row_spec],
        out_shape=[unit(UNIT, BF16), unit(CHUNK, F32), unit(CHUNK, BF16, 2 * LANES), unit(UNIT, BF16),
                   unit(CHUNK, BF16), unit(GAM_ROWS, F32),
                   jax.ShapeDtypeStruct((b, t, bw), F32), jax.ShapeDtypeStruct((b, t, bw), F32)],
        scratch_shapes=[pltpu.VMEM((rows + 2 * HALO, width), F32), pltpu.VMEM((rows + 2 * HALO, lw_), F32)]
        + [pltpu.VMEM((rows, bw), F32)] * 3 + [pltpu.VMEM((2, rows, bw), F32)] * 3
        + [pltpu.VMEM((4 * pairs, UNIT, UNIT), BF16), pltpu.VMEM((4 * pairs, UNIT, LANES), BF16)]
        + _UnitSolver.scratch(4 * pairs),
        compiler_params=_params(("parallel", "arbitrary"), rows * width * 4, rows * lw_ * 4, wcat.size * 2,
                                2 * pairs * nb * (UNIT * 2 + CHUNK * 4 + CHUNK * 4 + UNIT * 2 + CHUNK * 2 + UNIT * 4)
                                * LANES, 2 * rows * bw * 4,
                                scratch=(rows + 2 * HALO) * (width + lw_) * 4 + 9 * rows * bw * 4
                                + 4 * pairs * (_UnitSolver.BYTES_PER_UNIT + 2 * UNIT * LANES * 2),
                                temps=8 * rows * bw * 4 * 4),
        name="rwkv_prep",
    )(rkv, rkv, rkv, lora, lora, lora, mu_x, mu_l, wcat, bias, g2p, k_k, k_a, r_k)


def _seq_kernel(*refs, paired, nops):
    dir_ops = (refs[:nops], refs[nops:2 * nops])
    y_refs = refs[2 * nops:2 * nops + 2]
    s_ref, pq_ref = refs[2 * nops + 2:]
    nb_, nu_ = dir_ops[0][0].shape[1], dir_ops[0][0].shape[3]
    units = [(d, bi, ui) for d in range(2) for bi in range(nb_) for ui in range(nu_)]
    at = lambda ref, bi, ui: ref[0, bi, 0, ui]

    @pl.when(pl.program_id(0) == 0)
    def _():
        s_ref[...] = jnp.zeros_like(s_ref)

    m0 = _iota2((1, LANES), 1) < CHUNK
    same = (_iota2((LANES, LANES), 0) // CHUNK) == (_iota2((LANES, LANES), 1) // CHUNK)
    for j, (d, bi, ui) in enumerate(units):
        pq_ref[j] = jnp.dot(at(dir_ops[d][0], bi, ui), s_ref[j].astype(BF16), preferred_element_type=F32)
    for j, (d, bi, ui) in enumerate(units):
        wq_ref, u_ref, aq_ref, bt_ref, gm_ref = dir_ops[d][:5]
        s = s_ref[j]
        pq = pq_ref[j]
        un = at(u_ref, bi, ui) + pq[:CHUNK]
        unb = un.astype(BF16)
        if paired:
            vb = at(dir_ops[d][5], bi, ui)
            zero = jnp.zeros_like(unb)
            rhs_y = jnp.concatenate([jnp.where(m0, unb, zero), jnp.where(m0, zero, unb),
                                     jnp.where(m0, vb, zero), jnp.where(m0, zero, vb)], axis=0)
            rhs_s = jnp.concatenate([unb, vb], axis=0)
        else:
            rhs_y = jnp.concatenate([unb, unb], axis=0)
            rhs_s = unb
        y_refs[d][bi, 0, ui] = pq[CHUNK:] + jnp.dot(at(aq_ref, bi, ui), rhs_y, preferred_element_type=F32)
        upd = jnp.dot(at(bt_ref, bi, ui), rhs_s, preferred_element_type=F32)
        if paired:
            upd = jnp.where(same, upd, 0.0)
        gam = at(gm_ref, bi, ui).T[:, 0:1]
        s_ref[j] = s * jnp.broadcast_to(gam, (LANES, LANES)) + upd


def _seq(ops, paired):
    _, b, n, nu = ops[0].shape[:4]
    ub = 2 * b * nu
    fwd = lambda ni: (0, 0, ni, 0, 0, 0)
    bwd = lambda ni: (1, 0, n - 1 - ni, 0, 0, 0)
    specs = [pl.BlockSpec((1, b, 1, nu) + a.shape[4:], imap) for imap in (fwd, bwd) for a in ops]
    block_bytes = [ub * math.prod(a.shape[4:]) * a.dtype.itemsize for a in ops]
    y_shape = jax.ShapeDtypeStruct((b, n, nu, CHUNK, LANES), F32)
    return pl.pallas_call(
        functools.partial(_seq_kernel, paired=paired, nops=len(ops)),
        grid=(n,),
        in_specs=specs,
        out_specs=[pl.BlockSpec((b, 1, nu, CHUNK, LANES), lambda ni: (0, ni, 0, 0, 0)),
                   pl.BlockSpec((b, 1, nu, CHUNK, LANES), lambda ni: (0, n - 1 - ni, 0, 0, 0))],
        out_shape=[y_shape, y_shape],
        scratch_shapes=[pltpu.VMEM((ub, LANES, LANES), F32), pltpu.VMEM((ub, UNIT, LANES), F32)],
        compiler_params=_params(("arbitrary",), *block_bytes, ub * CHUNK * LANES * 4,
                                scratch=ub * (LANES + UNIT) * LANES * 4, temps=ub * UNIT * LANES * 4),
        name="seq_rwkv" if paired else "seq_delta",
    )(*ops, *ops)


def _even_out_kernel(of_ref, ob_ref, yf_ref, yb_ref, z_ref, bonus_ref, gate_ref, h_ref, an_ref, lnw_ref, lnb_ref,
                     wo_ref, o_ref):
    rows = h_ref.shape[1]
    a_heads = of_ref.shape[2]
    pairs = yf_ref.shape[2]
    z = z_ref[0]
    pieces = []
    for hh in range(a_heads):
        o = (of_ref[0, :, hh] + ob_ref[0, :, hh]).reshape(rows, LANES)
        pieces.append(_rms(o, an_ref[...]) * _silu(z[:, hh * LANES:(hh + 1) * LANES]))
    sr = _iota2((LANES, LANES), 0) // CHUNK
    sc = _iota2((LANES, LANES), 1) // CHUNK
    head_sel = jnp.where(sr == sc, 1.0, 0.0).astype(BF16)
    inv_n = 1.0 / CHUNK
    for p in range(pairs):
        ls = slice(p * LANES, (p + 1) * LANES)
        y = (yf_ref[0, :, p] + yb_ref[0, :, p]).reshape(rows, LANES)
        mu = _dot_sel_r(y, head_sel, 2) * inv_n
        yc = y - mu
        var = _dot_sel_r(yc * yc, head_sel, 2) * inv_n
        yn = yc * lax.rsqrt(var + B_GN_EPS) * lnw_ref[:, ls] + lnb_ref[:, ls]
        pieces.append((yn + bonus_ref[0][:, ls]) * gate_ref[0][:, ls])
    mix = jnp.concatenate(pieces, axis=1).astype(BF16)
    o_ref[0] = h_ref[0] + jnp.dot(mix, wo_ref[...], preferred_element_type=F32)


def _even_out(od, yr, z, bonus, gate, h3, a_norm, ln_w, ln_b, wo, rows=256):
    b, t, d = h3.shape
    nb = rows // CHUNK
    a_heads, pairs = od[0].shape[2], yr[0].shape[2]
    aw, bw = z.shape[2], bonus.shape[2]
    dspec = lambda nh: pl.BlockSpec((1, nb, nh, CHUNK, LANES), lambda bi, i: (bi, i, 0, 0, 0))
    rspec = lambda w: pl.BlockSpec((1, rows, w), lambda bi, i: (bi, i, 0))
    return pl.pallas_call(
        _even_out_kernel,
        grid=(b, t // rows),
        in_specs=[dspec(a_heads), dspec(a_heads), dspec(pairs), dspec(pairs), rspec(aw), rspec(bw),
                  rspec(bw), rspec(d), _const_spec((1, LANES)), _const_spec((1, bw)), _const_spec((1, bw)),
                  _const_spec(wo.shape)],
        out_specs=rspec(d),
        out_shape=jax.ShapeDtypeStruct((b, t, d), F32),
        compiler_params=_params(("parallel", "parallel"), 2 * rows * (aw + bw) * 4, rows * (aw + 2 * bw + 2 * d) * 4,
                                wo.size * 2, temps=6 * rows * d * 4),
        name="even_out",
    )(od[0], od[1], yr[0], yr[1], z, bonus, gate, h3, a_norm, ln_w, ln_b, wo)


def _even_layer(h3, nw, w_in, w_out, a_conv, a_log, a_dt_bias, a_norm, b_mu, b_w0, b_w2, b_a0, b_a2, b_g2,
                b_k_k, b_k_a, b_r_k, b_ln_w, b_ln_b):
    b, t, d = h3.shape
    a_heads = a_log.shape[1]
    hd = a_norm.shape[0]
    aw = a_heads * hd
    b_heads, b_hd = b_r_k.shape
    bw = b_heads * b_hd
    lora_w, lora_a, lora_g = b_w2.shape[1], b_a2.shape[1], b_g2.shape[0]
    assert b_hd == CHUNK and hd == LANES and 2 * (lora_w + lora_a) == LANES and lora_g <= LANES
    a_cols = 4 * aw + 4 * a_heads
    n_gate = 4 * a_heads
    pad_cols = lambda w, n: jnp.pad(w, ((0, 0), (0, n - w.shape[1])))
    w_bf = w_in.astype(BF16)
    wa, wb = w_bf[:, :a_cols], w_bf[:, a_cols:]
    w_all = jnp.concatenate([wa[:, :4 * aw], pad_cols(wa[:, 4 * aw:], LANES),
                             wb[:, :3 * bw], pad_cols(wb[:, 3 * bw:], 2 * LANES)], axis=1)
    splits = (3 * aw, aw, LANES, 3 * bw, 2 * LANES)
    qkv, z, ab, rkv, lora = _proj(h3.reshape(b * t, d), nw.reshape(1, d), w_all, splits)
    to3 = lambda a: a.reshape(b, t, a.shape[-1])

    alog_row = jnp.zeros((1, LANES), F32).at[0, :2 * a_heads].set(a_log.reshape(-1))
    dtb_row = jnp.zeros((1, LANES), F32).at[0, :2 * a_heads].set(a_dt_bias.reshape(-1))
    d_ops = _delta_prep(to3(qkv), to3(ab), a_conv, alog_row, dtb_row, a_heads)
    od = _seq(d_ops, paired=False)

    wcat = jnp.zeros((LANES, 4 * bw), F32)
    for e in range(2):
        wcat = wcat.at[e * lora_w:(e + 1) * lora_w, e * bw:(e + 1) * bw].set(b_w2[e])
        wcat = wcat.at[2 * lora_w + e * lora_a:2 * lora_w + (e + 1) * lora_a, (2 + e) * bw:(3 + e) * bw].set(b_a2[e])
    bias = jnp.concatenate([b_w0[0], b_w0[1], b_a0[0], b_a0[1]]).reshape(1, 4 * bw)
    g2p = jnp.zeros((LANES, bw), F32).at[:lora_g].set(b_g2).astype(BF16)
    mu_x = b_mu[:3 * bw].reshape(1, 3 * bw)
    mu_l = jnp.pad(b_mu[3 * bw:], (0, 2 * LANES - (b_mu.shape[0] - 3 * bw))).reshape(1, 2 * LANES)
    row = lambda a: a.reshape(1, bw)
    r_ops = _rwkv_prep(to3(rkv), to3(lora), mu_x, mu_l, wcat.astype(BF16), bias, g2p, row(b_k_k), row(b_k_a),
                       row(b_r_k), lora_w, lora_a)
    wq, u0, aq, bk, vb, gm, bonus, gate = r_ops
    yr = _seq([wq, u0, aq, bk, gm, vb], paired=True)

    return _even_out(od, yr, to3(z), bonus, gate, h3, a_norm.reshape(1, hd), row(b_ln_w), row(b_ln_b),
                     w_out.astype(BF16))


def _odd_layer(h3, nw, w_in, ln_w, ln_b, ws, bs, w_out):
    b, t, d = h3.shape
    groups, chunk, _ = ws.shape
    cw = w_out.shape[0]
    bs_full = jnp.repeat(bs.T, cw // groups, axis=1)
    out = _gmlp(h3.reshape(b * t, d), nw.reshape(1, d), w_in.astype(BF16), ln_w.reshape(1, cw), ln_b.reshape(1, cw),
                ws.astype(BF16), bs_full, w_out.astype(BF16))
    return out.reshape(b, t, d)


def kernel(x, p, norm_mix, norm_ffn, norm_ple, norm_final, w_in_even, w_out_even, a_conv, a_log, a_dt_bias, a_norm,
           b_mu, b_w0, b_w2, b_a0, b_a2, b_g2, b_k_k, b_k_a, b_r_k, b_ln_w, b_ln_b, w_in_odd, c_ln_w, c_ln_b, c_ws,
           c_bs, w_out_odd, w_gate, w_up, w_down, w_ple, w_ple_gate):
    b, t, d = x.shape
    depth = p.shape[0]
    h = x
    for i in range(depth):
        j = i // 2
        if i % 2 == 0:
            h = _even_layer(h, norm_mix[i], w_in_even[j], w_out_even[j], a_conv[j], a_log[j], a_dt_bias[j], a_norm[j],
                            b_mu[j], b_w0[j], b_w2[j], b_a0[j], b_a2[j], b_g2[j], b_k_k[j], b_k_a[j], b_r_k[j],
                            b_ln_w[j], b_ln_b[j])
        else:
            h = _odd_layer(h, norm_mix[i], w_in_odd[j], c_ln_w[j], c_ln_b[j], c_ws[j], c_bs[j], w_out_odd[j])
        h2 = h.reshape(b * t, d)
        h2 = _ffn_ple(h2, p[i].reshape(b * t, -1), norm_ffn[i].reshape(1, d), w_gate[i].astype(BF16),
                      w_up[i].astype(BF16), w_down[i].astype(BF16), norm_ple[i].reshape(1, d),
                      w_ple[i].astype(BF16), w_ple_gate[i].astype(BF16),
                      norm_final.reshape(1, d) if i == depth - 1 else None)
        h = h2.reshape(b, t, d)
    return h
```

```python
import functools
import math

import jax
import jax.numpy as jnp
from jax import lax
from jax.experimental import pallas as pl
from jax.experimental.pallas import tpu as pltpu

F32 = jnp.float32
BF16 = jnp.bfloat16

NORM_EPS = 1e-6
L2_EPS = 1e-6
B_GN_EPS = 64e-5
B_DECAY_SCALE = 0.606531
C_LN_EPS = 1e-5

LANES = 128
SUBLANES = 8
CHUNK = 64
UNIT = 2 * CHUNK
INV_BLOCK = 16
GAM_ROWS = SUBLANES
UNITS_IN_FLIGHT = 8
HALO = SUBLANES
VMEM_BUDGET = 56 * 1024 * 1024


def _params(sem, *block_bytes, scratch=0, temps=0, flags=None):
    need = 2 * sum(block_bytes) + scratch + temps
    return pltpu.CompilerParams(dimension_semantics=sem, vmem_limit_bytes=int(min(max(need, 16 << 20), VMEM_BUDGET)),
                                flags=flags)


def _dot(a, b):
    return jnp.dot(a.astype(BF16), b.astype(BF16), preferred_element_type=F32)


def _dot_nt(a, b):
    return lax.dot_general(a.astype(BF16), b.astype(BF16), (((1,), (1,)), ((), ())), preferred_element_type=F32)


def _split3(x):
    hi = x.astype(BF16)
    r1 = x - hi.astype(F32)
    mid = r1.astype(BF16)
    lo = (r1 - mid.astype(F32)).astype(BF16)
    return hi, mid, lo


def _dot_sel(sel, x):
    hi, mid, lo = _split3(x)
    d = lambda p: jnp.dot(sel, p, preferred_element_type=F32)
    return d(hi) + d(mid) + d(lo)


def _dot_sel_r(x, sel, pieces=3):
    d = lambda p: jnp.dot(p, sel, preferred_element_type=F32)
    return sum(d(p) for p in _split3(x)[:pieces])


def _sigmoid(x):
    return 1.0 / (1.0 + jnp.exp(-x))


def _silu(x):
    return x * _sigmoid(x)


def _softplus(x):
    return jnp.maximum(x, 0.0) + jnp.log(1.0 + jnp.exp(-jnp.abs(x)))


def _gelu_tanh(x):
    return 0.5 * x * (1.0 + jnp.tanh(math.sqrt(2.0 / math.pi) * (x + 0.044715 * (x * x * x))))


def _rms(x, w, eps=NORM_EPS):
    return x * lax.rsqrt(jnp.mean(x * x, axis=-1, keepdims=True) + eps) * w


def _iota2(shape, dim):
    return lax.broadcasted_iota(jnp.int32, shape, dim)


def _unit_masks(upper_top, upper_bot):
    r = _iota2((UNIT, UNIT), 0)
    c = _iota2((UNIT, UNIT), 1)
    same = (r // CHUNK) == (c // CHUNK)
    sign = jnp.where(r < CHUNK, -1 if upper_top else 1, -1 if upper_bot else 1)
    d = jnp.where(same, (r - c) * sign, -1)
    incl = d >= 0
    strict = d > 0
    blk = (r // INV_BLOCK) == (c // INV_BLOCK)
    eye = jnp.where(r == c, 1.0, 0.0).astype(F32)
    return incl, strict, blk, same, eye


class _UnitSolver:
    def __init__(self, refs):
        self.p, self.t, self.z, self.x, self.y = refs

    @staticmethod
    def scratch(nu):
        return [pltpu.VMEM((nu, UNIT, UNIT), BF16), pltpu.VMEM((nu, UNIT, UNIT), F32),
                pltpu.VMEM((nu, UNIT, 3 * UNIT), BF16), pltpu.VMEM((2, nu, UNIT, UNIT), BF16),
                pltpu.VMEM((nu, UNIT, 2 * UNIT), F32)]

    BYTES_PER_UNIT = UNIT * UNIT * (2 + 4 + 6 + 4 + 8)

    def stage(self, u, a, blk, eye):
        ad = jnp.where(blk, a, 0.0)
        self.p[u] = ad.astype(BF16)
        self.t[u] = eye + ad
        self.z[u, :, :UNIT] = jnp.where(blk, 0.0, a).astype(BF16)

    def set_rhs(self, u, tile, val):
        self.z[u, :, (tile + 1) * UNIT:(tile + 2) * UNIT] = val.astype(BF16)

    def result(self, u, tile):
        return self.y[u, :, tile * UNIT:(tile + 1) * UNIT]

    def items(self, units):
        assert CHUNK // INV_BLOCK == 4
        dot = lambda a, b: jnp.dot(a, b, preferred_element_type=F32)

        def square(u):
            p = self.p[u]
            self.p[u] = dot(p, p).astype(BF16)

        def grow(u):
            t = self.t[u]
            self.t[u] = t + dot(t.astype(BF16), self.p[u])

        def grow_square(u):
            t, p = self.t[u], self.p[u]
            r = dot(jnp.concatenate([t.astype(BF16), p], axis=0), p)
            self.t[u] = t + r[:UNIT]
            self.p[u] = r[UNIT:].astype(BF16)

        def apply_t(u):
            t = self.t[u].astype(BF16)
            self.x[0, u] = dot(t, self.z[u, :, :UNIT]).astype(BF16)
            self.y[u] = dot(t, self.z[u, :, UNIT:])

        def apply_x(u):
            x = self.x[0, u]
            self.x[1, u] = dot(x, x).astype(BF16)
            y = self.y[u]
            self.y[u] = y + dot(x, y.astype(BF16))

        def apply_x2(u):
            y = self.y[u]
            self.y[u] = y + dot(self.x[1, u], y.astype(BF16))

        levels = int(math.log2(INV_BLOCK)) - 1
        stages = [square] + [grow_square] * (levels - 1) + [grow, apply_t, apply_x, apply_x2]
        return [functools.partial(stage, u) for stage in stages for u in units]


def _emit(main, side=()):
    n, m = len(main), len(side)
    j = 0
    for idx, step in enumerate(main):
        step()
        while j < m and (j + 1) * n <= (idx + 1) * m:
            side[j]()
            j += 1
    for step in side[j:]:
        step()


def _fill_halo(xe_ref, x_ref, prev_ref, next_ref, i, nblk):
    rows = x_ref.shape[1]
    pf = jnp.where(i > 0, 1.0, 0.0).astype(F32)
    nf = jnp.where(i < nblk - 1, 1.0, 0.0).astype(F32)
    xe_ref[0:HALO, :] = prev_ref[0] * pf
    xe_ref[HALO:HALO + rows, :] = x_ref[0]
    xe_ref[HALO + rows:2 * HALO + rows, :] = next_ref[0] * nf


def _halo_specs(rows, width, t_len):
    per = rows // HALO
    last = t_len // HALO - 1
    return [
        pl.BlockSpec((1, rows, width), lambda b, i: (b, i, 0)),
        pl.BlockSpec((1, HALO, width), lambda b, i: (b, jnp.maximum(i * per - 1, 0), 0)),
        pl.BlockSpec((1, HALO, width), lambda b, i: (b, jnp.minimum((i + 1) * per, last), 0)),
    ]


def _const_spec(shape, single_buffer=False):
    nd = len(shape)
    if single_buffer:
        return pl.BlockSpec(shape, lambda *_: (0,) * nd, pipeline_mode=pl.Buffered(1))
    return pl.BlockSpec(shape, lambda *_: (0,) * nd)


def _proj_kernel(h_ref, nw_ref, w_ref, *o_refs, splits):
    hn = _rms(h_ref[...], nw_ref[...]).astype(BF16)
    off = 0
    for o_ref, n in zip(o_refs, splits):
        o_ref[...] = jnp.dot(hn, w_ref[:, off:off + n], preferred_element_type=F32)
        off += n


def _proj(h2, nw, w, splits, tm=512):
    m, d = h2.shape
    n = w.shape[1]
    return pl.pallas_call(
        functools.partial(_proj_kernel, splits=splits),
        grid=(m // tm,),
        in_specs=[pl.BlockSpec((tm, d), lambda i: (i, 0)), _const_spec((1, d)), _const_spec((d, n))],
        out_specs=[pl.BlockSpec((tm, s), lambda i: (i, 0)) for s in splits],
        out_shape=[jax.ShapeDtypeStruct((m, s), F32) for s in splits],
        compiler_params=_params(("parallel",), tm * d * 4, d * n * 2, tm * n * 4, temps=tm * d * 8),
        name="proj",
    )(h2, nw, w)


def _ffn_ple_kernel(h_ref, p_ref, nw_ref, wg_ref, wu_ref, wd_ref, npl_ref, wp_ref, wpg_ref, *rest, final):
    if final:
        nf_ref, o_ref = rest
    else:
        (o_ref,) = rest
    h = h_ref[...]
    hn = _rms(h, nw_ref[...]).astype(BF16)
    g = jnp.dot(hn, wg_ref[...], preferred_element_type=F32)
    u = jnp.dot(hn, wu_ref[...], preferred_element_type=F32)
    h = h + jnp.dot((_silu(g) * u).astype(BF16), wd_ref[...], preferred_element_type=F32)
    gate = _sigmoid(jnp.dot(_rms(h, npl_ref[...]).astype(BF16), wpg_ref[...], preferred_element_type=F32))
    e = jnp.dot(p_ref[...].astype(BF16), wp_ref[...], preferred_element_type=F32)
    out = h + e * gate
    if final:
        out = _rms(out, nf_ref[...])
    o_ref[...] = out


def _ffn_ple(h2, p2, nw, wg, wu, wd, npl, wp, wpg, nf=None, tm=512):
    m, d = h2.shape
    ff = wg.shape[1]
    pd = p2.shape[1]
    final = nf is not None
    ins = [h2, p2, nw, wg, wu, wd, npl, wp, wpg] + ([nf] if final else [])
    specs = [
        pl.BlockSpec((tm, d), lambda i: (i, 0)),
        pl.BlockSpec((tm, pd), lambda i: (i, 0)),
        _const_spec((1, d)),
        _const_spec((d, ff), True), _const_spec((d, ff), True), _const_spec((ff, d), True),
        _const_spec((1, d)), _const_spec((pd, d), True), _const_spec((d, d), True),
    ] + ([_const_spec((1, d))] if final else [])
    weights = (3 * d * ff + pd * d + d * d) * 2
    return pl.pallas_call(
        functools.partial(_ffn_ple_kernel, final=final),
        grid=(m // tm,),
        in_specs=specs,
        out_specs=pl.BlockSpec((tm, d), lambda i: (i, 0)),
        out_shape=jax.ShapeDtypeStruct((m, d), F32),
        compiler_params=_params(("parallel",), tm * d * 4, tm * pd * 4, tm * d * 4,
                                scratch=weights, temps=tm * ff * (4 + 4 + 2) + 3 * tm * d * 4),
        name="ffn_ple",
    )(*ins)


def _gmlp_kernel(h_ref, nw_ref, wi_ref, lnw_ref, lnb_ref, ws_ref, bs_ref, wo_ref, o_ref, *, chunk, groups):
    rows, d = h_ref.shape
    cw = wo_ref.shape[0]
    gd = cw // groups

    def chunk_steps(c):
        rs = slice(c * chunk, (c + 1) * chunk)
        st = {}

        def project():
            hn = _rms(h_ref[rs, :], nw_ref[...]).astype(BF16)
            st["uv"] = jnp.dot(hn, wi_ref[...], preferred_element_type=F32)

        def activate():
            uv = _gelu_tanh(st.pop("uv"))
            st["u"] = uv[:, :cw]
            v = uv[:, cw:]
            mu = jnp.mean(v, axis=-1, keepdims=True)
            vc = v - mu
            var = jnp.mean(vc * vc, axis=-1, keepdims=True)
            st["vn"] = (vc * lax.rsqrt(var + C_LN_EPS) * lnw_ref[...] + lnb_ref[...]).astype(BF16)

        def mix():
            vn = st.pop("vn")
            sv = jnp.concatenate(
                [jnp.dot(ws_ref[g], vn[:, g * gd:(g + 1) * gd], preferred_element_type=F32) for g in range(groups)],
                axis=1)
            st["gated"] = (st.pop("u") * (sv + bs_ref[...])).astype(BF16)

        def project_out():
            o_ref[rs, :] = h_ref[rs, :] + jnp.dot(st.pop("gated"), wo_ref[...], preferred_element_type=F32)

        return [project, activate, mix, project_out]

    steps = [chunk_steps(c) for c in range(rows // chunk)]
    depth = len(steps[0])
    for wave in range(len(steps) + depth - 1):
        for c in range(len(steps)):
            k = wave - c
            if 0 <= k < depth:
                steps[c][k]()


def _gmlp(h2, nw, wi, lnw, lnb, ws, bs_full, wo, tm=512):
    m, d = h2.shape
    groups, chunk, _ = ws.shape
    cw = wo.shape[0]
    return pl.pallas_call(
        functools.partial(_gmlp_kernel, chunk=chunk, groups=groups),
        grid=(m // tm,),
        in_specs=[pl.BlockSpec((tm, d), lambda i: (i, 0)), _const_spec((1, d)), _const_spec(wi.shape),
                  _const_spec((1, cw)), _const_spec((1, cw)), _const_spec(ws.shape), _const_spec(bs_full.shape),
                  _const_spec(wo.shape)],
        out_specs=pl.BlockSpec((tm, d), lambda i: (i, 0)),
        out_shape=jax.ShapeDtypeStruct((m, d), F32),
        compiler_params=_params(("parallel",), tm * d * 4, wi.size * 2, ws.size * 2, bs_full.size * 4, wo.size * 2,
                                tm * d * 4, temps=5 * tm * 2 * cw * 4),
        name="gmlp",
    )(h2, nw, wi, lnw, lnb, ws, bs_full, wo)


def _delta_prep_kernel(x_ref, xp_ref, xn_ref, ab_ref, cw_ref, alog_ref, dtb_ref,
                       wq_ref, u_ref, aq_ref, bt_ref, gm_ref,
                       xe_ref, q_scr, k_scr, v_scr, g_scr, b_scr, *solver_refs, heads, taps, group):
    i = pl.program_id(1)
    rows = x_ref.shape[1]
    width = x_ref.shape[2]
    aw = width // 3
    hd = aw // heads
    sol = _UnitSolver(solver_refs)
    _fill_halo(xe_ref, x_ref, xp_ref, xn_ref, i, pl.num_programs(1))

    def elementwise_items(c):
        r0 = c * CHUNK
        rs = slice(r0, r0 + CHUNK)

        def conv(cols):
            acc = jnp.zeros((CHUNK, hd), F32)
            for j in range(taps):
                s = HALO - taps // 2 + j + r0
                acc = acc + cw_ref[j:j + 1, cols] * xe_ref[s:s + CHUNK, cols]
            return _silu(acc)

        def head(hh):
            qh = conv(slice(hh * hd, (hh + 1) * hd))
            kh = conv(slice(aw + hh * hd, aw + (hh + 1) * hd))
            q_scr[hh, rs, :] = qh * lax.rsqrt(jnp.sum(qh * qh, axis=-1, keepdims=True) + L2_EPS) * (hd ** -0.5)
            k_scr[hh, rs, :] = kh * lax.rsqrt(jnp.sum(kh * kh, axis=-1, keepdims=True) + L2_EPS)
            v_scr[hh, rs, :] = conv(slice(2 * aw + hh * hd, 2 * aw + (hh + 1) * hd))

        def gates():
            ab = ab_ref[0, rs, :]
            g_scr[rs, :] = -jnp.exp(alog_ref[...]) * _softplus(ab + dtb_ref[...])
            b_scr[rs, :] = _sigmoid(ab)

        return [functools.partial(head, hh) for hh in range(heads)] + [gates]

    incl, strict, blk, _, eye = _unit_masks(False, True)
    r1 = _iota2((UNIT, 1), 0)
    top_col = r1 < CHUNK
    top_row = _iota2((1, UNIT), 1) < CHUNK
    tri_r = _iota2((UNIT, CHUNK), 0)
    tri_c = _iota2((UNIT, CHUNK), 1)
    cum_sel = jnp.where(jnp.where(tri_r < CHUNK, tri_r - tri_c, tri_c - tri_r + CHUNK) >= 0, 1.0, 0.0).astype(BF16)

    def stage_items(c, u0):
        rows_c = slice(c * CHUNK, (c + 1) * CHUNK)
        st = {}

        def cumulate():
            st["gst"] = _dot_sel(cum_sel, g_scr[rows_c, :])
            st["gst_t"] = st["gst"].T
            bch = b_scr[rows_c, :]
            st["bst"] = jnp.concatenate([bch, bch], axis=0)

        def head(hh):
            gst, gst_t, bst = st["gst"], st["gst_t"], st["bst"]
            fw, bw = hh, heads + hh
            gcs = jnp.where(top_col, gst[:, fw:fw + 1], gst[:, bw:bw + 1])
            grow = jnp.where(top_row, gst_t[fw:fw + 1, :], gst_t[bw:bw + 1, :])
            bcol = jnp.where(top_col, bst[:, 2 * heads + fw:2 * heads + fw + 1],
                             bst[:, 2 * heads + bw:2 * heads + bw + 1])
            glast = jnp.where(top_col, gst[CHUNK - 1:CHUNK, fw:fw + 1], gst[CHUNK:CHUNK + 1, bw:bw + 1])
            q = q_scr[hh, rows_c, :]
            k = k_scr[hh, rows_c, :]
            v = v_scr[hh, rows_c, :]
            qst = jnp.concatenate([q, q], axis=0)
            kst = jnp.concatenate([k, k], axis=0)
            vst = jnp.concatenate([v, v], axis=0)
            kst_b = kst.astype(BF16)
            decay = jnp.where(incl, jnp.exp(jnp.where(incl, gcs - grow, 0.0)), 0.0)
            kk = _dot_nt(kst_b, kst_b)
            m = jnp.where(strict, bcol * kk * decay, 0.0)
            sol.stage(u0 + hh, -m, blk, eye)
            eg = jnp.exp(gcs)
            sol.set_rhs(u0 + hh, 0, vst * bcol)
            sol.set_rhs(u0 + hh, 1, kst * (bcol * eg))
            qk = _dot_nt(qst, kst_b) * decay
            qd = qst * eg
            kdec = kst * jnp.exp(glast - gcs)
            gam = jnp.exp(glast)
            for dr in range(2):
                sl = slice(dr * CHUNK, (dr + 1) * CHUNK)
                wq_ref.at[dr, 0, c, hh][CHUNK:, :] = qd[sl].astype(BF16)
                aq_ref[dr, 0, c, hh] = qk[sl].astype(BF16)
                bt_ref[dr, 0, c, hh] = kdec[sl].T.astype(BF16)
                gm_ref[dr, 0, c, hh] = jnp.broadcast_to(gam[dr * CHUNK:dr * CHUNK + 1], (GAM_ROWS, LANES))

        return [cumulate] + [functools.partial(head, hh) for hh in range(heads)]

    def finish_items(c, u0):
        def head(hh):
            un, w = sol.result(u0 + hh, 0), sol.result(u0 + hh, 1)
            for dr in range(2):
                sl = slice(dr * CHUNK, (dr + 1) * CHUNK)
                wq_ref.at[dr, 0, c, hh][:CHUNK, :] = (-w[sl]).astype(BF16)
                u_ref[dr, 0, c, hh] = un[sl]

        return [functools.partial(head, hh) for hh in range(heads)]

    nu = group * heads
    chunks = lambda gi: range(gi * group, (gi + 1) * group)

    def prepare(gi, base):
        items = []
        for cc, c in enumerate(chunks(gi)):
            items += elementwise_items(c) + stage_items(c, base + cc * heads)
        return items

    def finish(gi, base):
        return [it for cc, c in enumerate(chunks(gi)) for it in finish_items(c, base + cc * heads)]

    n_groups = rows // (CHUNK * group)
    _emit(prepare(0, 0))
    for gi in range(n_groups):
        base, other = (gi % 2) * nu, ((gi + 1) % 2) * nu
        side = finish(gi - 1, other) if gi > 0 else []
        if gi + 1 < n_groups:
            side = side + prepare(gi + 1, other)
        _emit(sol.items(range(base, base + nu)), side)
    _emit(finish(n_groups - 1, ((n_groups - 1) % 2) * nu))


def _delta_prep(qkv, ab, conv_w, alog_row, dtb_row, heads, rows=512):
    b, t, width = qkv.shape
    hd = width // 3 // heads
    assert hd == LANES
    n = t // CHUNK
    nb = rows // CHUNK
    taps = conv_w.shape[0]
    cw = jnp.zeros((SUBLANES, width), F32).at[:taps].set(conv_w)
    group = UNITS_IN_FLIGHT // heads
    assert nb % group == 0
    unit = lambda r, dt, w=LANES: jax.ShapeDtypeStruct((2, b, n, heads, r, w), dt)
    ospec = lambda r, w=LANES: pl.BlockSpec((2, 1, nb, heads, r, w), lambda bi, i: (0, bi, i, 0, 0, 0))
    return pl.pallas_call(
        functools.partial(_delta_prep_kernel, heads=heads, taps=taps, group=group),
        grid=(b, t // rows),
        in_specs=_halo_specs(rows, width, t) + [
            pl.BlockSpec((1, rows, LANES), lambda bi, i: (bi, i, 0)),
            _const_spec((SUBLANES, width)), _const_spec((1, LANES)), _const_spec((1, LANES))],
        out_specs=[ospec(UNIT), ospec(CHUNK), ospec(CHUNK), ospec(UNIT, CHUNK), ospec(GAM_ROWS)],
        out_shape=[unit(UNIT, BF16), unit(CHUNK, F32), unit(CHUNK, BF16), unit(UNIT, BF16, CHUNK),
                   unit(GAM_ROWS, F32)],
        scratch_shapes=[pltpu.VMEM((rows + 2 * HALO, width), F32)]
        + [pltpu.VMEM((heads, rows, LANES), F32)] * 3 + [pltpu.VMEM((rows, LANES), F32)] * 2
        + _UnitSolver.scratch(2 * group * heads),
        compiler_params=_params(("parallel", "arbitrary"), rows * width * 4, rows * LANES * 4,
                                2 * heads * nb * (UNIT * 2 + CHUNK * 4 + CHUNK * 2 + UNIT * 2 + UNIT * 4) * LANES,
                                scratch=(2 * rows + 2 * HALO) * width * 4
                                + 2 * group * heads * _UnitSolver.BYTES_PER_UNIT, temps=4 * rows * width * 4),
        name="delta_prep",
    )(qkv, qkv, qkv, ab, cw, alog_row, dtb_row)


def _rwkv_prep_kernel(x_ref, xp_ref, xn_ref, l_ref, lp_ref, ln_ref, mux_ref, mul_ref, wcat_ref, bias_ref, g2_ref,
                      kk_w_ref, ka_ref, rk_ref,
                      wq_ref, u0_ref, aq_ref, bk_ref, vb_ref, gm_ref, bonus_ref, gate_ref,
                      xe_ref, le_ref, r_scr, v_scr, kk_scr, lw_scr, kka_scr, kd_scr, kr_scr, vs_scr,
                      *solver_refs, lora_w, lora_a):
    i = pl.program_id(1)
    nblk = pl.num_programs(1)
    sol = _UnitSolver(solver_refs)
    rows = x_ref.shape[1]
    bw = x_ref.shape[2] // 3
    pairs = bw // LANES
    _fill_halo(xe_ref, x_ref, xp_ref, xn_ref, i, nblk)
    _fill_halo(le_ref, l_ref, lp_ref, ln_ref, i, nblk)

    sr = _iota2((LANES, LANES), 0) // CHUNK
    sc = _iota2((LANES, LANES), 1) // CHUNK
    head_sel = jnp.where(sr == sc, 1.0, 0.0).astype(BF16)
    lane = _iota2((1, LANES), 1)

    def head_sum(a, pieces):
        return jnp.concatenate(
            [_dot_sel_r(a[:, p * LANES:(p + 1) * LANES], head_sel, pieces) for p in range(pairs)], axis=1)

    def elementwise_items(c):
        r0 = c * CHUNK
        rs = slice(r0, r0 + CHUNK)
        st = {}

        def shifted(e_ref, mu, cols):
            x = e_ref[HALO + r0:HALO + r0 + CHUNK, cols]
            nbr = 0.5 * (e_ref[HALO - 1 + r0:HALO - 1 + r0 + CHUNK, cols]
                         + e_ref[HALO + 1 + r0:HALO + 1 + r0 + CHUNK, cols])
            return x + (nbr - x) * mu[:, cols]

        def shift_rv():
            st["r"] = shifted(xe_ref, mux_ref, slice(0, bw))
            st["v"] = shifted(xe_ref, mux_ref, slice(2 * bw, 3 * bw))
            r_scr[rs, :] = st["r"]
            v_scr[rs, :] = st["v"]

        def shift_k():
            st["k"] = shifted(xe_ref, mux_ref, slice(bw, 2 * bw))
            kkp = st["k"] * kk_w_ref[...]
            st["kk"] = kkp * lax.rsqrt(head_sum(kkp * kkp, 1) + L2_EPS)
            kk_scr[rs, :] = st["kk"]
            st["kb"] = jnp.zeros_like(st["k"])

        def lora():
            lo = shifted(le_ref, mul_ref, slice(None))
            l01 = lo[:, :LANES]
            xin = jnp.where(lane < 2 * lora_w, jnp.tanh(l01), l01)
            st["wa"] = _dot(xin, wcat_ref[...]) + bias_ref[...]
            gate_ref[0, rs, :] = _dot(_sigmoid(lo[:, LANES:]), g2_ref[...])

        def direction(e):
            wa = st["wa"]
            a = _sigmoid(wa[:, (2 + e) * bw:(3 + e) * bw])
            lw_scr[e, rs, :] = -B_DECAY_SCALE * _sigmoid(wa[:, e * bw:(e + 1) * bw])
            kka_scr[e, rs, :] = st["kk"] * a
            kd = st["k"] * (1.0 + (a - 1.0) * ka_ref[...])
            kd_scr[e, rs, :] = kd
            st["kb"] = st["kb"] + 0.5 * kd

        def bonus():
            bonus_ref[0, rs, :] = head_sum(st["r"] * st["kb"] * rk_ref[...], 2) * st["v"]

        return [shift_rv, shift_k, lora, functools.partial(direction, 0), functools.partial(direction, 1), bonus]

    lanes2 = _iota2((1, LANES), 1)
    m0 = lanes2 < CHUNK
    tr = _iota2((CHUNK, CHUNK), 0)
    tc = _iota2((CHUNK, CHUNK), 1)
    cum_sel = (jnp.where(tc <= tr, 1.0, 0.0).astype(BF16), jnp.where(tc >= tr, 1.0, 0.0).astype(BF16))
    masks = (_unit_masks(False, False), _unit_masks(True, True))

    def stack(a):
        return jnp.concatenate([jnp.where(m0, a, 0.0), jnp.where(m0, 0.0, a)], axis=0)

    def halves(a):
        return a[:CHUNK] + a[CHUNK:]

    nu = 2 * pairs

    def stage_items(c, base):
        rows_c = slice(c * CHUNK, (c + 1) * CHUNK)
        cum = {}

        def cumulate(e):
            cum[e] = _dot_sel(cum_sel[e], lw_scr[e, rows_c, :])

        def unit_steps(e, p):
            incl, strict, blk, same, eye = masks[e]
            last = CHUNK - 1 if e == 0 else 0
            ls = slice(p * LANES, (p + 1) * LANES)
            u = base + e * pairs + p
            st = {}

            def decayed_rows():
                cs = cum[e][:, ls]
                e_neg = jnp.exp(-cs)
                st["xs"] = stack(-kk_scr[rows_c, ls] * jnp.exp(cs - lw_scr[e, rows_c, ls])).astype(BF16)
                st["bk"] = jnp.concatenate([stack(kka_scr[e, rows_c, ls] * e_neg), stack(kd_scr[e, rows_c, ls] * e_neg)],
                                           axis=0).astype(BF16)
                sol.set_rhs(u, 0, st["xs"])

            def pair_a():
                ga = _dot_nt(st["xs"], st["bk"])
                sol.stage(u, jnp.where(strict, ga[:, :UNIT], 0.0), blk, eye)
                kr_scr[u] = jnp.where(strict, ga[:, UNIT:], 0.0).astype(BF16)

            def pair_r():
                rt = r_scr[rows_c, ls] * jnp.exp(cum[e][:, ls])
                gr = _dot_nt(stack(rt), st["bk"])
                aq_ref[e, 0, c, p] = jnp.concatenate(
                    [halves(jnp.where(incl, gr[:, :UNIT], 0.0)), halves(jnp.where(incl, gr[:, UNIT:], 0.0))],
                    axis=1).astype(BF16)
                wq_ref.at[e, 0, c, p][CHUNK:, :] = rt.astype(BF16)

            def seq_operands():
                cs = cum[e][:, ls]
                cl = cum[e][last:last + 1, ls]
                e_last = jnp.exp(cl - cs)
                vv = v_scr[rows_c, ls]
                vs_scr[u] = stack(vv).astype(BF16)
                vb_ref[e, 0, c, p] = vv.astype(BF16)
                bk_ref[e, 0, c, p] = jnp.concatenate(
                    [kka_scr[e, rows_c, ls] * e_last, kd_scr[e, rows_c, ls] * e_last], axis=0).T.astype(BF16)
                gm_ref[e, 0, c, p] = jnp.broadcast_to(jnp.exp(cl), (GAM_ROWS, LANES))

            return [decayed_rows, pair_a, pair_r, seq_operands]

        items = []
        for e in range(2):
            items.append(functools.partial(cumulate, e))
            for p in range(pairs):
                items += unit_steps(e, p)
        return items

    def solve_items(base):
        def av(u):
            sol.set_rhs(u, 1, jnp.dot(kr_scr[u], vs_scr[u], preferred_element_type=F32))

        return ([functools.partial(av, u) for u in range(base, base + nu)]
                + sol.items(range(base, base + nu)))

    def finish_items(c, base):
        def unit(e, p):
            u = base + e * pairs + p
            wq_ref.at[e, 0, c, p][:CHUNK, :] = halves(sol.result(u, 0)).astype(BF16)
            u0_ref[e, 0, c, p] = halves(sol.result(u, 1))

        return [functools.partial(unit, e, p) for e in range(2) for p in range(pairs)]

    n_chunks = rows // CHUNK
    _emit(elementwise_items(0) + stage_items(0, 0))
    for c in range(n_chunks):
        base, other = (c % 2) * nu, ((c + 1) % 2) * nu
        side = finish_items(c - 1, other) if c > 0 else []
        if c + 1 < n_chunks:
            side = side + elementwise_items(c + 1) + stage_items(c + 1, other)
        _emit(solve_items(base), side)
    _emit(finish_items(n_chunks - 1, ((n_chunks - 1) % 2) * nu))


def _rwkv_prep(rkv, lora, mu_x, mu_l, wcat, bias, g2p, k_k, k_a, r_k, lora_w, lora_a, rows=512):
    b, t, width = rkv.shape
    bw = width // 3
    pairs = bw // LANES
    lw_ = lora.shape[2]
    n = t // CHUNK
    nb = rows // CHUNK
    unit = lambda r, dt, w=LANES: jax.ShapeDtypeStruct((2, b, n, pairs, r, w), dt)
    ospec = lambda r, w=LANES: pl.BlockSpec((2, 1, nb, pairs, r, w), lambda bi, i: (0, bi, i, 0, 0, 0))
    row_spec = pl.BlockSpec((1, rows, bw), lambda bi, i: (bi, i, 0))
    return pl.pallas_call(
        functools.partial(_rwkv_prep_kernel, lora_w=lora_w, lora_a=lora_a),
        grid=(b, t // rows),
        in_specs=_halo_specs(rows, width, t) + _halo_specs(rows, lw_, t) + [
            _const_spec((1, width)), _const_spec((1, lw_)), _const_spec(wcat.shape), _const_spec(bias.shape),
            _const_spec(g2p.shape), _const_spec((1, bw)), _const_spec((1, bw)), _const_spec((1, bw))],
        out_specs=[ospec(UNIT), ospec(CHUNK), ospec(CHUNK, 2 * LANES), ospec(UNIT), ospec(CHUNK),
                   ospec(GAM_ROWS), row_spec, row_spec],
        out_shape=[unit(UNIT, BF16), unit(CHUNK, F32), unit(CHUNK, BF16, 2 * LANES), unit(UNIT, BF16),
                   unit(CHUNK, BF16), unit(GAM_ROWS, F32),
                   jax.ShapeDtypeStruct((b, t, bw), F32), jax.ShapeDtypeStruct((b, t, bw), F32)],
        scratch_shapes=[pltpu.VMEM((rows + 2 * HALO, width), F32), pltpu.VMEM((rows + 2 * HALO, lw_), F32)]
        + [pltpu.VMEM((rows, bw), F32)] * 3 + [pltpu.VMEM((2, rows, bw), F32)] * 3
        + [pltpu.VMEM((4 * pairs, UNIT, UNIT), BF16), pltpu.VMEM((4 * pairs, UNIT, LANES), BF16)]
        + _UnitSolver.scratch(4 * pairs),
        compiler_params=_params(("parallel", "arbitrary"), rows * width * 4, rows * lw_ * 4, wcat.size * 2,
                                2 * pairs * nb * (UNIT * 2 + CHUNK * 4 + CHUNK * 4 + UNIT * 2 + CHUNK * 2 + UNIT * 4)
                                * LANES, 2 * rows * bw * 4,
                                scratch=(rows + 2 * HALO) * (width + lw_) * 4 + 9 * rows * bw * 4
                                + 4 * pairs * (_UnitSolver.BYTES_PER_UNIT + 2 * UNIT * LANES * 2),
                                temps=8 * rows * bw * 4 * 4),
        name="rwkv_prep",
    )(rkv, rkv, rkv, lora, lora, lora, mu_x, mu_l, wcat, bias, g2p, k_k, k_a, r_k)


def _seq_kernel(*refs, paired, nops):
    dir_ops = (refs[:nops], refs[nops:2 * nops])
    y_refs = refs[2 * nops:2 * nops + 2]
    s_ref, pq_ref = refs[2 * nops + 2:]
    nb_, nu_ = dir_ops[0][0].shape[1], dir_ops[0][0].shape[3]
    units = [(d, bi, ui) for d in range(2) for bi in range(nb_) for ui in range(nu_)]
    at = lambda ref, bi, ui: ref[0, bi, 0, ui]

    @pl.when(pl.program_id(0) == 0)
    def _():
        s_ref[...] = jnp.zeros_like(s_ref)

    m0 = _iota2((1, LANES), 1) < CHUNK
    same = (_iota2((LANES, LANES), 0) // CHUNK) == (_iota2((LANES, LANES), 1) // CHUNK)
    for j, (d, bi, ui) in enumerate(units):
        pq_ref[j] = jnp.dot(at(dir_ops[d][0], bi, ui), s_ref[j].astype(BF16), preferred_element_type=F32)
    for j, (d, bi, ui) in enumerate(units):
        wq_ref, u_ref, aq_ref, bt_ref, gm_ref = dir_ops[d][:5]
        s = s_ref[j]
        pq = pq_ref[j]
        un = at(u_ref, bi, ui) + pq[:CHUNK]
        unb = un.astype(BF16)
        if paired:
            vb = at(dir_ops[d][5], bi, ui)
            zero = jnp.zeros_like(unb)
            rhs_y = jnp.concatenate([jnp.where(m0, unb, zero), jnp.where(m0, zero, unb),
                                     jnp.where(m0, vb, zero), jnp.where(m0, zero, vb)], axis=0)
            rhs_s = jnp.concatenate([unb, vb], axis=0)
        else:
            rhs_y = jnp.concatenate([unb, unb], axis=0)
            rhs_s = unb
        y_refs[d][bi, 0, ui] = pq[CHUNK:] + jnp.dot(at(aq_ref, bi, ui), rhs_y, preferred_element_type=F32)
        upd = jnp.dot(at(bt_ref, bi, ui), rhs_s, preferred_element_type=F32)
        if paired:
            upd = jnp.where(same, upd, 0.0)
        gam = at(gm_ref, bi, ui).T[:, 0:1]
        s_ref[j] = s * jnp.broadcast_to(gam, (LANES, LANES)) + upd


def _seq(ops, paired):
    _, b, n, nu = ops[0].shape[:4]
    ub = 2 * b * nu
    fwd = lambda ni: (0, 0, ni, 0, 0, 0)
    bwd = lambda ni: (1, 0, n - 1 - ni, 0, 0, 0)
    specs = [pl.BlockSpec((1, b, 1, nu) + a.shape[4:], imap) for imap in (fwd, bwd) for a in ops]
    block_bytes = [ub * math.prod(a.shape[4:]) * a.dtype.itemsize for a in ops]
    y_shape = jax.ShapeDtypeStruct((b, n, nu, CHUNK, LANES), F32)
    return pl.pallas_call(
        functools.partial(_seq_kernel, paired=paired, nops=len(ops)),
        grid=(n,),
        in_specs=specs,
        out_specs=[pl.BlockSpec((b, 1, nu, CHUNK, LANES), lambda ni: (0, ni, 0, 0, 0)),
                   pl.BlockSpec((b, 1, nu, CHUNK, LANES), lambda ni: (0, n - 1 - ni, 0, 0, 0))],
        out_shape=[y_shape, y_shape],
        scratch_shapes=[pltpu.VMEM((ub, LANES, LANES), F32), pltpu.VMEM((ub, UNIT, LANES), F32)],
        compiler_params=_params(("arbitrary",), *block_bytes, ub * CHUNK * LANES * 4,
                                scratch=ub * (LANES + UNIT) * LANES * 4, temps=ub * UNIT * LANES * 4),
        name="seq_rwkv" if paired else "seq_delta",
    )(*ops, *ops)


def _even_out_kernel(of_ref, ob_ref, yf_ref, yb_ref, z_ref, bonus_ref, gate_ref, h_ref, an_ref, lnw_ref, lnb_ref,
                     wo_ref, o_ref):
    rows = h_ref.shape[1]
    a_heads = of_ref.shape[2]
    pairs = yf_ref.shape[2]
    z = z_ref[0]
    pieces = []
    for hh in range(a_heads):
        o = (of_ref[0, :, hh] + ob_ref[0, :, hh]).reshape(rows, LANES)
        pieces.append(_rms(o, an_ref[...]) * _silu(z[:, hh * LANES:(hh + 1) * LANES]))
    sr = _iota2((LANES, LANES), 0) // CHUNK
    sc = _iota2((LANES, LANES), 1) // CHUNK
    head_sel = jnp.where(sr == sc, 1.0, 0.0).astype(BF16)
    inv_n = 1.0 / CHUNK
    for p in range(pairs):
        ls = slice(p * LANES, (p + 1) * LANES)
        y = (yf_ref[0, :, p] + yb_ref[0, :, p]).reshape(rows, LANES)
        mu = _dot_sel_r(y, head_sel, 2) * inv_n
        yc = y - mu
        var = _dot_sel_r(yc * yc, head_sel, 2) * inv_n
        yn = yc * lax.rsqrt(var + B_GN_EPS) * lnw_ref[:, ls] + lnb_ref[:, ls]
        pieces.append((yn + bonus_ref[0][:, ls]) * gate_ref[0][:, ls])
    mix = jnp.concatenate(pieces, axis=1).astype(BF16)
    o_ref[0] = h_ref[0] + jnp.dot(mix, wo_ref[...], preferred_element_type=F32)


def _even_out(od, yr, z, bonus, gate, h3, a_norm, ln_w, ln_b, wo, rows=256):
    b, t, d = h3.shape
    nb = rows // CHUNK
    a_heads, pairs = od[0].shape[2], yr[0].shape[2]
    aw, bw = z.shape[2], bonus.shape[2]
    dspec = lambda nh: pl.BlockSpec((1, nb, nh, CHUNK, LANES), lambda bi, i: (bi, i, 0, 0, 0))
    rspec = lambda w: pl.BlockSpec((1, rows, w), lambda bi, i: (bi, i, 0))
    return pl.pallas_call(
        _even_out_kernel,
        grid=(b, t // rows),
        in_specs=[dspec(a_heads), dspec(a_heads), dspec(pairs), dspec(pairs), rspec(aw), rspec(bw),
                  rspec(bw), rspec(d), _const_spec((1, LANES)), _const_spec((1, bw)), _const_spec((1, bw)),
                  _const_spec(wo.shape)],
        out_specs=rspec(d),
        out_shape=jax.ShapeDtypeStruct((b, t, d), F32),
        compiler_params=_params(("parallel", "parallel"), 2 * rows * (aw + bw) * 4, rows * (aw + 2 * bw + 2 * d) * 4,
                                wo.size * 2, temps=6 * rows * d * 4),
        name="even_out",
    )(od[0], od[1], yr[0], yr[1], z, bonus, gate, h3, a_norm, ln_w, ln_b, wo)


def _even_layer(h3, nw, w_in, w_out, a_conv, a_log, a_dt_bias, a_norm, b_mu, b_w0, b_w2, b_a0, b_a2, b_g2,
                b_k_k, b_k_a, b_r_k, b_ln_w, b_ln_b):
    b, t, d = h3.shape
    a_heads = a_log.shape[1]
    hd = a_norm.shape[0]
    aw = a_heads * hd
    b_heads, b_hd = b_r_k.shape
    bw = b_heads * b_hd
    lora_w, lora_a, lora_g = b_w2.shape[1], b_a2.shape[1], b_g2.shape[0]
    assert b_hd == CHUNK and hd == LANES and 2 * (lora_w + lora_a) == LANES and lora_g <= LANES
    a_cols = 4 * aw + 4 * a_heads
    n_gate = 4 * a_heads
    pad_cols = lambda w, n: jnp.pad(w, ((0, 0), (0, n - w.shape[1])))
    w_bf = w_in.astype(BF16)
    wa, wb = w_bf[:, :a_cols], w_bf[:, a_cols:]
    w_all = jnp.concatenate([wa[:, :4 * aw], pad_cols(wa[:, 4 * aw:], LANES),
                             wb[:, :3 * bw], pad_cols(wb[:, 3 * bw:], 2 * LANES)], axis=1)
    splits = (3 * aw, aw, LANES, 3 * bw, 2 * LANES)
    qkv, z, ab, rkv, lora = _proj(h3.reshape(b * t, d), nw.reshape(1, d), w_all, splits)
    to3 = lambda a: a.reshape(b, t, a.shape[-1])

    alog_row = jnp.zeros((1, LANES), F32).at[0, :2 * a_heads].set(a_log.reshape(-1))
    dtb_row = jnp.zeros((1, LANES), F32).at[0, :2 * a_heads].set(a_dt_bias.reshape(-1))
    d_ops = _delta_prep(to3(qkv), to3(ab), a_conv, alog_row, dtb_row, a_heads)
    od = _seq(d_ops, paired=False)

    wcat = jnp.zeros((LANES, 4 * bw), F32)
    for e in range(2):
        wcat = wcat.at[e * lora_w:(e + 1) * lora_w, e * bw:(e + 1) * bw].set(b_w2[e])
        wcat = wcat.at[2 * lora_w + e * lora_a:2 * lora_w + (e + 1) * lora_a, (2 + e) * bw:(3 + e) * bw].set(b_a2[e])
    bias = jnp.concatenate([b_w0[0], b_w0[1], b_a0[0], b_a0[1]]).reshape(1, 4 * bw)
    g2p = jnp.zeros((LANES, bw), F32).at[:lora_g].set(b_g2).astype(BF16)
    mu_x = b_mu[:3 * bw].reshape(1, 3 * bw)
    mu_l = jnp.pad(b_mu[3 * bw:], (0, 2 * LANES - (b_mu.shape[0] - 3 * bw))).reshape(1, 2 * LANES)
    row = lambda a: a.reshape(1, bw)
    r_ops = _rwkv_prep(to3(rkv), to3(lora), mu_x, mu_l, wcat.astype(BF16), bias, g2p, row(b_k_k), row(b_k_a),
                       row(b_r_k), lora_w, lora_a)
    wq, u0, aq, bk, vb, gm, bonus, gate = r_ops
    yr = _seq([wq, u0, aq, bk, gm, vb], paired=True)

    return _even_out(od, yr, to3(z), bonus, gate, h3, a_norm.reshape(1, hd), row(b_ln_w), row(b_ln_b),
                     w_out.astype(BF16))


def _odd_layer(h3, nw, w_in, ln_w, ln_b, ws, bs, w_out):
    b, t, d = h3.shape
    groups, chunk, _ = ws.shape
    cw = w_out.shape[0]
    bs_full = jnp.repeat(bs.T, cw // groups, axis=1)
    out = _gmlp(h3.reshape(b * t, d), nw.reshape(1, d), w_in.astype(BF16), ln_w.reshape(1, cw), ln_b.reshape(1, cw),
                ws.astype(BF16), bs_full, w_out.astype(BF16))
    return out.reshape(b, t, d)


def kernel(x, p, norm_mix, norm_ffn, norm_ple, norm_final, w_in_even, w_out_even, a_conv, a_log, a_dt_bias, a_norm,
           b_mu, b_w0, b_w2, b_a0, b_a2, b_g2, b_k_k, b_k_a, b_r_k, b_ln_w, b_ln_b, w_in_odd, c_ln_w, c_ln_b, c_ws,
           c_bs, w_out_odd, w_gate, w_up, w_down, w_ple, w_ple_gate):
    b, t, d = x.shape
    depth = p.shape[0]
    h = x
    for i in range(depth):
        j = i // 2
        if i % 2 == 0:
            h = _even_layer(h, norm_mix[i], w_in_even[j], w_out_even[j], a_conv[j], a_log[j], a_dt_bias[j], a_norm[j],
                            b_mu[j], b_w0[j], b_w2[j], b_a0[j], b_a2[j], b_g2[j], b_k_k[j], b_k_a[j], b_r_k[j],
                            b_ln_w[j], b_ln_b[j])
        else:
            h = _odd_layer(h, norm_mix[i], w_in_odd[j], c_ln_w[j], c_ln_b[j], c_ws[j], c_bs[j], w_out_odd[j])
        h2 = h.reshape(b * t, d)
        h2 = _ffn_ple(h2, p[i].reshape(b * t, -1), norm_ffn[i].reshape(1, d), w_gate[i].astype(BF16),
                      w_up[i].astype(BF16), w_down[i].astype(BF16), norm_ple[i].reshape(1, d),
                      w_ple[i].astype(BF16), w_ple_gate[i].astype(BF16),
                      norm_final.reshape(1, d) if i == depth - 1 else None)
        h = h2.reshape(b, t, d)
    return h
```

```python
import functools
import math

import jax
import jax.numpy as jnp
from jax import lax
from jax.experimental import pallas as pl
from jax.experimental.pallas import tpu as pltpu

F32 = jnp.float32
BF16 = jnp.bfloat16

NORM_EPS = 1e-6
L2_EPS = 1e-6
B_GN_EPS = 64e-5
B_DECAY_SCALE = 0.606531
C_LN_EPS = 1e-5

LANES = 128
SUBLANES = 8
CHUNK = 64
UNIT = 2 * CHUNK
INV_BLOCK = 16
GAM_ROWS = SUBLANES
UNITS_IN_FLIGHT = 8
HALO = SUBLANES
VMEM_BUDGET = 56 * 1024 * 1024


def _params(sem, *block_bytes, scratch=0, temps=0, flags=None):
    need = 2 * sum(block_bytes) + scratch + temps
    return pltpu.CompilerParams(dimension_semantics=sem, vmem_limit_bytes=int(min(max(need, 16 << 20), VMEM_BUDGET)),
                                flags=flags)


def _dot(a, b):
    return jnp.dot(a.astype(BF16), b.astype(BF16), preferred_element_type=F32)


def _dot_nt(a, b):
    return lax.dot_general(a.astype(BF16), b.astype(BF16), (((1,), (1,)), ((), ())), preferred_element_type=F32)


def _split3(x):
    hi = x.astype(BF16)
    r1 = x - hi.astype(F32)
    mid = r1.astype(BF16)
    lo = (r1 - mid.astype(F32)).astype(BF16)
    return hi, mid, lo


def _dot_sel(sel, x):
    hi, mid, lo = _split3(x)
    d = lambda p: jnp.dot(sel, p, preferred_element_type=F32)
    return d(hi) + d(mid) + d(lo)


def _dot_sel_r(x, sel, pieces=3):
    d = lambda p: jnp.dot(p, sel, preferred_element_type=F32)
    return sum(d(p) for p in _split3(x)[:pieces])


def _sigmoid(x):
    return 1.0 / (1.0 + jnp.exp(-x))


def _silu(x):
    return x * _sigmoid(x)


def _softplus(x):
    return jnp.maximum(x, 0.0) + jnp.log(1.0 + jnp.exp(-jnp.abs(x)))


def _gelu_tanh(x):
    return 0.5 * x * (1.0 + jnp.tanh(math.sqrt(2.0 / math.pi) * (x + 0.044715 * (x * x * x))))


def _rms(x, w, eps=NORM_EPS):
    return x * lax.rsqrt(jnp.mean(x * x, axis=-1, keepdims=True) + eps) * w


def _iota2(shape, dim):
    return lax.broadcasted_iota(jnp.int32, shape, dim)


def _unit_masks(upper_top, upper_bot):
    r = _iota2((UNIT, UNIT), 0)
    c = _iota2((UNIT, UNIT), 1)
    same = (r // CHUNK) == (c // CHUNK)
    sign = jnp.where(r < CHUNK, -1 if upper_top else 1, -1 if upper_bot else 1)
    d = jnp.where(same, (r - c) * sign, -1)
    incl = d >= 0
    strict = d > 0
    blk = (r // INV_BLOCK) == (c // INV_BLOCK)
    eye = jnp.where(r == c, 1.0, 0.0).astype(F32)
    return incl, strict, blk, same, eye


class _UnitSolver:
    def __init__(self, refs):
        self.p, self.t, self.z, self.x, self.y = refs

    @staticmethod
    def scratch(nu):
        return [pltpu.VMEM((nu, UNIT, UNIT), BF16), pltpu.VMEM((nu, UNIT, UNIT), F32),
                pltpu.VMEM((nu, UNIT, 3 * UNIT), BF16), pltpu.VMEM((2, nu, UNIT, UNIT), BF16),
                pltpu.VMEM((nu, UNIT, 2 * UNIT), F32)]

    BYTES_PER_UNIT = UNIT * UNIT * (2 + 4 + 6 + 4 + 8)

    def stage(self, u, a, blk, eye):
        ad = jnp.where(blk, a, 0.0)
        self.p[u] = ad.astype(BF16)
        self.t[u] = eye + ad
        self.z[u, :, :UNIT] = jnp.where(blk, 0.0, a).astype(BF16)

    def set_rhs(self, u, tile, val):
        self.z[u, :, (tile + 1) * UNIT:(tile + 2) * UNIT] = val.astype(BF16)

    def result(self, u, tile):
        return self.y[u, :, tile * UNIT:(tile + 1) * UNIT]

    def items(self, units):
        assert CHUNK // INV_BLOCK == 4
        dot = lambda a, b: jnp.dot(a, b, preferred_element_type=F32)

        def square(u):
            p = self.p[u]
            self.p[u] = dot(p, p).astype(BF16)

        def grow(u):
            t = self.t[u]
            self.t[u] = t + dot(t.astype(BF16), self.p[u])

        def grow_square(u):
            t, p = self.t[u], self.p[u]
            r = dot(jnp.concatenate([t.astype(BF16), p], axis=0), p)
            self.t[u] = t + r[:UNIT]
            self.p[u] = r[UNIT:].astype(BF16)

        def apply_t(u):
            t = self.t[u].astype(BF16)
            self.x[0, u] = dot(t, self.z[u, :, :UNIT]).astype(BF16)
            self.y[u] = dot(t, self.z[u, :, UNIT:])

        def apply_x(u):
            x = self.x[0, u]
            self.x[1, u] = dot(x, x).astype(BF16)
            y = self.y[u]
            self.y[u] = y + dot(x, y.astype(BF16))

        def apply_x2(u):
            y = self.y[u]
            self.y[u] = y + dot(self.x[1, u], y.astype(BF16))

        levels = int(math.log2(INV_BLOCK)) - 1
        stages = [square] + [grow_square] * (levels - 1) + [grow, apply_t, apply_x, apply_x2]
        return [functools.partial(stage, u) for stage in stages for u in units]


def _emit(main, side=()):
    n, m = len(main), len(side)
    j = 0
    for idx, step in enumerate(main):
        step()
        while j < m and (j + 1) * n <= (idx + 1) * m:
            side[j]()
            j += 1
    for step in side[j:]:
        step()


def _fill_halo(xe_ref, x_ref, prev_ref, next_ref, i, nblk):
    rows = x_ref.shape[1]
    pf = jnp.where(i > 0, 1.0, 0.0).astype(F32)
    nf = jnp.where(i < nblk - 1, 1.0, 0.0).astype(F32)
    xe_ref[0:HALO, :] = prev_ref[0] * pf
    xe_ref[HALO:HALO + rows, :] = x_ref[0]
    xe_ref[HALO + rows:2 * HALO + rows, :] = next_ref[0] * nf


def _halo_specs(rows, width, t_len):
    per = rows // HALO
    last = t_len // HALO - 1
    return [
        pl.BlockSpec((1, rows, width), lambda b, i: (b, i, 0)),
        pl.BlockSpec((1, HALO, width), lambda b, i: (b, jnp.maximum(i * per - 1, 0), 0)),
        pl.BlockSpec((1, HALO, width), lambda b, i: (b, jnp.minimum((i + 1) * per, last), 0)),
    ]


def _const_spec(shape, single_buffer=False):
    nd = len(shape)
    if single_buffer:
        return pl.BlockSpec(shape, lambda *_: (0,) * nd, pipeline_mode=pl.Buffered(1))
    return pl.BlockSpec(shape, lambda *_: (0,) * nd)


def _proj_kernel(h_ref, nw_ref, w_ref, *o_refs, splits):
    hn = _rms(h_ref[...], nw_ref[...]).astype(BF16)
    off = 0
    for o_ref, n in zip(o_refs, splits):
        o_ref[...] = jnp.dot(hn, w_ref[:, off:off + n], preferred_element_type=F32)
        off += n


def _proj(h2, nw, w, splits, tm=512):
    m, d = h2.shape
    n = w.shape[1]
    return pl.pallas_call(
        functools.partial(_proj_kernel, splits=splits),
        grid=(m // tm,),
        in_specs=[pl.BlockSpec((tm, d), lambda i: (i, 0)), _const_spec((1, d)), _const_spec((d, n))],
        out_specs=[pl.BlockSpec((tm, s), lambda i: (i, 0)) for s in splits],
        out_shape=[jax.ShapeDtypeStruct((m, s), F32) for s in splits],
        compiler_params=_params(("parallel",), tm * d * 4, d * n * 2, tm * n * 4, temps=tm * d * 8),
        name="proj",
    )(h2, nw, w)


def _ffn_ple_kernel(h_ref, p_ref, nw_ref, wg_ref, wu_ref, wd_ref, npl_ref, wp_ref, wpg_ref, *rest, final):
    if final:
        nf_ref, o_ref = rest
    else:
        nf_ref, (o_ref,) = None, rest
    o_ref[...] = _ffn_ple_math(h_ref[...], p_ref[...], nw_ref, wg_ref, wu_ref, wd_ref, npl_ref, wp_ref, wpg_ref, nf_ref)


def _ffn_ple_math(h, p, nw_ref, wg_ref, wu_ref, wd_ref, npl_ref, wp_ref, wpg_ref, nf_ref):
    hn = _rms(h, nw_ref[...]).astype(BF16)
    g = jnp.dot(hn, wg_ref[...], preferred_element_type=F32)
    u = jnp.dot(hn, wu_ref[...], preferred_element_type=F32)
    h = h + jnp.dot((_silu(g) * u).astype(BF16), wd_ref[...], preferred_element_type=F32)
    gate = _sigmoid(jnp.dot(_rms(h, npl_ref[...]).astype(BF16), wpg_ref[...], preferred_element_type=F32))
    e = jnp.dot(p.astype(BF16), wp_ref[...], preferred_element_type=F32)
    out = h + e * gate
    if nf_ref is not None:
        out = _rms(out, nf_ref[...])
    return out


def _ffn_ple(h2, p2, nw, wg, wu, wd, npl, wp, wpg, nf=None, tm=512):
    m, d = h2.shape
    ff = wg.shape[1]
    pd = p2.shape[1]
    final = nf is not None
    ins = [h2, p2, nw, wg, wu, wd, npl, wp, wpg] + ([nf] if final else [])
    specs = [
        pl.BlockSpec((tm, d), lambda i: (i, 0)),
        pl.BlockSpec((tm, pd), lambda i: (i, 0)),
        _const_spec((1, d)),
        _const_spec((d, ff), True), _const_spec((d, ff), True), _const_spec((ff, d), True),
        _const_spec((1, d)), _const_spec((pd, d), True), _const_spec((d, d), True),
    ] + ([_const_spec((1, d))] if final else [])
    weights = (3 * d * ff + pd * d + d * d) * 2
    return pl.pallas_call(
        functools.partial(_ffn_ple_kernel, final=final),
        grid=(m // tm,),
        in_specs=specs,
        out_specs=pl.BlockSpec((tm, d), lambda i: (i, 0)),
        out_shape=jax.ShapeDtypeStruct((m, d), F32),
        compiler_params=_params(("parallel",), tm * d * 4, tm * pd * 4, tm * d * 4,
                                scratch=weights, temps=tm * ff * (4 + 4 + 2) + 3 * tm * d * 4),
        name="ffn_ple",
    )(*ins)


def _gmlp_kernel(h_ref, nw_ref, wi_ref, lnw_ref, lnb_ref, ws_ref, bs_ref, wo_ref, o_ref, *, chunk, groups):
    rows, d = h_ref.shape
    cw = wo_ref.shape[0]
    gd = cw // groups

    def chunk_steps(c):
        rs = slice(c * chunk, (c + 1) * chunk)
        st = {}

        def project():
            hn = _rms(h_ref[rs, :], nw_ref[...]).astype(BF16)
            st["uv"] = jnp.dot(hn, wi_ref[...], preferred_element_type=F32)

        def activate():
            uv = _gelu_tanh(st.pop("uv"))
            st["u"] = uv[:, :cw]
            v = uv[:, cw:]
            mu = jnp.mean(v, axis=-1, keepdims=True)
            vc = v - mu
            var = jnp.mean(vc * vc, axis=-1, keepdims=True)
            st["vn"] = (vc * lax.rsqrt(var + C_LN_EPS) * lnw_ref[...] + lnb_ref[...]).astype(BF16)

        def mix():
            vn = st.pop("vn")
            sv = jnp.concatenate(
                [jnp.dot(ws_ref[g], vn[:, g * gd:(g + 1) * gd], preferred_element_type=F32) for g in range(groups)],
                axis=1)
            st["gated"] = (st.pop("u") * (sv + bs_ref[...])).astype(BF16)

        def project_out():
            o_ref[rs, :] = h_ref[rs, :] + jnp.dot(st.pop("gated"), wo_ref[...], preferred_element_type=F32)

        return [project, activate, mix, project_out]

    steps = [chunk_steps(c) for c in range(rows // chunk)]
    depth = len(steps[0])
    for wave in range(len(steps) + depth - 1):
        for c in range(len(steps)):
            k = wave - c
            if 0 <= k < depth:
                steps[c][k]()


def _gmlp(h2, nw, wi, lnw, lnb, ws, bs_full, wo, tm=512):
    m, d = h2.shape
    groups, chunk, _ = ws.shape
    cw = wo.shape[0]
    return pl.pallas_call(
        functools.partial(_gmlp_kernel, chunk=chunk, groups=groups),
        grid=(m // tm,),
        in_specs=[pl.BlockSpec((tm, d), lambda i: (i, 0)), _const_spec((1, d)), _const_spec(wi.shape),
                  _const_spec((1, cw)), _const_spec((1, cw)), _const_spec(ws.shape), _const_spec(bs_full.shape),
                  _const_spec(wo.shape)],
        out_specs=pl.BlockSpec((tm, d), lambda i: (i, 0)),
        out_shape=jax.ShapeDtypeStruct((m, d), F32),
        compiler_params=_params(("parallel",), tm * d * 4, wi.size * 2, ws.size * 2, bs_full.size * 4, wo.size * 2,
                                tm * d * 4, temps=5 * tm * 2 * cw * 4),
        name="gmlp",
    )(h2, nw, wi, lnw, lnb, ws, bs_full, wo)


def _delta_prep_kernel(x_ref, xp_ref, xn_ref, ab_ref, cw_ref, alog_ref, dtb_ref,
                       wq_ref, u_ref, aq_ref, bt_ref, gm_ref,
                       xe_ref, q_scr, k_scr, v_scr, g_scr, b_scr, *solver_refs, heads, taps, group):
    i = pl.program_id(1)
    rows = x_ref.shape[1]
    width = x_ref.shape[2]
    aw = width // 3
    hd = aw // heads
    sol = _UnitSolver(solver_refs)
    _fill_halo(xe_ref, x_ref, xp_ref, xn_ref, i, pl.num_programs(1))

    def elementwise_items(c):
        r0 = c * CHUNK
        rs = slice(r0, r0 + CHUNK)

        def conv(cols):
            acc = jnp.zeros((CHUNK, hd), F32)
            for j in range(taps):
                s = HALO - taps // 2 + j + r0
                acc = acc + cw_ref[j:j + 1, cols] * xe_ref[s:s + CHUNK, cols]
            return _silu(acc)

        def head(hh):
            qh = conv(slice(hh * hd, (hh + 1) * hd))
            kh = conv(slice(aw + hh * hd, aw + (hh + 1) * hd))
            q_scr[hh, rs, :] = qh * lax.rsqrt(jnp.sum(qh * qh, axis=-1, keepdims=True) + L2_EPS) * (hd ** -0.5)
            k_scr[hh, rs, :] = kh * lax.rsqrt(jnp.sum(kh * kh, axis=-1, keepdims=True) + L2_EPS)
            v_scr[hh, rs, :] = conv(slice(2 * aw + hh * hd, 2 * aw + (hh + 1) * hd))

        def gates():
            ab = ab_ref[0, rs, :]
            g_scr[rs, :] = -jnp.exp(alog_ref[...]) * _softplus(ab + dtb_ref[...])
            b_scr[rs, :] = _sigmoid(ab)

        return [functools.partial(head, hh) for hh in range(heads)] + [gates]

    incl, strict, blk, _, eye = _unit_masks(False, True)
    r1 = _iota2((UNIT, 1), 0)
    top_col = r1 < CHUNK
    top_row = _iota2((1, UNIT), 1) < CHUNK
    tri_r = _iota2((UNIT, CHUNK), 0)
    tri_c = _iota2((UNIT, CHUNK), 1)
    cum_sel = jnp.where(jnp.where(tri_r < CHUNK, tri_r - tri_c, tri_c - tri_r + CHUNK) >= 0, 1.0, 0.0).astype(BF16)

    def stage_items(c, u0):
        rows_c = slice(c * CHUNK, (c + 1) * CHUNK)
        st = {}

        def cumulate():
            st["gst"] = _dot_sel(cum_sel, g_scr[rows_c, :])
            st["gst_t"] = st["gst"].T
            bch = b_scr[rows_c, :]
            st["bst"] = jnp.concatenate([bch, bch], axis=0)

        def head(hh):
            gst, gst_t, bst = st["gst"], st["gst_t"], st["bst"]
            fw, bw = hh, heads + hh
            gcs = jnp.where(top_col, gst[:, fw:fw + 1], gst[:, bw:bw + 1])
            grow = jnp.where(top_row, gst_t[fw:fw + 1, :], gst_t[bw:bw + 1, :])
            bcol = jnp.where(top_col, bst[:, 2 * heads + fw:2 * heads + fw + 1],
                             bst[:, 2 * heads + bw:2 * heads + bw + 1])
            glast = jnp.where(top_col, gst[CHUNK - 1:CHUNK, fw:fw + 1], gst[CHUNK:CHUNK + 1, bw:bw + 1])
            q = q_scr[hh, rows_c, :]
            k = k_scr[hh, rows_c, :]
            v = v_scr[hh, rows_c, :]
            qst = jnp.concatenate([q, q], axis=0)
            kst = jnp.concatenate([k, k], axis=0)
            vst = jnp.concatenate([v, v], axis=0)
            kst_b = kst.astype(BF16)
            decay = jnp.where(incl, jnp.exp(jnp.where(incl, gcs - grow, 0.0)), 0.0)
            kk = _dot_nt(kst_b, kst_b)
            m = jnp.where(strict, bcol * kk * decay, 0.0)
            sol.stage(u0 + hh, -m, blk, eye)
            eg = jnp.exp(gcs)
            sol.set_rhs(u0 + hh, 0, vst * bcol)
            sol.set_rhs(u0 + hh, 1, kst * (bcol * eg))
            qk = _dot_nt(qst, kst_b) * decay
            qd = qst * eg
            kdec = kst * jnp.exp(glast - gcs)
            gam = jnp.exp(glast)
            for dr in range(2):
                sl = slice(dr * CHUNK, (dr + 1) * CHUNK)
                wq_ref.at[dr, 0, c, hh][CHUNK:, :] = qd[sl].astype(BF16)
                aq_ref[dr, 0, c, hh] = qk[sl].astype(BF16)
                bt_ref[dr, 0, c, hh] = kdec[sl].T.astype(BF16)
                gm_ref[dr, 0, c, hh] = jnp.broadcast_to(gam[dr * CHUNK:dr * CHUNK + 1], (GAM_ROWS, LANES))

        return [cumulate] + [functools.partial(head, hh) for hh in range(heads)]

    def finish_items(c, u0):
        def head(hh):
            un, w = sol.result(u0 + hh, 0), sol.result(u0 + hh, 1)
            for dr in range(2):
                sl = slice(dr * CHUNK, (dr + 1) * CHUNK)
                wq_ref.at[dr, 0, c, hh][:CHUNK, :] = (-w[sl]).astype(BF16)
                u_ref[dr, 0, c, hh] = un[sl]

        return [functools.partial(head, hh) for hh in range(heads)]

    nu = group * heads
    chunks = lambda gi: range(gi * group, (gi + 1) * group)

    def prepare(gi, base):
        items = []
        for cc, c in enumerate(chunks(gi)):
            items += elementwise_items(c) + stage_items(c, base + cc * heads)
        return items

    def finish(gi, base):
        return [it for cc, c in enumerate(chunks(gi)) for it in finish_items(c, base + cc * heads)]

    n_groups = rows // (CHUNK * group)
    _emit(prepare(0, 0))
    for gi in range(n_groups):
        base, other = (gi % 2) * nu, ((gi + 1) % 2) * nu
        side = finish(gi - 1, other) if gi > 0 else []
        if gi + 1 < n_groups:
            side = side + prepare(gi + 1, other)
        _emit(sol.items(range(base, base + nu)), side)
    _emit(finish(n_groups - 1, ((n_groups - 1) % 2) * nu))


def _delta_prep(qkv, ab, conv_w, alog_row, dtb_row, heads, rows=512):
    b, t, width = qkv.shape
    hd = width // 3 // heads
    assert hd == LANES
    n = t // CHUNK
    nb = rows // CHUNK
    taps = conv_w.shape[0]
    cw = jnp.zeros((SUBLANES, width), F32).at[:taps].set(conv_w)
    group = UNITS_IN_FLIGHT // heads
    assert nb % group == 0
    unit = lambda r, dt, w=LANES: jax.ShapeDtypeStruct((2, b, n, heads, r, w), dt)
    ospec = lambda r, w=LANES: pl.BlockSpec((2, 1, nb, heads, r, w), lambda bi, i: (0, bi, i, 0, 0, 0))
    return pl.pallas_call(
        functools.partial(_delta_prep_kernel, heads=heads, taps=taps, group=group),
        grid=(b, t // rows),
        in_specs=_halo_specs(rows, width, t) + [
            pl.BlockSpec((1, rows, LANES), lambda bi, i: (bi, i, 0)),
            _const_spec((SUBLANES, width)), _const_spec((1, LANES)), _const_spec((1, LANES))],
        out_specs=[ospec(UNIT), ospec(CHUNK), ospec(CHUNK), ospec(UNIT, CHUNK), ospec(GAM_ROWS)],
        out_shape=[unit(UNIT, BF16), unit(CHUNK, F32), unit(CHUNK, BF16), unit(UNIT, BF16, CHUNK),
                   unit(GAM_ROWS, F32)],
        scratch_shapes=[pltpu.VMEM((rows + 2 * HALO, width), F32)]
        + [pltpu.VMEM((heads, rows, LANES), F32)] * 3 + [pltpu.VMEM((rows, LANES), F32)] * 2
        + _UnitSolver.scratch(2 * group * heads),
        compiler_params=_params(("parallel", "arbitrary"), rows * width * 4, rows * LANES * 4,
                                2 * heads * nb * (UNIT * 2 + CHUNK * 4 + CHUNK * 2 + UNIT * 2 + UNIT * 4) * LANES,
                                scratch=(2 * rows + 2 * HALO) * width * 4
                                + 2 * group * heads * _UnitSolver.BYTES_PER_UNIT, temps=4 * rows * width * 4),
        name="delta_prep",
    )(qkv, qkv, qkv, ab, cw, alog_row, dtb_row)


def _rwkv_prep_kernel(x_ref, xp_ref, xn_ref, l_ref, lp_ref, ln_ref, mux_ref, mul_ref, wcat_ref, bias_ref, g2_ref,
                      kk_w_ref, ka_ref, rk_ref,
                      wq_ref, u0_ref, aq_ref, bk_ref, vb_ref, gm_ref, bonus_ref, gate_ref,
                      xe_ref, le_ref, r_scr, v_scr, kk_scr, lw_scr, kka_scr, kd_scr, kr_scr, vs_scr,
                      *solver_refs, lora_w, lora_a):
    i = pl.program_id(1)
    nblk = pl.num_programs(1)
    sol = _UnitSolver(solver_refs)
    rows = x_ref.shape[1]
    bw = x_ref.shape[2] // 3
    pairs = bw // LANES
    _fill_halo(xe_ref, x_ref, xp_ref, xn_ref, i, nblk)
    _fill_halo(le_ref, l_ref, lp_ref, ln_ref, i, nblk)

    sr = _iota2((LANES, LANES), 0) // CHUNK
    sc = _iota2((LANES, LANES), 1) // CHUNK
    head_sel = jnp.where(sr == sc, 1.0, 0.0).astype(BF16)
    lane = _iota2((1, LANES), 1)

    def head_sum(a, pieces):
        return jnp.concatenate(
            [_dot_sel_r(a[:, p * LANES:(p + 1) * LANES], head_sel, pieces) for p in range(pairs)], axis=1)

    def elementwise_items(c):
        r0 = c * CHUNK
        rs = slice(r0, r0 + CHUNK)
        st = {}

        def shifted(e_ref, mu, cols):
            x = e_ref[HALO + r0:HALO + r0 + CHUNK, cols]
            nbr = 0.5 * (e_ref[HALO - 1 + r0:HALO - 1 + r0 + CHUNK, cols]
                         + e_ref[HALO + 1 + r0:HALO + 1 + r0 + CHUNK, cols])
            return x + (nbr - x) * mu[:, cols]

        def shift_rv():
            st["r"] = shifted(xe_ref, mux_ref, slice(0, bw))
            st["v"] = shifted(xe_ref, mux_ref, slice(2 * bw, 3 * bw))
            r_scr[rs, :] = st["r"]
            v_scr[rs, :] = st["v"]

        def shift_k():
            st["k"] = shifted(xe_ref, mux_ref, slice(bw, 2 * bw))
            kkp = st["k"] * kk_w_ref[...]
            st["kk"] = kkp * lax.rsqrt(head_sum(kkp * kkp, 1) + L2_EPS)
            kk_scr[rs, :] = st["kk"]
            st["kb"] = jnp.zeros_like(st["k"])

        def lora():
            lo = shifted(le_ref, mul_ref, slice(None))
            l01 = lo[:, :LANES]
            xin = jnp.where(lane < 2 * lora_w, jnp.tanh(l01), l01)
            st["wa"] = _dot(xin, wcat_ref[...]) + bias_ref[...]
            gate_ref[0, rs, :] = _dot(_sigmoid(lo[:, LANES:]), g2_ref[...])

        def direction(e):
            wa = st["wa"]
            a = _sigmoid(wa[:, (2 + e) * bw:(3 + e) * bw])
            lw_scr[e, rs, :] = -B_DECAY_SCALE * _sigmoid(wa[:, e * bw:(e + 1) * bw])
            kka_scr[e, rs, :] = st["kk"] * a
            kd = st["k"] * (1.0 + (a - 1.0) * ka_ref[...])
            kd_scr[e, rs, :] = kd
            st["kb"] = st["kb"] + 0.5 * kd

        def bonus():
            bonus_ref[0, rs, :] = head_sum(st["r"] * st["kb"] * rk_ref[...], 2) * st["v"]

        return [shift_rv, shift_k, lora, functools.partial(direction, 0), functools.partial(direction, 1), bonus]

    lanes2 = _iota2((1, LANES), 1)
    m0 = lanes2 < CHUNK
    tr = _iota2((CHUNK, CHUNK), 0)
    tc = _iota2((CHUNK, CHUNK), 1)
    cum_sel = (jnp.where(tc <= tr, 1.0, 0.0).astype(BF16), jnp.where(tc >= tr, 1.0, 0.0).astype(BF16))
    masks = (_unit_masks(False, False), _unit_masks(True, True))

    def stack(a):
        return jnp.concatenate([jnp.where(m0, a, 0.0), jnp.where(m0, 0.0, a)], axis=0)

    def halves(a):
        return a[:CHUNK] + a[CHUNK:]

    nu = 2 * pairs

    def stage_items(c, base):
        rows_c = slice(c * CHUNK, (c + 1) * CHUNK)
        cum = {}

        def cumulate(e):
            cum[e] = _dot_sel(cum_sel[e], lw_scr[e, rows_c, :])

        def unit_steps(e, p):
            incl, strict, blk, same, eye = masks[e]
            last = CHUNK - 1 if e == 0 else 0
            ls = slice(p * LANES, (p + 1) * LANES)
            u = base + e * pairs + p
            st = {}

            def decayed_rows():
                cs = cum[e][:, ls]
                e_neg = jnp.exp(-cs)
                st["xs"] = stack(-kk_scr[rows_c, ls] * jnp.exp(cs - lw_scr[e, rows_c, ls])).astype(BF16)
                st["bk"] = jnp.concatenate([stack(kka_scr[e, rows_c, ls] * e_neg), stack(kd_scr[e, rows_c, ls] * e_neg)],
                                           axis=0).astype(BF16)
                sol.set_rhs(u, 0, st["xs"])

            def pair_a():
                ga = _dot_nt(st["xs"], st["bk"])
                sol.stage(u, jnp.where(strict, ga[:, :UNIT], 0.0), blk, eye)
                kr_scr[u] = jnp.where(strict, ga[:, UNIT:], 0.0).astype(BF16)

            def pair_r():
                rt = r_scr[rows_c, ls] * jnp.exp(cum[e][:, ls])
                gr = _dot_nt(stack(rt), st["bk"])
                aq_ref[e, 0, c, p] = jnp.concatenate(
                    [halves(jnp.where(incl, gr[:, :UNIT], 0.0)), halves(jnp.where(incl, gr[:, UNIT:], 0.0))],
                    axis=1).astype(BF16)
                wq_ref.at[e, 0, c, p][CHUNK:, :] = rt.astype(BF16)

            def seq_operands():
                cs = cum[e][:, ls]
                cl = cum[e][last:last + 1, ls]
                e_last = jnp.exp(cl - cs)
                vv = v_scr[rows_c, ls]
                vs_scr[u] = stack(vv).astype(BF16)
                vb_ref[e, 0, c, p] = vv.astype(BF16)
                bk_ref[e, 0, c, p] = jnp.concatenate(
                    [kka_scr[e, rows_c, ls] * e_last, kd_scr[e, rows_c, ls] * e_last], axis=0).T.astype(BF16)
                gm_ref[e, 0, c, p] = jnp.broadcast_to(jnp.exp(cl), (GAM_ROWS, LANES))

            return [decayed_rows, pair_a, pair_r, seq_operands]

        items = []
        for e in range(2):
            items.append(functools.partial(cumulate, e))
            for p in range(pairs):
                items += unit_steps(e, p)
        return items

    def solve_items(base):
        def av(u):
            sol.set_rhs(u, 1, jnp.dot(kr_scr[u], vs_scr[u], preferred_element_type=F32))

        return ([functools.partial(av, u) for u in range(base, base + nu)]
                + sol.items(range(base, base + nu)))

    def finish_items(c, base):
        def unit(e, p):
            u = base + e * pairs + p
            wq_ref.at[e, 0, c, p][:CHUNK, :] = halves(sol.result(u, 0)).astype(BF16)
            u0_ref[e, 0, c, p] = halves(sol.result(u, 1))

        return [functools.partial(unit, e, p) for e in range(2) for p in range(pairs)]

    n_chunks = rows // CHUNK
    _emit(elementwise_items(0) + stage_items(0, 0))
    for c in range(n_chunks):
        base, other = (c % 2) * nu, ((c + 1) % 2) * nu
        side = finish_items(c - 1, other) if c > 0 else []
        if c + 1 < n_chunks:
            side = side + elementwise_items(c + 1) + stage_items(c + 1, other)
        _emit(solve_items(base), side)
    _emit(finish_items(n_chunks - 1, ((n_chunks - 1) % 2) * nu))


def _rwkv_prep(rkv, lora, mu_x, mu_l, wcat, bias, g2p, k_k, k_a, r_k, lora_w, lora_a, rows=512):
    b, t, width = rkv.shape
    bw = width // 3
    pairs = bw // LANES
    lw_ = lora.shape[2]
    n = t // CHUNK
    nb = rows // CHUNK
    unit = lambda r, dt, w=LANES: jax.ShapeDtypeStruct((2, b, n, pairs, r, w), dt)
    ospec = lambda r, w=LANES: pl.BlockSpec((2, 1, nb, pairs, r, w), lambda bi, i: (0, bi, i, 0, 0, 0))
    row_spec = pl.BlockSpec((1, rows, bw), lambda bi, i: (bi, i, 0))
    return pl.pallas_call(
        functools.partial(_rwkv_prep_kernel, lora_w=lora_w, lora_a=lora_a),
        grid=(b, t // rows),
        in_specs=_halo_specs(rows, width, t) + _halo_specs(rows, lw_, t) + [
            _const_spec((1, width)), _const_spec((1, lw_)), _const_spec(wcat.shape), _const_spec(bias.shape),
            _const_spec(g2p.shape), _const_spec((1, bw)), _const_spec((1, bw)), _const_spec((1, bw))],
        out_specs=[ospec(UNIT), ospec(CHUNK), ospec(CHUNK, 2 * LANES), ospec(UNIT), ospec(CHUNK),
                   ospec(GAM_ROWS), row_spec, row_spec],
        out_shape=[unit(UNIT, BF16), unit(CHUNK, F32), unit(CHUNK, BF16, 2 * LANES), unit(UNIT, BF16),
                   unit(CHUNK, BF16), unit(GAM_ROWS, F32),
                   jax.ShapeDtypeStruct((b, t, bw), F32), jax.ShapeDtypeStruct((b, t, bw), F32)],
        scratch_shapes=[pltpu.VMEM((rows + 2 * HALO, width), F32), pltpu.VMEM((rows + 2 * HALO, lw_), F32)]
        + [pltpu.VMEM((rows, bw), F32)] * 3 + [pltpu.VMEM((2, rows, bw), F32)] * 3
        + [pltpu.VMEM((4 * pairs, UNIT, UNIT), BF16), pltpu.VMEM((4 * pairs, UNIT, LANES), BF16)]
        + _UnitSolver.scratch(4 * pairs),
        compiler_params=_params(("parallel", "arbitrary"), rows * width * 4, rows * lw_ * 4, wcat.size * 2,
                                2 * pairs * nb * (UNIT * 2 + CHUNK * 4 + CHUNK * 4 + UNIT * 2 + CHUNK * 2 + UNIT * 4)
                                * LANES, 2 * rows * bw * 4,
                                scratch=(rows + 2 * HALO) * (width + lw_) * 4 + 9 * rows * bw * 4
                                + 4 * pairs * (_UnitSolver.BYTES_PER_UNIT + 2 * UNIT * LANES * 2),
                                temps=8 * rows * bw * 4 * 4),
        name="rwkv_prep",
    )(rkv, rkv, rkv, lora, lora, lora, mu_x, mu_l, wcat, bias, g2p, k_k, k_a, r_k)


def _seq_kernel(*refs, paired, nops):
    dir_ops = (refs[:nops], refs[nops:2 * nops])
    y_refs = refs[2 * nops:2 * nops + 2]
    s_ref, pq_ref = refs[2 * nops + 2:]
    nb_, nu_ = dir_ops[0][0].shape[1], dir_ops[0][0].shape[3]
    units = [(d, bi, ui) for d in range(2) for bi in range(nb_) for ui in range(nu_)]
    at = lambda ref, bi, ui: ref[0, bi, 0, ui]

    @pl.when(pl.program_id(0) == 0)
    def _():
        s_ref[...] = jnp.zeros_like(s_ref)

    m0 = _iota2((1, LANES), 1) < CHUNK
    same = (_iota2((LANES, LANES), 0) // CHUNK) == (_iota2((LANES, LANES), 1) // CHUNK)
    for j, (d, bi, ui) in enumerate(units):
        pq_ref[j] = jnp.dot(at(dir_ops[d][0], bi, ui), s_ref[j].astype(BF16), preferred_element_type=F32)
    for j, (d, bi, ui) in enumerate(units):
        wq_ref, u_ref, aq_ref, bt_ref, gm_ref = dir_ops[d][:5]
        s = s_ref[j]
        pq = pq_ref[j]
        un = at(u_ref, bi, ui) + pq[:CHUNK]
        unb = un.astype(BF16)
        if paired:
            vb = at(dir_ops[d][5], bi, ui)
            zero = jnp.zeros_like(unb)
            rhs_y = jnp.concatenate([jnp.where(m0, unb, zero), jnp.where(m0, zero, unb),
                                     jnp.where(m0, vb, zero), jnp.where(m0, zero, vb)], axis=0)
            rhs_s = jnp.concatenate([unb, vb], axis=0)
        else:
            rhs_y = jnp.concatenate([unb, unb], axis=0)
            rhs_s = unb
        y_refs[d][bi, 0, ui] = pq[CHUNK:] + jnp.dot(at(aq_ref, bi, ui), rhs_y, preferred_element_type=F32)
        upd = jnp.dot(at(bt_ref, bi, ui), rhs_s, preferred_element_type=F32)
        if paired:
            upd = jnp.where(same, upd, 0.0)
        gam = at(gm_ref, bi, ui).T[:, 0:1]
        s_ref[j] = s * jnp.broadcast_to(gam, (LANES, LANES)) + upd


def _seq(ops, paired):
    _, b, n, nu = ops[0].shape[:4]
    ub = 2 * b * nu
    fwd = lambda ni: (0, 0, ni, 0, 0, 0)
    bwd = lambda ni: (1, 0, n - 1 - ni, 0, 0, 0)
    specs = [pl.BlockSpec((1, b, 1, nu) + a.shape[4:], imap) for imap in (fwd, bwd) for a in ops]
    block_bytes = [ub * math.prod(a.shape[4:]) * a.dtype.itemsize for a in ops]
    y_shape = jax.ShapeDtypeStruct((b, n, nu, CHUNK, LANES), F32)
    return pl.pallas_call(
        functools.partial(_seq_kernel, paired=paired, nops=len(ops)),
        grid=(n,),
        in_specs=specs,
        out_specs=[pl.BlockSpec((b, 1, nu, CHUNK, LANES), lambda ni: (0, ni, 0, 0, 0)),
                   pl.BlockSpec((b, 1, nu, CHUNK, LANES), lambda ni: (0, n - 1 - ni, 0, 0, 0))],
        out_shape=[y_shape, y_shape],
        scratch_shapes=[pltpu.VMEM((ub, LANES, LANES), F32), pltpu.VMEM((ub, UNIT, LANES), F32)],
        compiler_params=_params(("arbitrary",), *block_bytes, ub * CHUNK * LANES * 4,
                                scratch=ub * (LANES + UNIT) * LANES * 4, temps=ub * UNIT * LANES * 4),
        name="seq_rwkv" if paired else "seq_delta",
    )(*ops, *ops)


def _even_out_kernel(of_ref, ob_ref, yf_ref, yb_ref, z_ref, bonus_ref, gate_ref, h_ref, an_ref, lnw_ref, lnb_ref,
                     wo_ref, p_ref, nw_ref, wg_ref, wu_ref, wd_ref, npl_ref, wp_ref, wpg_ref, *rest):
    nf_ref, o_ref = rest if len(rest) == 2 else (None, rest[0])
    rows = h_ref.shape[1]
    a_heads = of_ref.shape[2]
    pairs = yf_ref.shape[2]
    z = z_ref[0]
    pieces = []
    for hh in range(a_heads):
        o = (of_ref[0, :, hh] + ob_ref[0, :, hh]).reshape(rows, LANES)
        pieces.append(_rms(o, an_ref[...]) * _silu(z[:, hh * LANES:(hh + 1) * LANES]))
    sr = _iota2((LANES, LANES), 0) // CHUNK
    sc = _iota2((LANES, LANES), 1) // CHUNK
    head_sel = jnp.where(sr == sc, 1.0, 0.0).astype(BF16)
    inv_n = 1.0 / CHUNK
    for p in range(pairs):
        ls = slice(p * LANES, (p + 1) * LANES)
        y = (yf_ref[0, :, p] + yb_ref[0, :, p]).reshape(rows, LANES)
        mu = _dot_sel_r(y, head_sel, 2) * inv_n
        yc = y - mu
        var = _dot_sel_r(yc * yc, head_sel, 2) * inv_n
        yn = yc * lax.rsqrt(var + B_GN_EPS) * lnw_ref[:, ls] + lnb_ref[:, ls]
        pieces.append((yn + bonus_ref[0][:, ls]) * gate_ref[0][:, ls])
    mix = jnp.concatenate(pieces, axis=1).astype(BF16)
    h = h_ref[0] + jnp.dot(mix, wo_ref[...], preferred_element_type=F32)
    o_ref[0] = _ffn_ple_math(h, p_ref[0], nw_ref, wg_ref, wu_ref, wd_ref, npl_ref, wp_ref, wpg_ref, nf_ref)


def _even_out(od, yr, z, bonus, gate, h3, a_norm, ln_w, ln_b, wo, p3, ffn, nf=None, rows=256):
    b, t, d = h3.shape
    nb = rows // CHUNK
    a_heads, pairs = od[0].shape[2], yr[0].shape[2]
    aw, bw = z.shape[2], bonus.shape[2]
    pd = p3.shape[2]
    nw, wg, wu, wd, npl, wp, wpg = ffn
    ff = wg.shape[1]
    dspec = lambda nh: pl.BlockSpec((1, nb, nh, CHUNK, LANES), lambda bi, i: (bi, i, 0, 0, 0))
    rspec = lambda w: pl.BlockSpec((1, rows, w), lambda bi, i: (bi, i, 0))
    ins = [od[0], od[1], yr[0], yr[1], z, bonus, gate, h3, a_norm, ln_w, ln_b, wo, p3, nw, wg, wu, wd, npl, wp, wpg]
    specs = [dspec(a_heads), dspec(a_heads), dspec(pairs), dspec(pairs), rspec(aw), rspec(bw),
             rspec(bw), rspec(d), _const_spec((1, LANES)), _const_spec((1, bw)), _const_spec((1, bw)),
             _const_spec(wo.shape, True), rspec(pd), _const_spec((1, d)),
             _const_spec((d, ff), True), _const_spec((d, ff), True), _const_spec((ff, d), True),
             _const_spec((1, d)), _const_spec((pd, d), True), _const_spec((d, d), True)]
    if nf is not None:
        ins.append(nf)
        specs.append(_const_spec((1, d)))
    weights = (wo.size + 3 * d * ff + pd * d + d * d) * 2
    return pl.pallas_call(
        _even_out_kernel,
        grid=(b, t // rows),
        in_specs=specs,
        out_specs=rspec(d),
        out_shape=jax.ShapeDtypeStruct((b, t, d), F32),
        compiler_params=_params(("parallel", "parallel"), 2 * rows * (aw + bw) * 4,
                                rows * (aw + 2 * bw + 2 * d + pd) * 4, scratch=weights,
                                temps=rows * ff * (4 + 4 + 2) + 8 * rows * d * 4),
        name="even_out_ffn",
    )(*ins)


def _even_layer(h3, nw, w_in, w_out, a_conv, a_log, a_dt_bias, a_norm, b_mu, b_w0, b_w2, b_a0, b_a2, b_g2,
                b_k_k, b_k_a, b_r_k, b_ln_w, b_ln_b, p3, ffn, nf=None):
    b, t, d = h3.shape
    a_heads = a_log.shape[1]
    hd = a_norm.shape[0]
    aw = a_heads * hd
    b_heads, b_hd = b_r_k.shape
    bw = b_heads * b_hd
    lora_w, lora_a, lora_g = b_w2.shape[1], b_a2.shape[1], b_g2.shape[0]
    assert b_hd == CHUNK and hd == LANES and 2 * (lora_w + lora_a) == LANES and lora_g <= LANES
    a_cols = 4 * aw + 4 * a_heads
    n_gate = 4 * a_heads
    pad_cols = lambda w, n: jnp.pad(w, ((0, 0), (0, n - w.shape[1])))
    w_bf = w_in.astype(BF16)
    wa, wb = w_bf[:, :a_cols], w_bf[:, a_cols:]
    w_all = jnp.concatenate([wa[:, :4 * aw], pad_cols(wa[:, 4 * aw:], LANES),
                             wb[:, :3 * bw], pad_cols(wb[:, 3 * bw:], 2 * LANES)], axis=1)
    splits = (3 * aw, aw, LANES, 3 * bw, 2 * LANES)
    qkv, z, ab, rkv, lora = _proj(h3.reshape(b * t, d), nw.reshape(1, d), w_all, splits)
    to3 = lambda a: a.reshape(b, t, a.shape[-1])

    alog_row = jnp.zeros((1, LANES), F32).at[0, :2 * a_heads].set(a_log.reshape(-1))
    dtb_row = jnp.zeros((1, LANES), F32).at[0, :2 * a_heads].set(a_dt_bias.reshape(-1))
    d_ops = _delta_prep(to3(qkv), to3(ab), a_conv, alog_row, dtb_row, a_heads)
    od = _seq(d_ops, paired=False)

    wcat = jnp.zeros((LANES, 4 * bw), F32)
    for e in range(2):
        wcat = wcat.at[e * lora_w:(e + 1) * lora_w, e * bw:(e + 1) * bw].set(b_w2[e])
        wcat = wcat.at[2 * lora_w + e * lora_a:2 * lora_w + (e + 1) * lora_a, (2 + e) * bw:(3 + e) * bw].set(b_a2[e])
    bias = jnp.concatenate([b_w0[0], b_w0[1], b_a0[0], b_a0[1]]).reshape(1, 4 * bw)
    g2p = jnp.zeros((LANES, bw), F32).at[:lora_g].set(b_g2).astype(BF16)
    mu_x = b_mu[:3 * bw].reshape(1, 3 * bw)
    mu_l = jnp.pad(b_mu[3 * bw:], (0, 2 * LANES - (b_mu.shape[0] - 3 * bw))).reshape(1, 2 * LANES)
    row = lambda a: a.reshape(1, bw)
    r_ops = _rwkv_prep(to3(rkv), to3(lora), mu_x, mu_l, wcat.astype(BF16), bias, g2p, row(b_k_k), row(b_k_a),
                       row(b_r_k), lora_w, lora_a)
    wq, u0, aq, bk, vb, gm, bonus, gate = r_ops
    yr = _seq([wq, u0, aq, bk, gm, vb], paired=True)

    return _even_out(od, yr, to3(z), bonus, gate, h3, a_norm.reshape(1, hd), row(b_ln_w), row(b_ln_b),
                     w_out.astype(BF16), p3, ffn, nf)


def _odd_layer(h3, nw, w_in, ln_w, ln_b, ws, bs, w_out):
    b, t, d = h3.shape
    groups, chunk, _ = ws.shape
    cw = w_out.shape[0]
    bs_full = jnp.repeat(bs.T, cw // groups, axis=1)
    out = _gmlp(h3.reshape(b * t, d), nw.reshape(1, d), w_in.astype(BF16), ln_w.reshape(1, cw), ln_b.reshape(1, cw),
                ws.astype(BF16), bs_full, w_out.astype(BF16))
    return out.reshape(b, t, d)


def kernel(x, p, norm_mix, norm_ffn, norm_ple, norm_final, w_in_even, w_out_even, a_conv, a_log, a_dt_bias, a_norm,
           b_mu, b_w0, b_w2, b_a0, b_a2, b_g2, b_k_k, b_k_a, b_r_k, b_ln_w, b_ln_b, w_in_odd, c_ln_w, c_ln_b, c_ws,
           c_bs, w_out_odd, w_gate, w_up, w_down, w_ple, w_ple_gate):
    b, t, d = x.shape
    depth = p.shape[0]
    h = x
    for i in range(depth):
        j = i // 2
        ffn = (norm_ffn[i].reshape(1, d), w_gate[i].astype(BF16), w_up[i].astype(BF16), w_down[i].astype(BF16),
               norm_ple[i].reshape(1, d), w_ple[i].astype(BF16), w_ple_gate[i].astype(BF16))
        nf = norm_final.reshape(1, d) if i == depth - 1 else None
        if i % 2 == 0:
            h = _even_layer(h, norm_mix[i], w_in_even[j], w_out_even[j], a_conv[j], a_log[j], a_dt_bias[j], a_norm[j],
                            b_mu[j], b_w0[j], b_w2[j], b_a0[j], b_a2[j], b_g2[j], b_k_k[j], b_k_a[j], b_r_k[j],
                            b_ln_w[j], b_ln_b[j], p[i], ffn, nf)
        else:
            h = _odd_layer(h, norm_mix[i], w_in_odd[j], c_ln_w[j], c_ln_b[j], c_ws[j], c_bs[j], w_out_odd[j])
            h = _ffn_ple(h.reshape(b * t, d), p[i].reshape(b * t, -1), *ffn, nf).reshape(b, t, d)
    return h
```

```python
import functools
import math

import jax
import jax.numpy as jnp
from jax import lax
from jax.experimental import pallas as pl
from jax.experimental.pallas import tpu as pltpu

F32 = jnp.float32
BF16 = jnp.bfloat16

NORM_EPS = 1e-6
L2_EPS = 1e-6
B_GN_EPS = 64e-5
B_DECAY_SCALE = 0.606531
C_LN_EPS = 1e-5

LANES = 128
SUBLANES = 8
CHUNK = 64
UNIT = 2 * CHUNK
INV_BLOCK = 16
GAM_ROWS = SUBLANES
UNITS_IN_FLIGHT = 8
HALO = SUBLANES
VMEM_BUDGET = 56 * 1024 * 1024


def _params(sem, *block_bytes, scratch=0, temps=0, flags=None):
    need = 2 * sum(block_bytes) + scratch + temps
    return pltpu.CompilerParams(dimension_semantics=sem, vmem_limit_bytes=int(min(max(need, 16 << 20), VMEM_BUDGET)),
                                flags=flags)


def _dot(a, b):
    return jnp.dot(a.astype(BF16), b.astype(BF16), preferred_element_type=F32)


def _dot_nt(a, b):
    return lax.dot_general(a.astype(BF16), b.astype(BF16), (((1,), (1,)), ((), ())), preferred_element_type=F32)


def _split3(x):
    hi = x.astype(BF16)
    r1 = x - hi.astype(F32)
    mid = r1.astype(BF16)
    lo = (r1 - mid.astype(F32)).astype(BF16)
    return hi, mid, lo


def _dot_sel(sel, x):
    hi, mid, lo = _split3(x)
    d = lambda p: jnp.dot(sel, p, preferred_element_type=F32)
    return d(hi) + d(mid) + d(lo)


def _dot_sel_r(x, sel, pieces=3):
    d = lambda p: jnp.dot(p, sel, preferred_element_type=F32)
    return sum(d(p) for p in _split3(x)[:pieces])


def _sigmoid(x):
    return 1.0 / (1.0 + jnp.exp(-x))


def _silu(x):
    return x * _sigmoid(x)


def _softplus(x):
    return jnp.maximum(x, 0.0) + jnp.log(1.0 + jnp.exp(-jnp.abs(x)))


def _gelu_tanh(x):
    return 0.5 * x * (1.0 + jnp.tanh(math.sqrt(2.0 / math.pi) * (x + 0.044715 * (x * x * x))))


def _rms(x, w, eps=NORM_EPS):
    return x * lax.rsqrt(jnp.mean(x * x, axis=-1, keepdims=True) + eps) * w


def _iota2(shape, dim):
    return lax.broadcasted_iota(jnp.int32, shape, dim)


def _unit_masks(upper_top, upper_bot):
    r = _iota2((UNIT, UNIT), 0)
    c = _iota2((UNIT, UNIT), 1)
    same = (r // CHUNK) == (c // CHUNK)
    sign = jnp.where(r < CHUNK, -1 if upper_top else 1, -1 if upper_bot else 1)
    d = jnp.where(same, (r - c) * sign, -1)
    incl = d >= 0
    strict = d > 0
    blk = (r // INV_BLOCK) == (c // INV_BLOCK)
    eye = jnp.where(r == c, 1.0, 0.0).astype(F32)
    return incl, strict, blk, same, eye


class _UnitSolver:
    def __init__(self, refs):
        self.p, self.t, self.z, self.x, self.y = refs

    @staticmethod
    def scratch(nu):
        return [pltpu.VMEM((nu, UNIT, UNIT), BF16), pltpu.VMEM((nu, UNIT, UNIT), F32),
                pltpu.VMEM((nu, UNIT, 3 * UNIT), BF16), pltpu.VMEM((2, nu, UNIT, UNIT), BF16),
                pltpu.VMEM((nu, UNIT, 2 * UNIT), F32)]

    BYTES_PER_UNIT = UNIT * UNIT * (2 + 4 + 6 + 4 + 8)

    def stage(self, u, a, blk, eye):
        ad = jnp.where(blk, a, 0.0)
        self.p[u] = ad.astype(BF16)
        self.t[u] = eye + ad
        self.z[u, :, :UNIT] = jnp.where(blk, 0.0, a).astype(BF16)

    def set_rhs(self, u, tile, val):
        self.z[u, :, (tile + 1) * UNIT:(tile + 2) * UNIT] = val.astype(BF16)

    def result(self, u, tile):
        return self.y[u, :, tile * UNIT:(tile + 1) * UNIT]

    def items(self, units):
        assert CHUNK // INV_BLOCK == 4
        dot = lambda a, b: jnp.dot(a, b, preferred_element_type=F32)

        def square(u):
            p = self.p[u]
            self.p[u] = dot(p, p).astype(BF16)

        def grow(u):
            t = self.t[u]
            self.t[u] = t + dot(t.astype(BF16), self.p[u])

        def grow_square(u):
            t, p = self.t[u], self.p[u]
            r = dot(jnp.concatenate([t.astype(BF16), p], axis=0), p)
            self.t[u] = t + r[:UNIT]
            self.p[u] = r[UNIT:].astype(BF16)

        def apply_t(u):
            t = self.t[u].astype(BF16)
            self.x[0, u] = dot(t, self.z[u, :, :UNIT]).astype(BF16)
            self.y[u] = dot(t, self.z[u, :, UNIT:])

        def apply_x(u):
            x = self.x[0, u]
            self.x[1, u] = dot(x, x).astype(BF16)
            y = self.y[u]
            self.y[u] = y + dot(x, y.astype(BF16))

        def apply_x2(u):
            y = self.y[u]
            self.y[u] = y + dot(self.x[1, u], y.astype(BF16))

        levels = int(math.log2(INV_BLOCK)) - 1
        stages = [square] + [grow_square] * (levels - 1) + [grow, apply_t, apply_x, apply_x2]
        return [functools.partial(stage, u) for stage in stages for u in units]


def _emit(main, side=()):
    n, m = len(main), len(side)
    j = 0
    for idx, step in enumerate(main):
        step()
        while j < m and (j + 1) * n <= (idx + 1) * m:
            side[j]()
            j += 1
    for step in side[j:]:
        step()


def _fill_halo(xe_ref, x_ref, prev_ref, next_ref, i, nblk):
    rows = x_ref.shape[1]
    pf = jnp.where(i > 0, 1.0, 0.0).astype(F32)
    nf = jnp.where(i < nblk - 1, 1.0, 0.0).astype(F32)
    xe_ref[0:HALO, :] = prev_ref[0] * pf
    xe_ref[HALO:HALO + rows, :] = x_ref[0]
    xe_ref[HALO + rows:2 * HALO + rows, :] = next_ref[0] * nf


def _halo_specs(rows, width, t_len):
    per = rows // HALO
    last = t_len // HALO - 1
    return [
        pl.BlockSpec((1, rows, width), lambda b, i: (b, i, 0)),
        pl.BlockSpec((1, HALO, width), lambda b, i: (b, jnp.maximum(i * per - 1, 0), 0)),
        pl.BlockSpec((1, HALO, width), lambda b, i: (b, jnp.minimum((i + 1) * per, last), 0)),
    ]


def _const_spec(shape, single_buffer=False):
    nd = len(shape)
    if single_buffer:
        return pl.BlockSpec(shape, lambda *_: (0,) * nd, pipeline_mode=pl.Buffered(1))
    return pl.BlockSpec(shape, lambda *_: (0,) * nd)


def _proj_kernel(h_ref, nw_ref, w_ref, *o_refs, splits):
    hn = _rms(h_ref[...], nw_ref[...]).astype(BF16)
    off = 0
    for o_ref, n in zip(o_refs, splits):
        o_ref[...] = jnp.dot(hn, w_ref[:, off:off + n], preferred_element_type=F32)
        off += n


def _proj(h2, nw, w, splits, tm=512):
    m, d = h2.shape
    n = w.shape[1]
    return pl.pallas_call(
        functools.partial(_proj_kernel, splits=splits),
        grid=(m // tm,),
        in_specs=[pl.BlockSpec((tm, d), lambda i: (i, 0)), _const_spec((1, d)), _const_spec((d, n))],
        out_specs=[pl.BlockSpec((tm, s), lambda i: (i, 0)) for s in splits],
        out_shape=[jax.ShapeDtypeStruct((m, s), F32) for s in splits],
        compiler_params=_params(("parallel",), tm * d * 4, d * n * 2, tm * n * 4, temps=tm * d * 8),
        name="proj",
    )(h2, nw, w)


def _ffn_ple_kernel(h_ref, p_ref, nw_ref, wg_ref, wu_ref, wd_ref, npl_ref, wp_ref, wpg_ref, *rest, final):
    if final:
        nf_ref, o_ref = rest
    else:
        nf_ref, (o_ref,) = None, rest
    o_ref[...] = _ffn_ple_math(h_ref[...], p_ref[...], nw_ref, wg_ref, wu_ref, wd_ref, npl_ref, wp_ref, wpg_ref, nf_ref)


def _ffn_ple_math(h, p, nw_ref, wg_ref, wu_ref, wd_ref, npl_ref, wp_ref, wpg_ref, nf_ref):
    hn = _rms(h, nw_ref[...]).astype(BF16)
    g = jnp.dot(hn, wg_ref[...], preferred_element_type=F32)
    u = jnp.dot(hn, wu_ref[...], preferred_element_type=F32)
    h = h + jnp.dot((_silu(g) * u).astype(BF16), wd_ref[...], preferred_element_type=F32)
    gate = _sigmoid(jnp.dot(_rms(h, npl_ref[...]).astype(BF16), wpg_ref[...], preferred_element_type=F32))
    e = jnp.dot(p.astype(BF16), wp_ref[...], preferred_element_type=F32)
    out = h + e * gate
    if nf_ref is not None:
        out = _rms(out, nf_ref[...])
    return out


def _ffn_ple(h2, p_sel, nw, wg, wu, wd, npl, wp, wpg, nf=None, tm=512):
    m, d = h2.shape
    ff = wg.shape[1]
    p_all, layer = p_sel
    pd = p_all.shape[-1]
    p2 = p_all.reshape(p_all.shape[0], m, pd)
    final = nf is not None
    ins = [h2, p2, nw, wg, wu, wd, npl, wp, wpg] + ([nf] if final else [])
    specs = [
        pl.BlockSpec((tm, d), lambda i: (i, 0)),
        pl.BlockSpec((None, tm, pd), lambda i: (layer, i, 0)),
        _const_spec((1, d)),
        _const_spec((d, ff), True), _const_spec((d, ff), True), _const_spec((ff, d), True),
        _const_spec((1, d)), _const_spec((pd, d), True), _const_spec((d, d), True),
    ] + ([_const_spec((1, d))] if final else [])
    weights = (3 * d * ff + pd * d + d * d) * 2
    return pl.pallas_call(
        functools.partial(_ffn_ple_kernel, final=final),
        grid=(m // tm,),
        in_specs=specs,
        out_specs=pl.BlockSpec((tm, d), lambda i: (i, 0)),
        out_shape=jax.ShapeDtypeStruct((m, d), F32),
        compiler_params=_params(("parallel",), tm * d * 4, tm * pd * 4, tm * d * 4,
                                scratch=weights, temps=tm * ff * (4 + 4 + 2) + 3 * tm * d * 4),
        name="ffn_ple",
    )(*ins)


def _gmlp_kernel(h_ref, nw_ref, wi_ref, lnw_ref, lnb_ref, ws_ref, bs_ref, wo_ref, o_ref, *, chunk, groups):
    rows, d = h_ref.shape
    cw = wo_ref.shape[0]
    gd = cw // groups

    def chunk_steps(c):
        rs = slice(c * chunk, (c + 1) * chunk)
        st = {}

        def project():
            hn = _rms(h_ref[rs, :], nw_ref[...]).astype(BF16)
            st["uv"] = jnp.dot(hn, wi_ref[...], preferred_element_type=F32)

        def activate():
            uv = _gelu_tanh(st.pop("uv"))
            st["u"] = uv[:, :cw]
            v = uv[:, cw:]
            mu = jnp.mean(v, axis=-1, keepdims=True)
            vc = v - mu
            var = jnp.mean(vc * vc, axis=-1, keepdims=True)
            st["vn"] = (vc * lax.rsqrt(var + C_LN_EPS) * lnw_ref[...] + lnb_ref[...]).astype(BF16)

        def mix():
            vn = st.pop("vn")
            sv = jnp.concatenate(
                [jnp.dot(ws_ref[g], vn[:, g * gd:(g + 1) * gd], preferred_element_type=F32) for g in range(groups)],
                axis=1)
            st["gated"] = (st.pop("u") * (sv + bs_ref[...])).astype(BF16)

        def project_out():
            o_ref[rs, :] = h_ref[rs, :] + jnp.dot(st.pop("gated"), wo_ref[...], preferred_element_type=F32)

        return [project, activate, mix, project_out]

    steps = [chunk_steps(c) for c in range(rows // chunk)]
    depth = len(steps[0])
    for wave in range(len(steps) + depth - 1):
        for c in range(len(steps)):
            k = wave - c
            if 0 <= k < depth:
                steps[c][k]()


def _gmlp(h2, nw, wi, lnw, lnb, ws, bs_full, wo, tm=512):
    m, d = h2.shape
    groups, chunk, _ = ws.shape
    cw = wo.shape[0]
    return pl.pallas_call(
        functools.partial(_gmlp_kernel, chunk=chunk, groups=groups),
        grid=(m // tm,),
        in_specs=[pl.BlockSpec((tm, d), lambda i: (i, 0)), _const_spec((1, d)), _const_spec(wi.shape),
                  _const_spec((1, cw)), _const_spec((1, cw)), _const_spec(ws.shape), _const_spec(bs_full.shape),
                  _const_spec(wo.shape)],
        out_specs=pl.BlockSpec((tm, d), lambda i: (i, 0)),
        out_shape=jax.ShapeDtypeStruct((m, d), F32),
        compiler_params=_params(("parallel",), tm * d * 4, wi.size * 2, ws.size * 2, bs_full.size * 4, wo.size * 2,
                                tm * d * 4, temps=5 * tm * 2 * cw * 4),
        name="gmlp",
    )(h2, nw, wi, lnw, lnb, ws, bs_full, wo)


def _delta_prep_kernel(x_ref, xp_ref, xn_ref, ab_ref, cw_ref, alog_ref, dtb_ref,
                       wq_ref, u_ref, aq_ref, bt_ref, gm_ref,
                       xe_ref, q_scr, k_scr, v_scr, g_scr, b_scr, *solver_refs, heads, taps, group):
    i = pl.program_id(1)
    rows = x_ref.shape[1]
    width = x_ref.shape[2]
    aw = width // 3
    hd = aw // heads
    sol = _UnitSolver(solver_refs)
    _fill_halo(xe_ref, x_ref, xp_ref, xn_ref, i, pl.num_programs(1))

    def elementwise_items(c):
        r0 = c * CHUNK
        rs = slice(r0, r0 + CHUNK)

        def conv(cols):
            acc = jnp.zeros((CHUNK, hd), F32)
            for j in range(taps):
                s = HALO - taps // 2 + j + r0
                acc = acc + cw_ref[j:j + 1, cols] * xe_ref[s:s + CHUNK, cols]
            return _silu(acc)

        def head(hh):
            qh = conv(slice(hh * hd, (hh + 1) * hd))
            kh = conv(slice(aw + hh * hd, aw + (hh + 1) * hd))
            q_scr[hh, rs, :] = qh * lax.rsqrt(jnp.sum(qh * qh, axis=-1, keepdims=True) + L2_EPS) * (hd ** -0.5)
            k_scr[hh, rs, :] = kh * lax.rsqrt(jnp.sum(kh * kh, axis=-1, keepdims=True) + L2_EPS)
            v_scr[hh, rs, :] = conv(slice(2 * aw + hh * hd, 2 * aw + (hh + 1) * hd))

        def gates():
            ab = ab_ref[0, rs, :]
            g_scr[rs, :] = -jnp.exp(alog_ref[...]) * _softplus(ab + dtb_ref[...])
            b_scr[rs, :] = _sigmoid(ab)

        return [functools.partial(head, hh) for hh in range(heads)] + [gates]

    incl, strict, blk, _, eye = _unit_masks(False, True)
    r1 = _iota2((UNIT, 1), 0)
    top_col = r1 < CHUNK
    top_row = _iota2((1, UNIT), 1) < CHUNK
    tri_r = _iota2((UNIT, CHUNK), 0)
    tri_c = _iota2((UNIT, CHUNK), 1)
    cum_sel = jnp.where(jnp.where(tri_r < CHUNK, tri_r - tri_c, tri_c - tri_r + CHUNK) >= 0, 1.0, 0.0).astype(BF16)

    def stage_items(c, u0):
        rows_c = slice(c * CHUNK, (c + 1) * CHUNK)
        st = {}

        def cumulate():
            st["gst"] = _dot_sel(cum_sel, g_scr[rows_c, :])
            st["gst_t"] = st["gst"].T
            bch = b_scr[rows_c, :]
            st["bst"] = jnp.concatenate([bch, bch], axis=0)

        def head(hh):
            gst, gst_t, bst = st["gst"], st["gst_t"], st["bst"]
            fw, bw = hh, heads + hh
            gcs = jnp.where(top_col, gst[:, fw:fw + 1], gst[:, bw:bw + 1])
            grow = jnp.where(top_row, gst_t[fw:fw + 1, :], gst_t[bw:bw + 1, :])
            bcol = jnp.where(top_col, bst[:, 2 * heads + fw:2 * heads + fw + 1],
                             bst[:, 2 * heads + bw:2 * heads + bw + 1])
            glast = jnp.where(top_col, gst[CHUNK - 1:CHUNK, fw:fw + 1], gst[CHUNK:CHUNK + 1, bw:bw + 1])
            q = q_scr[hh, rows_c, :]
            k = k_scr[hh, rows_c, :]
            v = v_scr[hh, rows_c, :]
            qst = jnp.concatenate([q, q], axis=0)
            kst = jnp.concatenate([k, k], axis=0)
            vst = jnp.concatenate([v, v], axis=0)
            kst_b = kst.astype(BF16)
            decay = jnp.where(incl, jnp.exp(jnp.where(incl, gcs - grow, 0.0)), 0.0)
            kk = _dot_nt(kst_b, kst_b)
            m = jnp.where(strict, bcol * kk * decay, 0.0)
            sol.stage(u0 + hh, -m, blk, eye)
            eg = jnp.exp(gcs)
            sol.set_rhs(u0 + hh, 0, vst * bcol)
            sol.set_rhs(u0 + hh, 1, kst * (bcol * eg))
            qk = _dot_nt(qst, kst_b) * decay
            qd = qst * eg
            kdec = kst * jnp.exp(glast - gcs)
            gam = jnp.exp(glast)
            for dr in range(2):
                sl = slice(dr * CHUNK, (dr + 1) * CHUNK)
                wq_ref.at[dr, 0, c, hh][CHUNK:, :] = qd[sl].astype(BF16)
                aq_ref[dr, 0, c, hh] = qk[sl].astype(BF16)
                bt_ref[dr, 0, c, hh] = kdec[sl].T.astype(BF16)
                gm_ref[dr, 0, c, hh] = jnp.broadcast_to(gam[dr * CHUNK:dr * CHUNK + 1], (GAM_ROWS, LANES))

        return [cumulate] + [functools.partial(head, hh) for hh in range(heads)]

    def finish_items(c, u0):
        def head(hh):
            un, w = sol.result(u0 + hh, 0), sol.result(u0 + hh, 1)
            for dr in range(2):
                sl = slice(dr * CHUNK, (dr + 1) * CHUNK)
                wq_ref.at[dr, 0, c, hh][:CHUNK, :] = (-w[sl]).astype(BF16)
                u_ref[dr, 0, c, hh] = un[sl]

        return [functools.partial(head, hh) for hh in range(heads)]

    nu = group * heads
    chunks = lambda gi: range(gi * group, (gi + 1) * group)

    def prepare(gi, base):
        items = []
        for cc, c in enumerate(chunks(gi)):
            items += elementwise_items(c) + stage_items(c, base + cc * heads)
        return items

    def finish(gi, base):
        return [it for cc, c in enumerate(chunks(gi)) for it in finish_items(c, base + cc * heads)]

    n_groups = rows // (CHUNK * group)
    _emit(prepare(0, 0))
    for gi in range(n_groups):
        base, other = (gi % 2) * nu, ((gi + 1) % 2) * nu
        side = finish(gi - 1, other) if gi > 0 else []
        if gi + 1 < n_groups:
            side = side + prepare(gi + 1, other)
        _emit(sol.items(range(base, base + nu)), side)
    _emit(finish(n_groups - 1, ((n_groups - 1) % 2) * nu))


def _delta_prep(qkv, ab, conv_w, alog_row, dtb_row, heads, rows=512):
    b, t, width = qkv.shape
    hd = width // 3 // heads
    assert hd == LANES
    n = t // CHUNK
    nb = rows // CHUNK
    taps = conv_w.shape[0]
    cw = jnp.zeros((SUBLANES, width), F32).at[:taps].set(conv_w)
    group = UNITS_IN_FLIGHT // heads
    assert nb % group == 0
    unit = lambda r, dt, w=LANES: jax.ShapeDtypeStruct((2, b, n, heads, r, w), dt)
    ospec = lambda r, w=LANES: pl.BlockSpec((2, 1, nb, heads, r, w), lambda bi, i: (0, bi, i, 0, 0, 0))
    return pl.pallas_call(
        functools.partial(_delta_prep_kernel, heads=heads, taps=taps, group=group),
        grid=(b, t // rows),
        in_specs=_halo_specs(rows, width, t) + [
            pl.BlockSpec((1, rows, LANES), lambda bi, i: (bi, i, 0)),
            _const_spec((SUBLANES, width)), _const_spec((1, LANES)), _const_spec((1, LANES))],
        out_specs=[ospec(UNIT), ospec(CHUNK), ospec(CHUNK), ospec(UNIT, CHUNK), ospec(GAM_ROWS)],
        out_shape=[unit(UNIT, BF16), unit(CHUNK, F32), unit(CHUNK, BF16), unit(UNIT, BF16, CHUNK),
                   unit(GAM_ROWS, F32)],
        scratch_shapes=[pltpu.VMEM((rows + 2 * HALO, width), F32)]
        + [pltpu.VMEM((heads, rows, LANES), F32)] * 3 + [pltpu.VMEM((rows, LANES), F32)] * 2
        + _UnitSolver.scratch(2 * group * heads),
        compiler_params=_params(("parallel", "arbitrary"), rows * width * 4, rows * LANES * 4,
                                2 * heads * nb * (UNIT * 2 + CHUNK * 4 + CHUNK * 2 + UNIT * 2 + UNIT * 4) * LANES,
                                scratch=(2 * rows + 2 * HALO) * width * 4
                                + 2 * group * heads * _UnitSolver.BYTES_PER_UNIT, temps=4 * rows * width * 4),
        name="delta_prep",
    )(qkv, qkv, qkv, ab, cw, alog_row, dtb_row)


def _rwkv_prep_kernel(x_ref, xp_ref, xn_ref, l_ref, lp_ref, ln_ref, mux_ref, mul_ref, wcat_ref, bias_ref, g2_ref,
                      kk_w_ref, ka_ref, rk_ref,
                      wq_ref, u0_ref, aq_ref, bk_ref, vb_ref, gm_ref, bonus_ref, gate_ref,
                      xe_ref, le_ref, r_scr, v_scr, kk_scr, lw_scr, kka_scr, kd_scr, kr_scr, vs_scr,
                      *solver_refs, lora_w, lora_a):
    i = pl.program_id(1)
    nblk = pl.num_programs(1)
    sol = _UnitSolver(solver_refs)
    rows = x_ref.shape[1]
    bw = x_ref.shape[2] // 3
    pairs = bw // LANES
    _fill_halo(xe_ref, x_ref, xp_ref, xn_ref, i, nblk)
    _fill_halo(le_ref, l_ref, lp_ref, ln_ref, i, nblk)

    sr = _iota2((LANES, LANES), 0) // CHUNK
    sc = _iota2((LANES, LANES), 1) // CHUNK
    head_sel = jnp.where(sr == sc, 1.0, 0.0).astype(BF16)
    lane = _iota2((1, LANES), 1)

    def head_sum(a, pieces):
        return jnp.concatenate(
            [_dot_sel_r(a[:, p * LANES:(p + 1) * LANES], head_sel, pieces) for p in range(pairs)], axis=1)

    def elementwise_items(c):
        r0 = c * CHUNK
        rs = slice(r0, r0 + CHUNK)
        st = {}

        def shifted(e_ref, mu, cols):
            x = e_ref[HALO + r0:HALO + r0 + CHUNK, cols]
            nbr = 0.5 * (e_ref[HALO - 1 + r0:HALO - 1 + r0 + CHUNK, cols]
                         + e_ref[HALO + 1 + r0:HALO + 1 + r0 + CHUNK, cols])
            return x + (nbr - x) * mu[:, cols]

        def shift_rv():
            st["r"] = shifted(xe_ref, mux_ref, slice(0, bw))
            st["v"] = shifted(xe_ref, mux_ref, slice(2 * bw, 3 * bw))
            r_scr[rs, :] = st["r"]
            v_scr[rs, :] = st["v"]

        def shift_k():
            st["k"] = shifted(xe_ref, mux_ref, slice(bw, 2 * bw))
            kkp = st["k"] * kk_w_ref[...]
            st["kk"] = kkp * lax.rsqrt(head_sum(kkp * kkp, 1) + L2_EPS)
            kk_scr[rs, :] = st["kk"]
            st["kb"] = jnp.zeros_like(st["k"])

        def lora():
            lo = shifted(le_ref, mul_ref, slice(None))
            l01 = lo[:, :LANES]
            xin = jnp.where(lane < 2 * lora_w, jnp.tanh(l01), l01)
            st["wa"] = _dot(xin, wcat_ref[...]) + bias_ref[...]
            gate_ref[0, rs, :] = _dot(_sigmoid(lo[:, LANES:]), g2_ref[...])

        def direction(e):
            wa = st["wa"]
            a = _sigmoid(wa[:, (2 + e) * bw:(3 + e) * bw])
            lw_scr[e, rs, :] = -B_DECAY_SCALE * _sigmoid(wa[:, e * bw:(e + 1) * bw])
            kka_scr[e, rs, :] = st["kk"] * a
            kd = st["k"] * (1.0 + (a - 1.0) * ka_ref[...])
            kd_scr[e, rs, :] = kd
            st["kb"] = st["kb"] + 0.5 * kd

        def bonus():
            bonus_ref[0, rs, :] = head_sum(st["r"] * st["kb"] * rk_ref[...], 2) * st["v"]

        return [shift_rv, shift_k, lora, functools.partial(direction, 0), functools.partial(direction, 1), bonus]

    lanes2 = _iota2((1, LANES), 1)
    m0 = lanes2 < CHUNK
    tr = _iota2((CHUNK, CHUNK), 0)
    tc = _iota2((CHUNK, CHUNK), 1)
    cum_sel = (jnp.where(tc <= tr, 1.0, 0.0).astype(BF16), jnp.where(tc >= tr, 1.0, 0.0).astype(BF16))
    masks = (_unit_masks(False, False), _unit_masks(True, True))

    def stack(a):
        return jnp.concatenate([jnp.where(m0, a, 0.0), jnp.where(m0, 0.0, a)], axis=0)

    def halves(a):
        return a[:CHUNK] + a[CHUNK:]

    nu = 2 * pairs

    def stage_items(c, base):
        rows_c = slice(c * CHUNK, (c + 1) * CHUNK)
        cum = {}

        def cumulate(e):
            cum[e] = _dot_sel(cum_sel[e], lw_scr[e, rows_c, :])

        def unit_steps(e, p):
            incl, strict, blk, same, eye = masks[e]
            last = CHUNK - 1 if e == 0 else 0
            ls = slice(p * LANES, (p + 1) * LANES)
            u = base + e * pairs + p
            st = {}

            def decayed_rows():
                cs = cum[e][:, ls]
                e_neg = jnp.exp(-cs)
                st["xs"] = stack(-kk_scr[rows_c, ls] * jnp.exp(cs - lw_scr[e, rows_c, ls])).astype(BF16)
                st["bk"] = jnp.concatenate([stack(kka_scr[e, rows_c, ls] * e_neg), stack(kd_scr[e, rows_c, ls] * e_neg)],
                                           axis=0).astype(BF16)
                sol.set_rhs(u, 0, st["xs"])

            def pair_a():
                ga = _dot_nt(st["xs"], st["bk"])
                sol.stage(u, jnp.where(strict, ga[:, :UNIT], 0.0), blk, eye)
                kr_scr[u] = jnp.where(strict, ga[:, UNIT:], 0.0).astype(BF16)

            def pair_r():
                rt = r_scr[rows_c, ls] * jnp.exp(cum[e][:, ls])
                gr = _dot_nt(stack(rt), st["bk"])
                aq_ref[e, 0, c, p] = jnp.concatenate(
                    [halves(jnp.where(incl, gr[:, :UNIT], 0.0)), halves(jnp.where(incl, gr[:, UNIT:], 0.0))],
                    axis=1).astype(BF16)
                wq_ref.at[e, 0, c, p][CHUNK:, :] = rt.astype(BF16)

            def seq_operands():
                cs = cum[e][:, ls]
                cl = cum[e][last:last + 1, ls]
                e_last = jnp.exp(cl - cs)
                vv = v_scr[rows_c, ls]
                vs_scr[u] = stack(vv).astype(BF16)
                vb_ref[e, 0, c, p] = vv.astype(BF16)
                bk_ref[e, 0, c, p] = jnp.concatenate(
                    [kka_scr[e, rows_c, ls] * e_last, kd_scr[e, rows_c, ls] * e_last], axis=0).T.astype(BF16)
                gm_ref[e, 0, c, p] = jnp.broadcast_to(jnp.exp(cl), (GAM_ROWS, LANES))

            return [decayed_rows, pair_a, pair_r, seq_operands]

        items = []
        for e in range(2):
            items.append(functools.partial(cumulate, e))
            for p in range(pairs):
                items += unit_steps(e, p)
        return items

    def solve_items(base):
        def av(u):
            sol.set_rhs(u, 1, jnp.dot(kr_scr[u], vs_scr[u], preferred_element_type=F32))

        return ([functools.partial(av, u) for u in range(base, base + nu)]
                + sol.items(range(base, base + nu)))

    def finish_items(c, base):
        def unit(e, p):
            u = base + e * pairs + p
            wq_ref.at[e, 0, c, p][:CHUNK, :] = halves(sol.result(u, 0)).astype(BF16)
            u0_ref[e, 0, c, p] = halves(sol.result(u, 1))

        return [functools.partial(unit, e, p) for e in range(2) for p in range(pairs)]

    n_chunks = rows // CHUNK
    _emit(elementwise_items(0) + stage_items(0, 0))
    for c in range(n_chunks):
        base, other = (c % 2) * nu, ((c + 1) % 2) * nu
        side = finish_items(c - 1, other) if c > 0 else []
        if c + 1 < n_chunks:
            side = side + elementwise_items(c + 1) + stage_items(c + 1, other)
        _emit(solve_items(base), side)
    _emit(finish_items(n_chunks - 1, ((n_chunks - 1) % 2) * nu))


def _rwkv_prep(rkv, lora, mu_x, mu_l, wcat, bias, g2p, k_k, k_a, r_k, lora_w, lora_a, rows=512):
    b, t, width = rkv.shape
    bw = width // 3
    pairs = bw // LANES
    lw_ = lora.shape[2]
    n = t // CHUNK
    nb = rows // CHUNK
    unit = lambda r, dt, w=LANES: jax.ShapeDtypeStruct((2, b, n, pairs, r, w), dt)
    ospec = lambda r, w=LANES: pl.BlockSpec((2, 1, nb, pairs, r, w), lambda bi, i: (0, bi, i, 0, 0, 0))
    row_spec = pl.BlockSpec((1, rows, bw), lambda bi, i: (bi, i, 0))
    return pl.pallas_call(
        functools.partial(_rwkv_prep_kernel, lora_w=lora_w, lora_a=lora_a),
        grid=(b, t // rows),
        in_specs=_halo_specs(rows, width, t) + _halo_specs(rows, lw_, t) + [
            _const_spec((1, width)), _const_spec((1, lw_)), _const_spec(wcat.shape), _const_spec(bias.shape),
            _const_spec(g2p.shape), _const_spec((1, bw)), _const_spec((1, bw)), _const_spec((1, bw))],
        out_specs=[ospec(UNIT), ospec(CHUNK), ospec(CHUNK, 2 * LANES), ospec(UNIT), ospec(CHUNK),
                   ospec(GAM_ROWS), row_spec, row_spec],
        out_shape=[unit(UNIT, BF16), unit(CHUNK, F32), unit(CHUNK, BF16, 2 * LANES), unit(UNIT, BF16),
                   unit(CHUNK, BF16), unit(GAM_ROWS, F32),
                   jax.ShapeDtypeStruct((b, t, bw), F32), jax.ShapeDtypeStruct((b, t, bw), F32)],
        scratch_shapes=[pltpu.VMEM((rows + 2 * HALO, width), F32), pltpu.VMEM((rows + 2 * HALO, lw_), F32)]
        + [pltpu.VMEM((rows, bw), F32)] * 3 + [pltpu.VMEM((2, rows, bw), F32)] * 3
        + [pltpu.VMEM((4 * pairs, UNIT, UNIT), BF16), pltpu.VMEM((4 * pairs, UNIT, LANES), BF16)]
        + _UnitSolver.scratch(4 * pairs),
        compiler_params=_params(("parallel", "arbitrary"), rows * width * 4, rows * lw_ * 4, wcat.size * 2,
                                2 * pairs * nb * (UNIT * 2 + CHUNK * 4 + CHUNK * 4 + UNIT * 2 + CHUNK * 2 + UNIT * 4)
                                * LANES, 2 * rows * bw * 4,
                                scratch=(rows + 2 * HALO) * (width + lw_) * 4 + 9 * rows * bw * 4
                                + 4 * pairs * (_UnitSolver.BYTES_PER_UNIT + 2 * UNIT * LANES * 2),
                                temps=8 * rows * bw * 4 * 4),
        name="rwkv_prep",
    )(rkv, rkv, rkv, lora, lora, lora, mu_x, mu_l, wcat, bias, g2p, k_k, k_a, r_k)


def _seq_kernel(*refs, paired, nops):
    dir_ops = (refs[:nops], refs[nops:2 * nops])
    y_refs = refs[2 * nops:2 * nops + 2]
    s_ref, pq_ref = refs[2 * nops + 2:]
    nb_, nu_ = dir_ops[0][0].shape[1], dir_ops[0][0].shape[3]
    units = [(d, bi, ui) for d in range(2) for bi in range(nb_) for ui in range(nu_)]
    at = lambda ref, bi, ui: ref[0, bi, 0, ui]

    @pl.when(pl.program_id(0) == 0)
    def _():
        s_ref[...] = jnp.zeros_like(s_ref)

    m0 = _iota2((1, LANES), 1) < CHUNK
    same = (_iota2((LANES, LANES), 0) // CHUNK) == (_iota2((LANES, LANES), 1) // CHUNK)
    for j, (d, bi, ui) in enumerate(units):
        pq_ref[j] = jnp.dot(at(dir_ops[d][0], bi, ui), s_ref[j].astype(BF16), preferred_element_type=F32)
    for j, (d, bi, ui) in enumerate(units):
        wq_ref, u_ref, aq_ref, bt_ref, gm_ref = dir_ops[d][:5]
        s = s_ref[j]
        pq = pq_ref[j]
        un = at(u_ref, bi, ui) + pq[:CHUNK]
        unb = un.astype(BF16)
        if paired:
            vb = at(dir_ops[d][5], bi, ui)
            zero = jnp.zeros_like(unb)
            rhs_y = jnp.concatenate([jnp.where(m0, unb, zero), jnp.where(m0, zero, unb),
                                     jnp.where(m0, vb, zero), jnp.where(m0, zero, vb)], axis=0)
            rhs_s = jnp.concatenate([unb, vb], axis=0)
        else:
            rhs_y = jnp.concatenate([unb, unb], axis=0)
            rhs_s = unb
        y_refs[d][bi, 0, ui] = pq[CHUNK:] + jnp.dot(at(aq_ref, bi, ui), rhs_y, preferred_element_type=F32)
        upd = jnp.dot(at(bt_ref, bi, ui), rhs_s, preferred_element_type=F32)
        if paired:
            upd = jnp.where(same, upd, 0.0)
        gam = at(gm_ref, bi, ui).T[:, 0:1]
        s_ref[j] = s * jnp.broadcast_to(gam, (LANES, LANES)) + upd


def _seq(ops, paired):
    _, b, n, nu = ops[0].shape[:4]
    ub = 2 * b * nu
    fwd = lambda ni: (0, 0, ni, 0, 0, 0)
    bwd = lambda ni: (1, 0, n - 1 - ni, 0, 0, 0)
    specs = [pl.BlockSpec((1, b, 1, nu) + a.shape[4:], imap) for imap in (fwd, bwd) for a in ops]
    block_bytes = [ub * math.prod(a.shape[4:]) * a.dtype.itemsize for a in ops]
    y_shape = jax.ShapeDtypeStruct((b, n, nu, CHUNK, LANES), F32)
    return pl.pallas_call(
        functools.partial(_seq_kernel, paired=paired, nops=len(ops)),
        grid=(n,),
        in_specs=specs,
        out_specs=[pl.BlockSpec((b, 1, nu, CHUNK, LANES), lambda ni: (0, ni, 0, 0, 0)),
                   pl.BlockSpec((b, 1, nu, CHUNK, LANES), lambda ni: (0, n - 1 - ni, 0, 0, 0))],
        out_shape=[y_shape, y_shape],
        scratch_shapes=[pltpu.VMEM((ub, LANES, LANES), F32), pltpu.VMEM((ub, UNIT, LANES), F32)],
        compiler_params=_params(("arbitrary",), *block_bytes, ub * CHUNK * LANES * 4,
                                scratch=ub * (LANES + UNIT) * LANES * 4, temps=ub * UNIT * LANES * 4),
        name="seq_rwkv" if paired else "seq_delta",
    )(*ops, *ops)


def _even_out_kernel(of_ref, ob_ref, yf_ref, yb_ref, z_ref, bonus_ref, gate_ref, h_ref, an_ref, lnw_ref, lnb_ref,
                     wo_ref, p_ref, nw_ref, wg_ref, wu_ref, wd_ref, npl_ref, wp_ref, wpg_ref, *rest):
    nf_ref, o_ref = rest if len(rest) == 2 else (None, rest[0])
    rows = h_ref.shape[1]
    a_heads = of_ref.shape[2]
    pairs = yf_ref.shape[2]
    z = z_ref[0]
    pieces = []
    for hh in range(a_heads):
        o = (of_ref[0, :, hh] + ob_ref[0, :, hh]).reshape(rows, LANES)
        pieces.append(_rms(o, an_ref[...]) * _silu(z[:, hh * LANES:(hh + 1) * LANES]))
    sr = _iota2((LANES, LANES), 0) // CHUNK
    sc = _iota2((LANES, LANES), 1) // CHUNK
    head_sel = jnp.where(sr == sc, 1.0, 0.0).astype(BF16)
    inv_n = 1.0 / CHUNK
    for p in range(pairs):
        ls = slice(p * LANES, (p + 1) * LANES)
        y = (yf_ref[0, :, p] + yb_ref[0, :, p]).reshape(rows, LANES)
        mu = _dot_sel_r(y, head_sel, 2) * inv_n
        yc = y - mu
        var = _dot_sel_r(yc * yc, head_sel, 2) * inv_n
        yn = yc * lax.rsqrt(var + B_GN_EPS) * lnw_ref[:, ls] + lnb_ref[:, ls]
        pieces.append((yn + bonus_ref[0][:, ls]) * gate_ref[0][:, ls])
    mix = jnp.concatenate(pieces, axis=1).astype(BF16)
    h = h_ref[0] + jnp.dot(mix, wo_ref[...], preferred_element_type=F32)
    o_ref[0] = _ffn_ple_math(h, p_ref[0], nw_ref, wg_ref, wu_ref, wd_ref, npl_ref, wp_ref, wpg_ref, nf_ref)


def _even_out(od, yr, z, bonus, gate, h3, a_norm, ln_w, ln_b, wo, p_sel, ffn, nf=None, rows=256):
    b, t, d = h3.shape
    nb = rows // CHUNK
    a_heads, pairs = od[0].shape[2], yr[0].shape[2]
    aw, bw = z.shape[2], bonus.shape[2]
    p_all, layer = p_sel
    pd = p_all.shape[-1]
    pspec = pl.BlockSpec((None, 1, rows, pd), lambda bi, i: (layer, bi, i, 0))
    nw, wg, wu, wd, npl, wp, wpg = ffn
    ff = wg.shape[1]
    dspec = lambda nh: pl.BlockSpec((1, nb, nh, CHUNK, LANES), lambda bi, i: (bi, i, 0, 0, 0))
    rspec = lambda w: pl.BlockSpec((1, rows, w), lambda bi, i: (bi, i, 0))
    ins = [od[0], od[1], yr[0], yr[1], z, bonus, gate, h3, a_norm, ln_w, ln_b, wo, p_all, nw, wg, wu, wd, npl, wp,
           wpg]
    specs = [dspec(a_heads), dspec(a_heads), dspec(pairs), dspec(pairs), rspec(aw), rspec(bw),
             rspec(bw), rspec(d), _const_spec((1, LANES)), _const_spec((1, bw)), _const_spec((1, bw)),
             _const_spec(wo.shape, True), pspec, _const_spec((1, d)),
             _const_spec((d, ff), True), _const_spec((d, ff), True), _const_spec((ff, d), True),
             _const_spec((1, d)), _const_spec((pd, d), True), _const_spec((d, d), True)]
    if nf is not None:
        ins.append(nf)
        specs.append(_const_spec((1, d)))
    weights = (wo.size + 3 * d * ff + pd * d + d * d) * 2
    return pl.pallas_call(
        _even_out_kernel,
        grid=(b, t // rows),
        in_specs=specs,
        out_specs=rspec(d),
        out_shape=jax.ShapeDtypeStruct((b, t, d), F32),
        compiler_params=_params(("parallel", "parallel"), 2 * rows * (aw + bw) * 4,
                                rows * (aw + 2 * bw + 2 * d + pd) * 4, scratch=weights,
                                temps=rows * ff * (4 + 4 + 2) + 8 * rows * d * 4),
        name="even_out_ffn",
    )(*ins)


def _even_layer(h3, nw, w_in, w_out, a_conv, a_log, a_dt_bias, a_norm, b_mu, b_w0, b_w2, b_a0, b_a2, b_g2,
                b_k_k, b_k_a, b_r_k, b_ln_w, b_ln_b, p_sel, ffn, nf=None):
    b, t, d = h3.shape
    a_heads = a_log.shape[1]
    hd = a_norm.shape[0]
    aw = a_heads * hd
    b_heads, b_hd = b_r_k.shape
    bw = b_heads * b_hd
    lora_w, lora_a, lora_g = b_w2.shape[1], b_a2.shape[1], b_g2.shape[0]
    assert b_hd == CHUNK and hd == LANES and 2 * (lora_w + lora_a) == LANES and lora_g <= LANES
    a_cols = 4 * aw + 4 * a_heads
    n_gate = 4 * a_heads
    pad_cols = lambda w, n: jnp.pad(w, ((0, 0), (0, n - w.shape[1])))
    w_bf = w_in.astype(BF16)
    wa, wb = w_bf[:, :a_cols], w_bf[:, a_cols:]
    w_all = jnp.concatenate([wa[:, :4 * aw], pad_cols(wa[:, 4 * aw:], LANES),
                             wb[:, :3 * bw], pad_cols(wb[:, 3 * bw:], 2 * LANES)], axis=1)
    splits = (3 * aw, aw, LANES, 3 * bw, 2 * LANES)
    qkv, z, ab, rkv, lora = _proj(h3.reshape(b * t, d), nw.reshape(1, d), w_all, splits)
    to3 = lambda a: a.reshape(b, t, a.shape[-1])

    alog_row = jnp.zeros((1, LANES), F32).at[0, :2 * a_heads].set(a_log.reshape(-1))
    dtb_row = jnp.zeros((1, LANES), F32).at[0, :2 * a_heads].set(a_dt_bias.reshape(-1))
    d_ops = _delta_prep(to3(qkv), to3(ab), a_conv, alog_row, dtb_row, a_heads)
    od = _seq(d_ops, paired=False)

    wcat = jnp.zeros((LANES, 4 * bw), F32)
    for e in range(2):
        wcat = wcat.at[e * lora_w:(e + 1) * lora_w, e * bw:(e + 1) * bw].set(b_w2[e])
        wcat = wcat.at[2 * lora_w + e * lora_a:2 * lora_w + (e + 1) * lora_a, (2 + e) * bw:(3 + e) * bw].set(b_a2[e])
    bias = jnp.concatenate([b_w0[0], b_w0[1], b_a0[0], b_a0[1]]).reshape(1, 4 * bw)
    g2p = jnp.zeros((LANES, bw), F32).at[:lora_g].set(b_g2).astype(BF16)
    mu_x = b_mu[:3 * bw].reshape(1, 3 * bw)
    mu_l = jnp.pad(b_mu[3 * bw:], (0, 2 * LANES - (b_mu.shape[0] - 3 * bw))).reshape(1, 2 * LANES)
    row = lambda a: a.reshape(1, bw)
    r_ops = _rwkv_prep(to3(rkv), to3(lora), mu_x, mu_l, wcat.astype(BF16), bias, g2p, row(b_k_k), row(b_k_a),
                       row(b_r_k), lora_w, lora_a)
    wq, u0, aq, bk, vb, gm, bonus, gate = r_ops
    yr = _seq([wq, u0, aq, bk, gm, vb], paired=True)

    return _even_out(od, yr, to3(z), bonus, gate, h3, a_norm.reshape(1, hd), row(b_ln_w), row(b_ln_b),
                     w_out.astype(BF16), p_sel, ffn, nf)


def _odd_layer(h3, nw, w_in, ln_w, ln_b, ws, bs, w_out):
    b, t, d = h3.shape
    groups, chunk, _ = ws.shape
    cw = w_out.shape[0]
    bs_full = jnp.repeat(bs.T, cw // groups, axis=1)
    out = _gmlp(h3.reshape(b * t, d), nw.reshape(1, d), w_in.astype(BF16), ln_w.reshape(1, cw), ln_b.reshape(1, cw),
                ws.astype(BF16), bs_full, w_out.astype(BF16))
    return out.reshape(b, t, d)


def kernel(x, p, norm_mix, norm_ffn, norm_ple, norm_final, w_in_even, w_out_even, a_conv, a_log, a_dt_bias, a_norm,
           b_mu, b_w0, b_w2, b_a0, b_a2, b_g2, b_k_k, b_k_a, b_r_k, b_ln_w, b_ln_b, w_in_odd, c_ln_w, c_ln_b, c_ws,
           c_bs, w_out_odd, w_gate, w_up, w_down, w_ple, w_ple_gate):
    b, t, d = x.shape
    depth = p.shape[0]
    h = x
    for i in range(depth):
        j = i // 2
        ffn = (norm_ffn[i].reshape(1, d), w_gate[i].astype(BF16), w_up[i].astype(BF16), w_down[i].astype(BF16),
               norm_ple[i].reshape(1, d), w_ple[i].astype(BF16), w_ple_gate[i].astype(BF16))
        nf = norm_final.reshape(1, d) if i == depth - 1 else None
        if i % 2 == 0:
            h = _even_layer(h, norm_mix[i], w_in_even[j], w_out_even[j], a_conv[j], a_log[j], a_dt_bias[j], a_norm[j],
                            b_mu[j], b_w0[j], b_w2[j], b_a0[j], b_a2[j], b_g2[j], b_k_k[j], b_k_a[j], b_r_k[j],
                            b_ln_w[j], b_ln_b[j], (p, i), ffn, nf)
        else:
            h = _odd_layer(h, norm_mix[i], w_in_odd[j], c_ln_w[j], c_ln_b[j], c_ws[j], c_bs[j], w_out_odd[j])
            h = _ffn_ple(h.reshape(b * t, d), (p, i), *ffn, nf).reshape(b, t, d)
    return h
```
